```python
import math
import jax, jax.numpy as jnp
from jax import lax
import numpy as np

D_MODEL = 2048
BATCH = 4
SEQ = 2048
DEPTH = 2
DEC_BATCH = 1
DEC_SEQ = 16384
PAST_LEN = 128

HEAD_DIM = 128
GRID_W = 64
EPS = 1e-6
ROPE_THETA = 10000.0
NEG_INF = -1e30

DIL_CONFIGS = ((128, 1), (512, 4), (2048, 16))
A_GROUPS = 3
A_HEADS_PER_GROUP = 6
A_HEADS = A_GROUPS * A_HEADS_PER_GROUP
A_BLOCK = 64
T5_BUCKETS = 32
T5_MAX_DIST = 1024

B_Q_HEADS = 6
B_KV_HEADS = 2
B_BLOCK = 128

C_HEADS = 6
NA_ROWS = 8
NA_COLS = 16

D_HEADS = 4
D_QK = 128
D_V = 256
RET_CHUNK = 128

N_EXPERTS = 32
TOP_K = 4
D_FF = 2048
SWIGLU_LIMIT = 7.0
SWIGLU_ALPHA = 1.702

N_BRANCH = 4
A_OUT = A_HEADS_PER_GROUP * HEAD_DIM
B_OUT = B_Q_HEADS * HEAD_DIM
C_OUT = C_HEADS * HEAD_DIM
D_OUT = D_HEADS * D_V
BRANCH_OUT = (A_OUT, B_OUT, C_OUT, D_OUT)
MIX_WIDTH = A_OUT + B_OUT + C_OUT + D_OUT

IN_SPLITS = (A_HEADS * HEAD_DIM, A_HEADS * HEAD_DIM, A_HEADS * HEAD_DIM,
             B_Q_HEADS * HEAD_DIM, B_KV_HEADS * HEAD_DIM, B_KV_HEADS * HEAD_DIM,
             C_HEADS * HEAD_DIM, C_HEADS * HEAD_DIM, C_HEADS * HEAD_DIM,
             D_HEADS * D_QK, D_HEADS * D_QK, D_HEADS * D_V, D_HEADS * D_V,
             N_BRANCH * D_MODEL)
N_IN = sum(IN_SPLITS)

kernel_name = 'hybrid_gated_parallel_encoder'


def _rms_norm(x, g):
    xf = x.astype(jnp.float32)
    y = xf * lax.rsqrt(jnp.mean(xf * xf, axis=-1, keepdims=True) + EPS)
    return (y * g.astype(jnp.float32)).astype(x.dtype)


def _rope(x, pos):
    half = x.shape[-1] // 2
    freqs = ROPE_THETA ** (-jnp.arange(half, dtype=jnp.float32) / half)
    ang = pos.astype(jnp.float32)[:, None] * freqs[None, :]
    cos = jnp.cos(ang)[:, None, :]
    sin = jnp.sin(ang)[:, None, :]
    xf = x.astype(jnp.float32)
    x1, x2 = xf[..., :half], xf[..., half:]
    return jnp.concatenate([x1 * cos - x2 * sin, x2 * cos + x1 * sin], axis=-1).astype(x.dtype)


def _axial_rope(x, rows, cols):
    half = x.shape[-1] // 2
    return jnp.concatenate([_rope(x[..., :half], rows), _rope(x[..., half:], cols)], axis=-1)


def _t5_bucket(rel):
    nb = T5_BUCKETS // 2
    max_exact = nb // 2
    n = jnp.abs(rel)
    large = max_exact + (jnp.log(jnp.maximum(n, 1).astype(jnp.float32) / max_exact)
                         / math.log(T5_MAX_DIST / max_exact) * (nb - max_exact)).astype(jnp.int32)
    large = jnp.minimum(large, nb - 1)
    return jnp.where(rel > 0, nb, 0) + jnp.where(n < max_exact, n, large)


def _strided_window_attn(q, k, v, d, half, bias_table):
    B, S, H, Dh = q.shape
    L = S // d
    to_cls = lambda z: z.reshape(B, L, d, H, Dh).transpose(0, 2, 1, 3, 4)
    qc, kc, vc = to_cls(q), to_cls(k), to_cls(v)
    bq = math.gcd(L, A_BLOCK)
    nb = L // bq
    span = bq + 2 * half
    pad = ((0, 0), (0, 0), (half, half), (0, 0), (0, 0))
    kp, vp = jnp.pad(kc, pad), jnp.pad(vc, pad)
    idx = (jnp.arange(nb) * bq)[:, None] + jnp.arange(span)[None, :]
    kb, vb = kp[:, :, idx], vp[:, :, idx]
    qb = qc.reshape(B, d, nb, bq, H, Dh)
    logits = jnp.einsum('bcnqhe,bcnkhe->bcnhqk', qb, kb, preferred_element_type=jnp.float32) * (Dh ** -0.5)
    qi = (jnp.arange(nb) * bq)[:, None] + jnp.arange(bq)[None, :]
    ki = idx - half
    rel = ki[:, None, :] - qi[:, :, None]
    valid = (jnp.abs(rel) <= half) & (ki[:, None, :] >= 0) & (ki[:, None, :] < L)
    bias = bias_table[_t5_bucket(rel * d)].astype(jnp.float32).transpose(0, 3, 1, 2)
    logits = jnp.where(valid[None, None, :, None], logits + bias[None, None], NEG_INF)
    m = jnp.max(logits, axis=-1, keepdims=True)
    p = jnp.exp(logits - m)
    l = jnp.sum(p, axis=-1)
    o = jnp.einsum('bcnhqk,bcnkhe->bcnqhe', p, vb.astype(jnp.float32)) / l.transpose(0, 1, 2, 4, 3)[..., None]
    lse = (m[..., 0] + jnp.log(l)).transpose(0, 1, 2, 4, 3)
    o = o.reshape(B, d, L, H, Dh).transpose(0, 2, 1, 3, 4).reshape(B, S, H, Dh)
    lse = lse.reshape(B, d, L, H).transpose(0, 2, 1, 3).reshape(B, S, H)
    return o, lse


def _dilated_attention(q, k, v, t5_bias):
    B, S = q.shape[:2]
    outs, lses = [], []
    for g, (w, d) in enumerate(DIL_CONFIGS):
        hs = slice(g * A_HEADS_PER_GROUP, (g + 1) * A_HEADS_PER_GROUP)
        o, lse = _strided_window_attn(q[:, :, hs], k[:, :, hs], v[:, :, hs], d, w // (2 * d), t5_bias[:, hs])
        outs.append(o)
        lses.append(lse)
    wts = jax.nn.softmax(jnp.stack(lses, axis=0), axis=0)
    o = jnp.sum(wts[..., None] * jnp.stack(outs, axis=0), axis=0)
    return o.reshape(B, S, A_OUT)


def _axial_gqa(q, k, v, qk_g, rows, cols):
    B, S = q.shape[:2]
    q = _axial_rope(_rms_norm(q, qk_g[0]), rows, cols)
    k = _axial_rope(_rms_norm(k, qk_g[1]), rows, cols)
    rep = B_Q_HEADS // B_KV_HEADS
    nblk = S // B_BLOCK
    qb = q.reshape(B, nblk, B_BLOCK, B_KV_HEADS, rep, HEAD_DIM).transpose(1, 0, 2, 3, 4, 5)
    scale = HEAD_DIM ** -0.5

    def block(qi):
        s = jnp.einsum('bqgrd,bkgd->bgrqk', qi, k, preferred_element_type=jnp.float32) * scale
        p = jax.nn.softmax(s, axis=-1)
        return jnp.einsum('bgrqk,bkgd->bqgrd', p.astype(v.dtype), v)

    o = lax.map(block, qb)
    return o.transpose(1, 0, 2, 3, 4, 5).reshape(B, S, B_OUT)


def _neighbourhood_attention(q, k, v, rpb):
    B, S, H, Dh = q.shape
    rows_n = S // GRID_W
    kr = min(NA_ROWS, rows_n)
    kc = NA_COLS
    qg = q.reshape(B, rows_n, GRID_W, H, Dh)
    kg = k.reshape(B, rows_n, GRID_W, H, Dh)
    vg = v.reshape(B, rows_n, GRID_W, H, Dh)
    rows = jnp.arange(rows_n)
    rstart = jnp.clip(rows - kr // 2, 0, rows_n - kr)
    ridx = rstart[:, None] + jnp.arange(kr)[None, :]
    kb, vb = kg[:, ridx], vg[:, ridx]
    logits = jnp.einsum('brqhe,brkwhe->brhqkw', qg, kb, preferred_element_type=jnp.float32) * (Dh ** -0.5)
    cols = jnp.arange(GRID_W)
    cstart = jnp.clip(cols - kc // 2, 0, GRID_W - kc)
    col_valid = (cols[None, :] >= cstart[:, None]) & (cols[None, :] < cstart[:, None] + kc)
    dr = ridx - rows[:, None]
    dc = jnp.clip(cols[None, :] - cols[:, None], -(NA_COLS - 1), NA_COLS - 1)
    ri = (dr + NA_ROWS - 1)[:, None, :, None]
    ci = (dc + NA_COLS - 1)[None, :, None, :]
    bias = rpb[:, ri, ci].astype(jnp.float32).transpose(1, 0, 2, 3, 4)
    logits = jnp.where(col_valid[None, None, None, :, None, :], logits + bias[None], NEG_INF)
    p = jax.nn.softmax(logits.reshape(B, rows_n, H, GRID_W, kr * GRID_W), axis=-1)
    p = p.reshape(B, rows_n, H, GRID_W, kr, GRID_W)
    o = jnp.einsum('brhqkw,brkwhe->brqhe', p, vb.astype(jnp.float32))
    return o.reshape(B, S, C_OUT)


def _retention_dir(q, k, v, lg, include_diag):
    B, S, H, dk = q.shape
    dv = v.shape[-1]
    C = RET_CHUNK
    N = S // C
    qc = q.reshape(B, N, C, H, dk)
    kc = k.reshape(B, N, C, H, dk)
    vc = v.reshape(B, N, C, H, dv)
    j = jnp.arange(C, dtype=jnp.float32)
    diff = j[:, None] - j[None, :]
    allowed = (diff >= 0) if include_diag else (diff > 0)
    dmat = jnp.where(allowed[None], jnp.exp(jnp.maximum(diff, 0.0)[None] * lg[:, None, None]), 0.0)
    scores = jnp.einsum('bnihd,bnjhd->bnhij', qc, kc) * dmat
    inner = jnp.einsum('bnhij,bnjhe->bnihe', scores, vc)
    k_dec = kc * jnp.exp((C - 1 - j)[:, None] * lg[None, :])[:, :, None]
    chunk_kv = jnp.einsum('bnjhd,bnjhe->nbhde', k_dec, vc)
    g_chunk = jnp.exp(C * lg)[None, :, None, None]

    def step(state, kv):
        return state * g_chunk + kv, state

    _, prev = lax.scan(step, jnp.zeros((B, H, dk, dv), jnp.float32), chunk_kv)
    q_dec = qc * jnp.exp((j + 1.0)[:, None] * lg[None, :])[:, :, None]
    cross = jnp.einsum('bnihd,nbhde->bnihe', q_dec, prev)
    return (inner + cross).reshape(B, S, H, dv)


def _retention(q, k, v, g, decay_logit, pos):
    B, S = q.shape[:2]
    qf = _rope(q, pos).astype(jnp.float32)
    kf = _rope(k, pos).astype(jnp.float32) * (D_QK ** -0.5)
    vf = v.astype(jnp.float32)
    lg = jax.nn.log_sigmoid(decay_logit.astype(jnp.float32))
    fwd = _retention_dir(qf, kf, vf, lg[0], True)
    bwd = jnp.flip(_retention_dir(jnp.flip(qf, 1), jnp.flip(kf, 1), jnp.flip(vf, 1), lg[1], False), 1)
    o = fwd + bwd
    mu = jnp.mean(o, axis=-1, keepdims=True)
    var = jnp.mean(jnp.square(o - mu), axis=-1, keepdims=True)
    o = (o - mu) * lax.rsqrt(var + EPS)
    return (jax.nn.silu(g.astype(jnp.float32)) * o.reshape(B, S, D_OUT)).astype(g.dtype)


def _moe(h, w_router, b_router, w_up, b_up, w_down, b_down):
    B, S, D = h.shape
    t = h.reshape(B * S, D)
    logits = (t @ w_router + b_router).astype(jnp.float32)
    top_v, top_i = lax.top_k(logits, TOP_K)
    wts = jax.nn.softmax(top_v, axis=-1)
    combine = jnp.sum(jax.nn.one_hot(top_i, N_EXPERTS, dtype=jnp.float32) * wts[..., None], axis=1)
    out = jnp.zeros((B * S, D), jnp.float32)
    for e in range(N_EXPERTS):
        gu = t @ w_up[e] + b_up[e]
        glu = jnp.minimum(gu[:, 0::2], SWIGLU_LIMIT)
        lin = jnp.clip(gu[:, 1::2], -SWIGLU_LIMIT, SWIGLU_LIMIT)
        act = glu * jax.nn.sigmoid(SWIGLU_ALPHA * glu) * (lin + 1.0)
        out = out + combine[:, e:e + 1] * (act @ w_down[e] + b_down[e])
    return out.astype(h.dtype).reshape(B, S, D)


def _layer(x, c, t5_bias, w_mod, b_mod, gains, w_in, qk_g, rpb, dlogit, w_branch, w_out,
           w_router, b_router, w_up, b_up, w_down, b_down):
    B, S, _ = x.shape
    pos = jnp.arange(S)
    rows, cols = pos // GRID_W, pos % GRID_W
    mod = (jax.nn.silu(c) @ w_mod + b_mod)[:, None, :]
    sh1, sc1, gt1, sh2, sc2, gt2 = jnp.split(mod, 6, axis=-1)
    h = _rms_norm(x, gains[0]) * (1 + sc1) + sh1
    proj = h @ w_in
    pts, acc = [], 0
    for wdt in IN_SPLITS[:-1]:
        acc += wdt
        pts.append(acc)
    (aq, ak, av, bq, bk, bv, cq, ck, cv, dq, dk, dv, dg, gate_logits) = jnp.split(proj, pts, axis=-1)
    hd = lambda z, n: z.reshape(B, S, n, -1)
    o_a = _dilated_attention(hd(aq, A_HEADS), hd(ak, A_HEADS), hd(av, A_HEADS), t5_bias)
    o_b = _axial_gqa(hd(bq, B_Q_HEADS), hd(bk, B_KV_HEADS), hd(bv, B_KV_HEADS), qk_g, rows, cols)
    o_c = _neighbourhood_attention(hd(cq, C_HEADS), hd(ck, C_HEADS), hd(cv, C_HEADS), rpb)
    o_d = _retention(hd(dq, D_HEADS), hd(dk, D_HEADS), hd(dv, D_HEADS), dg, dlogit, pos)
    gates = jax.nn.sigmoid(gate_logits.astype(jnp.float32)).reshape(B, S, N_BRANCH, D_MODEL)
    terms, off = [], 0
    for i, o in enumerate((o_a, o_b, o_c, o_d)):
        wdt = BRANCH_OUT[i]
        terms.append(gates[:, :, i] * (o.astype(x.dtype) @ w_branch[off:off + wdt]))
        off += wdt
    merged = terms[0] + terms[1] + terms[2] + terms[3]
    y = merged.astype(x.dtype) @ w_out
    x = x + gt1 * _rms_norm(y, gains[1])
    h2 = _rms_norm(x, gains[2]) * (1 + sc2) + sh2
    f = _moe(h2, w_router, b_router, w_up, b_up, w_down, b_down)
    return x + gt2 * _rms_norm(f, gains[3])


def setup_inputs(seed: int = 0) -> dict:
    key = jax.random.key(seed)
    ks = jax.random.split(key, 20)
    f32 = jnp.float32
    nrm = lambda k, shape, s: jax.random.normal(k, shape, f32) * s
    base_decay = jnp.log(2.0 ** (5.0 + jnp.arange(D_HEADS, dtype=f32)) - 1.0)
    return {
        'x_prompt': nrm(ks[0], (BATCH, SEQ, D_MODEL), 1.0),
        'x_sample': nrm(ks[1], (DEC_BATCH, DEC_SEQ, D_MODEL), 1.0),
        'c_prompt': nrm(ks[2], (BATCH, D_MODEL), 1.0),
        'c_sample': nrm(ks[3], (DEC_BATCH, D_MODEL), 1.0),
        't5_bias': nrm(ks[4], (T5_BUCKETS, A_HEADS), 0.1),
        'w_mod': nrm(ks[5], (DEPTH, D_MODEL, 6 * D_MODEL), 0.5 * D_MODEL ** -0.5),
        'b_mod': nrm(ks[6], (DEPTH, 6 * D_MODEL), 0.01),
        'norm_gains': 1.0 + nrm(ks[7], (DEPTH, 4, D_MODEL), 0.05),
        'w_in': nrm(ks[8], (DEPTH, D_MODEL, N_IN), D_MODEL ** -0.5),
        'qk_norm_gains': 1.0 + nrm(ks[9], (DEPTH, 2, HEAD_DIM), 0.05),
        'na_rpb': nrm(ks[10], (DEPTH, C_HEADS, 2 * NA_ROWS - 1, 2 * NA_COLS - 1), 0.1),
        'ret_decay_logit': base_decay[None, None, :] + nrm(ks[11], (DEPTH, 2, D_HEADS), 0.1),
        'w_branch': nrm(ks[12], (DEPTH, MIX_WIDTH, D_MODEL), (MIX_WIDTH / N_BRANCH) ** -0.5),
        'w_out': nrm(ks[13], (DEPTH, D_MODEL, D_MODEL), D_MODEL ** -0.5),
        'w_router': nrm(ks[14], (DEPTH, D_MODEL, N_EXPERTS), D_MODEL ** -0.5),
        'b_router': nrm(ks[15], (DEPTH, N_EXPERTS), 0.01),
        'w_up': nrm(ks[16], (DEPTH, N_EXPERTS, D_MODEL, 2 * D_FF), D_MODEL ** -0.5),
        'b_up': nrm(ks[17], (DEPTH, N_EXPERTS, 2 * D_FF), 0.01),
        'w_down': nrm(ks[18], (DEPTH, N_EXPERTS, D_FF, D_MODEL), D_FF ** -0.5),
        'b_down': nrm(ks[19], (DEPTH, N_EXPERTS, D_MODEL), 0.01),
    }


def reference(x_prompt, x_sample, c_prompt, c_sample, t5_bias, w_mod, b_mod, norm_gains, w_in,
              qk_norm_gains, na_rpb, ret_decay_logit, w_branch, w_out, w_router, b_router,
              w_up, b_up, w_down, b_down):
    def run(x, c):
        for l in range(DEPTH):
            x = _layer(x, c, t5_bias, w_mod[l], b_mod[l], norm_gains[l], w_in[l], qk_norm_gains[l],
                       na_rpb[l], ret_decay_logit[l], w_branch[l], w_out[l], w_router[l], b_router[l],
                       w_up[l], b_up[l], w_down[l], b_down[l])
        return x

    y_prompt = run(x_prompt, c_prompt)
    y_sample = run(x_sample, c_sample)
    return (y_prompt, y_sample)
```

```python
import functools
import math
from typing import NamedTuple

import jax
import jax.numpy as jnp
from jax import lax
from jax.experimental import pallas as pl
from jax.experimental.pallas import tpu as pltpu

F32 = jnp.float32
BF16 = jnp.bfloat16
I32 = jnp.int32

D_MODEL = 2048
DEPTH = 2
HEAD_DIM = 128
GRID_W = 64
EPS = 1e-6
ROPE_THETA = 10000.0
NEG_INF = -1e30

DIL_CONFIGS = ((128, 1), (512, 4), (2048, 16))
A_GROUPS = 3
A_HEADS_PER_GROUP = 6
A_HEADS = A_GROUPS * A_HEADS_PER_GROUP
T5_BUCKETS = 32
T5_MAX_DIST = 1024
B_Q_HEADS = 6
B_KV_HEADS = 2
C_HEADS = 6
NA_ROWS = 8
NA_COLS = 16
D_HEADS = 4
D_QK = 128
D_V = 256
N_EXPERTS = 32
TOP_K = 4
D_FF = 2048
SWIGLU_LIMIT = 7.0
SWIGLU_ALPHA = 1.702
N_BRANCH = 4

A_OUT = A_HEADS_PER_GROUP * HEAD_DIM
B_OUT = B_Q_HEADS * HEAD_DIM
C_OUT = C_HEADS * HEAD_DIM
D_OUT = D_HEADS * D_V
MIX_WIDTH = A_OUT + B_OUT + C_OUT + D_OUT

IN_SPLITS = (A_HEADS * HEAD_DIM, A_HEADS * HEAD_DIM, A_HEADS * HEAD_DIM,
             B_Q_HEADS * HEAD_DIM, B_KV_HEADS * HEAD_DIM, B_KV_HEADS * HEAD_DIM,
             C_HEADS * HEAD_DIM, C_HEADS * HEAD_DIM, C_HEADS * HEAD_DIM,
             D_HEADS * D_QK, D_HEADS * D_QK, D_HEADS * D_V, D_HEADS * D_V,
             N_BRANCH * D_MODEL)
N_IN = sum(IN_SPLITS)
_OFF = [0]
for _w in IN_SPLITS:
    _OFF.append(_OFF[-1] + _w)
(OFF_AQ, OFF_AK, OFF_AV, OFF_BQ, OFF_BK, OFF_BV, OFF_CQ, OFF_CK, OFF_CV,
 OFF_DQ, OFF_DK, OFF_DV, OFF_DG, OFF_GATE) = _OFF[:-1]

SEG = 2048
LANES = 128
VMEM_LIMIT = 56 * 1024 * 1024


class Layout(NamedTuple):
    n_p: int
    s_p: int
    n_s: int
    s_s: int

    @property
    def t_p(self):
        return self.n_p * self.s_p

    @property
    def t(self):
        return self.n_p * self.s_p + self.n_s * self.s_s


def _cparams(*sem):
    return pltpu.CompilerParams(dimension_semantics=sem, vmem_limit_bytes=VMEM_LIMIT)


def _mod_kernel(c_ref, w_ref, b_ref, o_ref):
    c = c_ref[...]
    s = (c * jax.nn.sigmoid(c)).astype(BF16)
    o_ref[0] = jnp.dot(s, w_ref[0].astype(BF16), preferred_element_type=F32) + b_ref[0]


def _modulation(c8, w_mod, b_mod):
    tn = 1024
    n6 = 6 * D_MODEL
    return pl.pallas_call(
        _mod_kernel,
        grid=(DEPTH, n6 // tn),
        in_specs=[pl.BlockSpec((8, D_MODEL), lambda l, j: (0, 0)),
                  pl.BlockSpec((1, D_MODEL, tn), lambda l, j: (l, 0, j)),
                  pl.BlockSpec((1, 1, tn), lambda l, j: (l, 0, j))],
        out_specs=pl.BlockSpec((1, 8, tn), lambda l, j: (l, 0, j)),
        out_shape=jax.ShapeDtypeStruct((DEPTH, 8, n6), F32),
        compiler_params=_cparams("parallel", "parallel"),
        name="adaln_mod",
    )(c8, w_mod, b_mod.reshape(DEPTH, 1, n6))


def _inproj_kernel(x_ref, mod_ref, g_ref, w_ref, o_ref, h_ref):
    @pl.when(pl.program_id(1) == 0)
    def _():
        x = x_ref[...]
        y = x * lax.rsqrt(jnp.mean(x * x, axis=-1, keepdims=True) + EPS) * g_ref[...]
        m = mod_ref[0]
        h_ref[...] = (y * (1.0 + m[1:2]) + m[0:1]).astype(BF16)

    o_ref[...] = jnp.dot(h_ref[...], w_ref[...], preferred_element_type=F32).astype(o_ref.dtype)


def _norm_inproj(x, modseg, gain, w_in_bf16):
    t = x.shape[0]
    tm, tn = 1024, 1280
    assert t % tm == 0 and N_IN % tn == 0 and SEG % tm == 0
    return pl.pallas_call(
        _inproj_kernel,
        grid=(t // tm, N_IN // tn),
        in_specs=[pl.BlockSpec((tm, D_MODEL), lambda i, j: (i, 0)),
                  pl.BlockSpec((1, 8, D_MODEL), lambda i, j: (i * tm // SEG, 0, 0)),
                  pl.BlockSpec((1, D_MODEL), lambda i, j: (0, 0)),
                  pl.BlockSpec((D_MODEL, tn), lambda i, j: (0, j))],
        out_specs=pl.BlockSpec((tm, tn), lambda i, j: (i, j)),
        out_shape=jax.ShapeDtypeStruct((t, N_IN), BF16),
        scratch_shapes=[pltpu.VMEM((tm, D_MODEL), BF16)],
        compiler_params=_cparams("parallel", "arbitrary"),
        name="norm_inproj",
    )(x, modseg, gain.reshape(1, D_MODEL), w_in_bf16)


def _local_pos(lay):
    return jnp.concatenate([jnp.tile(jnp.arange(lay.s_p), lay.n_p), jnp.tile(jnp.arange(lay.s_s), lay.n_s)])


def _axial_tables(lay):
    pos = _local_pos(lay)
    lane = jnp.arange(LANES)
    quarter = HEAD_DIM // 4
    freqs = ROPE_THETA ** (-jnp.arange(quarter, dtype=F32) / quarter)
    f = freqs[lane % quarter]
    p = jnp.where(lane[None, :] < HEAD_DIM // 2, (pos // GRID_W)[:, None], (pos % GRID_W)[:, None]).astype(F32)
    ang = p * f[None, :]
    sign = jnp.where((lane % (2 * quarter)) < quarter, -1.0, 1.0).astype(F32)
    return jnp.cos(ang), jnp.sin(ang) * sign[None, :]


def _rope_tables(lay):
    pos = _local_pos(lay)
    lane = jnp.arange(LANES)
    half = D_QK // 2
    freqs = ROPE_THETA ** (-jnp.arange(half, dtype=F32) / half)
    ang = pos.astype(F32)[:, None] * freqs[lane % half][None, :]
    sign = jnp.where(lane < half, -1.0, 1.0).astype(F32)
    return jnp.cos(ang), jnp.sin(ang) * sign[None, :]


def _lane_iota(shape):
    return lax.broadcasted_iota(I32, shape, len(shape) - 1)


def _axial_rotate(x, cos, sin_signed):
    q = HEAD_DIM // 4
    lo = (_lane_iota(x.shape) % (2 * q)) < q
    partner = jnp.where(lo, pltpu.roll(x, LANES - q, 1), pltpu.roll(x, q, 1))
    return x * cos + partner * sin_signed


def _rope_rotate(x, cos, sin_signed):
    return x * cos + pltpu.roll(x, D_QK // 2, 1) * sin_signed


def _bprep_kernel(x_ref, g_ref, cos_ref, sin_ref, o_ref):
    x = x_ref[...].astype(F32)
    y = x * lax.rsqrt(jnp.mean(x * x, axis=-1, keepdims=True) + EPS) * g_ref[0]
    y = _axial_rotate(y, cos_ref[...], sin_ref[...])
    scale = jnp.where(pl.program_id(1) < B_Q_HEADS, HEAD_DIM ** -0.5, 1.0)
    o_ref[...] = (y * scale).astype(o_ref.dtype)


def _b_prepare(proj, qk_g, cos, sin):
    t = proj.shape[0]
    tr = 1024
    nh = B_Q_HEADS + B_KV_HEADS
    g8 = jnp.concatenate([jnp.tile(qk_g[0:1], (B_Q_HEADS, 1)), jnp.tile(qk_g[1:2], (B_KV_HEADS, 1))]).reshape(nh, 1, HEAD_DIM)
    return pl.pallas_call(
        _bprep_kernel,
        grid=(t // tr, nh),
        in_specs=[pl.BlockSpec((tr, HEAD_DIM), lambda i, h: (i, OFF_BQ // HEAD_DIM + h)),
                  pl.BlockSpec((1, 1, HEAD_DIM), lambda i, h: (h, 0, 0)),
                  pl.BlockSpec((tr, HEAD_DIM), lambda i, h: (i, 0)),
                  pl.BlockSpec((tr, HEAD_DIM), lambda i, h: (i, 0))],
        out_specs=pl.BlockSpec((tr, HEAD_DIM), lambda i, h: (i, h)),
        out_shape=jax.ShapeDtypeStruct((t, nh * HEAD_DIM), BF16),
        compiler_params=_cparams("parallel", "parallel"),
        name="b_prep",
    )(proj, g8, cos, sin)


def _flash_kernel(qt_ref, kt_ref, first_ref, last_ref, q_ref, k_ref, v_ref, o_ref, m_ref, l_ref, acc_ref):
    s_id = pl.program_id(1)
    rep = B_Q_HEADS // B_KV_HEADS

    @pl.when(first_ref[s_id] == 1)
    def _():
        m_ref[...] = jnp.full(m_ref.shape, -jnp.inf, F32)
        l_ref[...] = jnp.zeros(l_ref.shape, F32)
        acc_ref[...] = jnp.zeros(acc_ref.shape, F32)

    k = k_ref[...]
    v = v_ref[...]
    for r in range(rep):
        q = q_ref[:, r * HEAD_DIM:(r + 1) * HEAD_DIM]
        s = lax.dot_general(q, k, (((1,), (1,)), ((), ())), preferred_element_type=F32)
        m_prev = m_ref[r]
        m_cur = jnp.maximum(m_prev, jnp.max(s, axis=-1, keepdims=True))
        alpha = jnp.exp(m_prev - m_cur)
        p = jnp.exp(s - m_cur[:, 0:1])
        l_ref[r] = alpha * l_ref[r] + jnp.sum(p, axis=-1, keepdims=True)
        acc_ref[r] = alpha * acc_ref[r] + jnp.dot(p.astype(BF16), v, preferred_element_type=F32)
        m_ref[r] = m_cur

    @pl.when(last_ref[s_id] == 1)
    def _():
        for r in range(rep):
            o_ref[:, r * HEAD_DIM:(r + 1) * HEAD_DIM] = (acc_ref[r] / l_ref[r]).astype(o_ref.dtype)


def _flash_tables(lay, tq, tk):
    qt, kt, first, last = [], [], [], []
    for (n, s, base) in ((lay.n_p, lay.s_p, 0), (lay.n_s, lay.s_s, lay.t_p)):
        for b in range(n):
            for qi in range(s // tq):
                nk = s // tk
                for ki in range(nk):
                    qt.append((base + b * s) // tq + qi)
                    kt.append((base + b * s) // tk + ki)
                    first.append(int(ki == 0))
                    last.append(int(ki == nk - 1))
    mk = lambda a: jnp.asarray(a, dtype=I32)
    return mk(qt), mk(kt), mk(first), mk(last)


def _axial_gqa(proj, bprep, lay):
    t = proj.shape[0]
    tq, tk = 512, 512
    rep = B_Q_HEADS // B_KV_HEADS
    qt, kt, first, last = _flash_tables(lay, tq, tk)
    n_steps = qt.shape[0]
    gs = pltpu.PrefetchScalarGridSpec(
        num_scalar_prefetch=4,
        grid=(B_KV_HEADS, n_steps),
        in_specs=[pl.BlockSpec((tq, rep * HEAD_DIM), lambda g, s, qt, kt, f, l: (qt[s], g)),
                  pl.BlockSpec((tk, HEAD_DIM), lambda g, s, qt, kt, f, l: (kt[s], B_Q_HEADS + g)),
                  pl.BlockSpec((tk, HEAD_DIM), lambda g, s, qt, kt, f, l: (kt[s], OFF_BV // HEAD_DIM + g))],
        out_specs=pl.BlockSpec((tq, rep * HEAD_DIM), lambda g, s, qt, kt, f, l: (qt[s], g)),
        scratch_shapes=[pltpu.VMEM((rep, tq, LANES), F32), pltpu.VMEM((rep, tq, LANES), F32),
                        pltpu.VMEM((rep, tq, HEAD_DIM), F32)],
    )
    return pl.pallas_call(
        _flash_kernel,
        grid_spec=gs,
        out_shape=jax.ShapeDtypeStruct((t, B_OUT), BF16),
        compiler_params=_cparams("parallel", "arbitrary"),
        name="b_flash",
    )(qt, kt, first, last, bprep, bprep, proj)


A_BQ = 128


def _t5_bucket(rel):
    nb = T5_BUCKETS // 2
    max_exact = nb // 2
    n = jnp.abs(rel)
    large = max_exact + (jnp.log(jnp.maximum(n, 1).astype(F32) / max_exact)
                         / math.log(T5_MAX_DIST / max_exact) * (nb - max_exact)).astype(I32)
    large = jnp.minimum(large, nb - 1)
    return jnp.where(rel > 0, nb, 0) + jnp.where(n < max_exact, n, large)


def _dil_bias(t5_bias, g, d, half):
    rel = (jnp.arange(3 * A_BQ)[None, :] - A_BQ) - jnp.arange(A_BQ)[:, None]
    tab = t5_bias[:, g * A_HEADS_PER_GROUP:(g + 1) * A_HEADS_PER_GROUP].astype(F32)
    bias = tab[_t5_bucket(rel * d)].transpose(2, 0, 1)
    return jnp.where((jnp.abs(rel) <= half)[None], bias, NEG_INF)


def _dil_kernel(q_ref, kp_ref, kc_ref, kn_ref, vp_ref, vc_ref, vn_ref, b_ref, o_ref, lse_ref, *, nblk_p, tblk_p, nblk_s):
    i = pl.program_id(1)
    in_p = i < tblk_p
    nblk = jnp.where(in_p, nblk_p, nblk_s)
    il = jnp.where(in_p, i, i - tblk_p) % nblk
    prev_ok = il > 0
    next_ok = il < nblk - 1
    scale = HEAD_DIM ** -0.5
    dn = (((1,), (1,)), ((), ()))
    for h in range(A_HEADS_PER_GROUP):
        sl = slice(h * HEAD_DIM, (h + 1) * HEAD_DIM)
        q = q_ref[:, sl]
        b = b_ref[h]
        sp = lax.dot_general(q, kp_ref[:, sl], dn, preferred_element_type=F32) * scale + b[:, 0:A_BQ]
        sc = lax.dot_general(q, kc_ref[:, sl], dn, preferred_element_type=F32) * scale + b[:, A_BQ:2 * A_BQ]
        sn = lax.dot_general(q, kn_ref[:, sl], dn, preferred_element_type=F32) * scale + b[:, 2 * A_BQ:3 * A_BQ]
        sp = jnp.where(prev_ok, sp, NEG_INF)
        sn = jnp.where(next_ok, sn, NEG_INF)
        m = jnp.maximum(jnp.maximum(jnp.max(sp, axis=-1, keepdims=True), jnp.max(sc, axis=-1, keepdims=True)),
                        jnp.max(sn, axis=-1, keepdims=True))
        pp, pc, pn = jnp.exp(sp - m), jnp.exp(sc - m), jnp.exp(sn - m)
        l = (jnp.sum(pp, axis=-1, keepdims=True) + jnp.sum(pc, axis=-1, keepdims=True)
             + jnp.sum(pn, axis=-1, keepdims=True))
        o = (jnp.dot(pp.astype(BF16), vp_ref[:, sl], preferred_element_type=F32)
             + jnp.dot(pc.astype(BF16), vc_ref[:, sl], preferred_element_type=F32)
             + jnp.dot(pn.astype(BF16), vn_ref[:, sl], preferred_element_type=F32))
        o_ref[:, sl] = (o / l).astype(o_ref.dtype)
        lse_ref[:, sl] = jnp.broadcast_to(m + jnp.log(l), (A_BQ, HEAD_DIM))


def _dilated_group(proj, t5_bias, lay, g):
    w, d = DIL_CONFIGS[g]
    half = w // (2 * d)
    t = proj.shape[0]
    rows = t // d
    nblk_p = lay.s_p // d // A_BQ
    nblk_s = lay.s_s // d // A_BQ
    assert nblk_p >= 1 and nblk_s >= 1 and half <= A_BQ
    tblk_p = lay.t_p // d // A_BQ
    tblk = rows // A_BQ
    proj_c = proj.reshape(rows, d * N_IN)
    bias = _dil_bias(t5_bias, g, d, half)
    gw = A_HEADS_PER_GROUP * HEAD_DIM

    def spec(off, shift):
        def imap(c, i):
            return (pl.multiple_of(jnp.clip(i + shift, 0, tblk - 1) * A_BQ, A_BQ),
                    pl.multiple_of(c * N_IN + off + g * gw, LANES))
        return pl.BlockSpec((pl.Element(A_BQ), pl.Element(gw)), imap)

    kern = functools.partial(_dil_kernel, nblk_p=nblk_p, tblk_p=tblk_p, nblk_s=nblk_s)
    o, lse = pl.pallas_call(
        kern,
        grid=(d, tblk),
        in_specs=[spec(OFF_AQ, 0), spec(OFF_AK, -1), spec(OFF_AK, 0), spec(OFF_AK, 1),
                  spec(OFF_AV, -1), spec(OFF_AV, 0), spec(OFF_AV, 1),
                  pl.BlockSpec((A_HEADS_PER_GROUP, A_BQ, 3 * A_BQ), lambda c, i: (0, 0, 0))],
        out_specs=[pl.BlockSpec((A_BQ, gw), lambda c, i: (i, c)),
                   pl.BlockSpec((A_BQ, gw), lambda c, i: (i, c))],
        out_shape=[jax.ShapeDtypeStruct((rows, d * gw), BF16), jax.ShapeDtypeStruct((rows, d * gw), F32)],
        compiler_params=_cparams("parallel", "parallel"),
        name=f"a_dilated_g{g}",
    )(proj_c, proj_c, proj_c, proj_c, proj_c, proj_c, proj_c, bias)
    return o.reshape(t, gw), lse.reshape(t, gw)


C_QROWS = 8
C_KROWS = 2 * NA_ROWS
C_TQ = C_QROWS * GRID_W
C_TK = C_KROWS * GRID_W


def _na_bias(rpb):
    qq = jnp.arange(C_TQ)
    kk = jnp.arange(C_TK)
    qc = (qq % GRID_W)[:, None]
    kc = (kk % GRID_W)[None, :]
    cstart = jnp.clip(qc - NA_COLS // 2, 0, GRID_W - NA_COLS)
    col_ok = (kc >= cstart) & (kc < cstart + NA_COLS)
    ci = jnp.clip(kc - qc, -(NA_COLS - 1), NA_COLS - 1) + NA_COLS - 1
    out = []
    for off in (0, NA_ROWS // 2, NA_ROWS):
        qr = (off + qq // GRID_W)[:, None]
        kr = (kk // GRID_W)[None, :]
        rstart = jnp.clip(qr - NA_ROWS // 2, 0, C_KROWS - NA_ROWS)
        row_ok = (kr >= rstart) & (kr < rstart + NA_ROWS)
        ri = jnp.clip(kr - qr + NA_ROWS - 1, 0, 2 * NA_ROWS - 2)
        b = rpb.astype(F32)[:, ri, ci]
        out.append(jnp.where((row_ok & col_ok)[None], b, NEG_INF))
    return jnp.stack(out)


def _na_kernel(q_ref, k_ref, v_ref, b_ref, o_ref):
    s = lax.dot_general(q_ref[...], k_ref[...], (((1,), (1,)), ((), ())), preferred_element_type=F32)
    s = s * (HEAD_DIM ** -0.5) + b_ref[...]
    m = jnp.max(s, axis=-1, keepdims=True)
    p = jnp.exp(s - m)
    l = jnp.sum(p, axis=-1, keepdims=True)
    o = jnp.dot(p.astype(BF16), v_ref[...], preferred_element_type=F32)
    o_ref[...] = (o / l).astype(o_ref.dtype)


def _neighbourhood(proj, rpb, lay):
    t = proj.shape[0]
    r_p, r_s = lay.s_p // GRID_W, lay.s_s // GRID_W
    assert r_p >= C_KROWS and r_s >= C_KROWS and r_p % C_QROWS == 0 and r_s % C_QROWS == 0
    blk_p = lay.t_p // C_TQ
    bias = _na_bias(rpb)

    def window(i):
        in_p = i < blk_p
        per_seq = jnp.where(in_p, lay.s_p // C_TQ, lay.s_s // C_TQ)
        rows = jnp.where(in_p, r_p, r_s)
        il = jnp.where(in_p, i, i - blk_p)
        seq0 = (i - il % per_seq) * C_TQ
        r0 = (il % per_seq) * C_QROWS
        w0 = jnp.clip(r0 - NA_ROWS // 2, 0, rows - C_KROWS)
        return seq0 + w0 * GRID_W, (r0 - w0) // (NA_ROWS // 2)

    def kv_spec(off):
        return pl.BlockSpec((pl.Element(C_TK), pl.Element(HEAD_DIM)),
                            lambda h, i: (pl.multiple_of(window(i)[0], GRID_W),
                                          pl.multiple_of(off + h * HEAD_DIM, LANES)))

    return pl.pallas_call(
        _na_kernel,
        grid=(C_HEADS, t // C_TQ),
        in_specs=[pl.BlockSpec((C_TQ, HEAD_DIM), lambda h, i: (i, OFF_CQ // HEAD_DIM + h)),
                  kv_spec(OFF_CK), kv_spec(OFF_CV),
                  pl.BlockSpec((None, None, C_TQ, C_TK), lambda h, i: (window(i)[1], h, 0, 0))],
        out_specs=pl.BlockSpec((C_TQ, HEAD_DIM), lambda h, i: (i, h)),
        out_shape=jax.ShapeDtypeStruct((t, C_OUT), BF16),
        compiler_params=_cparams("parallel", "parallel"),
        name="c_neighbourhood",
    )(proj, proj, proj, bias)


RET_C = 256


def _log_sigmoid(x):
    return jnp.minimum(x, 0.0) - jnp.log(1.0 + jnp.exp(-jnp.abs(x)))


def _ret_qk(q_ref, k_ref, cos_ref, sin_ref, h):
    sl = slice(h * D_QK, (h + 1) * D_QK)
    cos, sin = cos_ref[...], sin_ref[...]
    q = _rope_rotate(q_ref[:, sl].astype(F32), cos, sin)
    k = _rope_rotate(k_ref[:, sl].astype(F32), cos, sin) * (D_QK ** -0.5)
    return q, k


def _ret_bwd_kernel(cb_ref, first_ref, q_ref, k_ref, v_ref, cos_ref, sin_ref, dl_ref, o_ref, st_ref):
    s_id = pl.program_id(0)

    @pl.when(first_ref[s_id] == 1)
    def _():
        st_ref[...] = jnp.zeros(st_ref.shape, F32)

    row = lax.broadcasted_iota(I32, (RET_C, 1), 0).astype(F32)
    for h in range(D_HEADS):
        lg = _log_sigmoid(dl_ref[D_HEADS + h:D_HEADS + h + 1, :])
        q, k = _ret_qk(q_ref, k_ref, cos_ref, sin_ref, h)
        v = v_ref[:, h * D_V:(h + 1) * D_V]
        st = st_ref[h]
        q_dec = (q * jnp.exp((RET_C - row) * lg)).astype(BF16)
        o_ref[:, h * D_V:(h + 1) * D_V] = jnp.dot(q_dec, st.astype(BF16), preferred_element_type=F32)
        k_dec = (k * jnp.exp(row * lg)).astype(BF16)
        kv = lax.dot_general(k_dec, v, (((0,), (0,)), ((), ())), preferred_element_type=F32)
        st_ref[h] = st * jnp.exp(RET_C * lg[:, 0:1]) + kv


def _ret_fwd_kernel(cb_ref, first_ref, q_ref, k_ref, v_ref, g_ref, xb_ref, cos_ref, sin_ref, dl_ref, o_ref, st_ref):
    s_id = pl.program_id(0)

    @pl.when(first_ref[s_id] == 1)
    def _():
        st_ref[...] = jnp.zeros(st_ref.shape, F32)

    row = lax.broadcasted_iota(I32, (RET_C, 1), 0).astype(F32)
    diff = (lax.broadcasted_iota(I32, (RET_C, RET_C), 0) - lax.broadcasted_iota(I32, (RET_C, RET_C), 1)).astype(F32)
    for h in range(D_HEADS):
        lgf = _log_sigmoid(dl_ref[h:h + 1, :])
        lgb = _log_sigmoid(dl_ref[D_HEADS + h:D_HEADS + h + 1, :])
        q, k = _ret_qk(q_ref, k_ref, cos_ref, sin_ref, h)
        v = v_ref[:, h * D_V:(h + 1) * D_V]
        st = st_ref[h]
        dmat = jnp.where(diff >= 0, jnp.exp(jnp.maximum(diff, 0.0) * lgf[:, 0:1]),
                         jnp.exp(jnp.maximum(-diff, 0.0) * lgb[:, 0:1]))
        s = lax.dot_general(q.astype(BF16), k.astype(BF16), (((1,), (1,)), ((), ())), preferred_element_type=F32)
        o = jnp.dot((s * dmat).astype(BF16), v, preferred_element_type=F32)
        q_dec = (q * jnp.exp((row + 1.0) * lgf)).astype(BF16)
        o = o + jnp.dot(q_dec, st.astype(BF16), preferred_element_type=F32)
        o = o + xb_ref[:, h * D_V:(h + 1) * D_V]
        k_dec = (k * jnp.exp((RET_C - 1.0 - row) * lgf)).astype(BF16)
        kv = lax.dot_general(k_dec, v, (((0,), (0,)), ((), ())), preferred_element_type=F32)
        st_ref[h] = st * jnp.exp(RET_C * lgf[:, 0:1]) + kv
        mu = jnp.mean(o, axis=-1, keepdims=True)
        var = jnp.mean(jnp.square(o - mu), axis=-1, keepdims=True)
        on = (o - mu) * lax.rsqrt(var + EPS)
        g = g_ref[:, h * D_V:(h + 1) * D_V].astype(F32)
        o_ref[:, h * D_V:(h + 1) * D_V] = (g * jax.nn.sigmoid(g) * on).astype(o_ref.dtype)


def _ret_tables(lay, reverse):
    cb, first = [], []
    for (n, s, base) in ((lay.n_p, lay.s_p, 0), (lay.n_s, lay.s_s, lay.t_p)):
        for b in range(n):
            nc = s // RET_C
            order = range(nc - 1, -1, -1) if reverse else range(nc)
            for j, c in enumerate(order):
                cb.append((base + b * s) // RET_C + c)
                first.append(int(j == 0))
    return jnp.asarray(cb, dtype=I32), jnp.asarray(first, dtype=I32)


def _retention(proj, dlogit, cos, sin, lay):
    t = proj.shape[0]
    dl = jnp.broadcast_to(dlogit.astype(F32).reshape(2 * D_HEADS, 1), (2 * D_HEADS, LANES))
    qw, vw = D_HEADS * D_QK, D_HEADS * D_V
    row_spec = lambda width, off: pl.BlockSpec((pl.Element(RET_C), pl.Element(width)),
                                               lambda s, cb, f: (pl.multiple_of(cb[s] * RET_C, RET_C), off))
    tab_spec = pl.BlockSpec((RET_C, LANES), lambda s, cb, f: (cb[s], 0))
    dl_spec = pl.BlockSpec((2 * D_HEADS, LANES), lambda s, cb, f: (0, 0))
    out_spec = pl.BlockSpec((RET_C, vw), lambda s, cb, f: (cb[s], 0))
    state = pltpu.VMEM((D_HEADS, D_QK, D_V), F32)

    cb, first = _ret_tables(lay, True)
    xb = pl.pallas_call(
        _ret_bwd_kernel,
        grid_spec=pltpu.PrefetchScalarGridSpec(
            num_scalar_prefetch=2, grid=(cb.shape[0],),
            in_specs=[row_spec(qw, OFF_DQ), row_spec(qw, OFF_DK), row_spec(vw, OFF_DV), tab_spec, tab_spec, dl_spec],
            out_specs=out_spec, scratch_shapes=[state]),
        out_shape=jax.ShapeDtypeStruct((t, vw), F32),
        compiler_params=_cparams("arbitrary"),
        name="d_retention_bwd",
    )(cb, first, proj, proj, proj, cos, sin, dl)

    cb, first = _ret_tables(lay, False)
    return pl.pallas_call(
        _ret_fwd_kernel,
        grid_spec=pltpu.PrefetchScalarGridSpec(
            num_scalar_prefetch=2, grid=(cb.shape[0],),
            in_specs=[row_spec(qw, OFF_DQ), row_spec(qw, OFF_DK), row_spec(vw, OFF_DV), row_spec(vw, OFF_DG),
                      out_spec, tab_spec, tab_spec, dl_spec],
            out_specs=out_spec, scratch_shapes=[state]),
        out_shape=jax.ShapeDtypeStruct((t, vw), BF16),
        compiler_params=_cparams("arbitrary"),
        name="d_retention_fwd",
    )(cb, first, proj, proj, proj, proj, xb, cos, sin, dl)


def _merge_kernel(oa0, oa1, oa2, ls0, ls1, ls2, ob, oc, od, g0, g1, g2, g3, wa, wb, wc, wd, o_ref, oa_ref):
    @pl.when(pl.program_id(1) == 0)
    def _():
        l0, l1, l2 = ls0[...], ls1[...], ls2[...]
        m = jnp.maximum(jnp.maximum(l0, l1), l2)
        e0, e1, e2 = jnp.exp(l0 - m), jnp.exp(l1 - m), jnp.exp(l2 - m)
        num = e0 * oa0[...].astype(F32) + e1 * oa1[...].astype(F32) + e2 * oa2[...].astype(F32)
        oa_ref[...] = (num / (e0 + e1 + e2)).astype(BF16)

    def term(gate, o, w):
        return jax.nn.sigmoid(gate[...].astype(F32)) * jnp.dot(o, w[...], preferred_element_type=F32)

    acc = term(g0, oa_ref[...], wa) + term(g1, ob[...], wb) + term(g2, oc[...], wc) + term(g3, od[...], wd)
    o_ref[...] = acc.astype(o_ref.dtype)


def _branch_merge(proj, a_parts, o_b, o_c, o_d, w_branch_bf16):
    t = proj.shape[0]
    tm, tn = 512, 512
    (oa0, ls0), (oa1, ls1), (oa2, ls2) = a_parts
    row = lambda width: pl.BlockSpec((tm, width), lambda i, j: (i, 0))
    gate = lambda b: pl.BlockSpec((pl.Element(tm), pl.Element(tn)),
                                  lambda i, j: (pl.multiple_of(i * tm, tm),
                                                pl.multiple_of(OFF_GATE + b * D_MODEL + j * tn, LANES)))
    wspec = lambda width: pl.BlockSpec((width, tn), lambda i, j: (0, j))
    offs = (0, A_OUT, A_OUT + B_OUT, A_OUT + B_OUT + C_OUT, MIX_WIDTH)
    ws = [w_branch_bf16[offs[b]:offs[b + 1]] for b in range(N_BRANCH)]
    return pl.pallas_call(
        _merge_kernel,
        grid=(t // tm, D_MODEL // tn),
        in_specs=[row(A_OUT)] * 6 + [row(B_OUT), row(C_OUT), row(D_OUT)]
                 + [gate(0), gate(1), gate(2), gate(3)]
                 + [wspec(A_OUT), wspec(B_OUT), wspec(C_OUT), wspec(D_OUT)],
        out_specs=pl.BlockSpec((tm, tn), lambda i, j: (i, j)),
        out_shape=jax.ShapeDtypeStruct((t, D_MODEL), BF16),
        scratch_shapes=[pltpu.VMEM((tm, A_OUT), BF16)],
        compiler_params=_cparams("parallel", "arbitrary"),
        name="branch_merge",
    )(oa0, oa1, oa2, ls0, ls1, ls2, o_b, o_c, o_d, proj, proj, proj, proj, *ws)


def _split3(x):
    hi = x.astype(BF16)
    lo = (x - hi.astype(F32)).astype(BF16)
    return hi, lo


def _outproj_kernel(mg_ref, x_ref, mod_ref, g_ref, w_ref, wr_hi, wr_lo, br_ref, xo_ref, h2_ref, lg_ref):
    y = jnp.dot(mg_ref[...], w_ref[...], preferred_element_type=F32)
    m = mod_ref[0]
    g = g_ref[...]
    yn = y * lax.rsqrt(jnp.mean(y * y, axis=-1, keepdims=True) + EPS) * g[1:2]
    x = x_ref[...] + m[2:3] * yn
    xo_ref[...] = x
    h2 = x * lax.rsqrt(jnp.mean(x * x, axis=-1, keepdims=True) + EPS) * g[2:3] * (1.0 + m[4:5]) + m[3:4]
    h2b = h2.astype(BF16)
    h2_ref[...] = h2
    lg_ref[...] = (jnp.dot(h2b, wr_hi[...], preferred_element_type=F32)
                   + jnp.dot(h2b, wr_lo[...], preferred_element_type=F32) + br_ref[...])


def _outproj_residual(merged, x, modseg, gains, w_out_bf16, w_router, b_router):
    t = x.shape[0]
    tm = 256
    wr = jnp.pad(w_router.astype(F32), ((0, 0), (0, LANES - N_EXPERTS)))
    wr_hi, wr_lo = _split3(wr)
    br = jnp.pad(b_router.astype(F32), (0, LANES - N_EXPERTS)).reshape(1, LANES)
    gains8 = jnp.pad(gains.astype(F32), ((0, 4), (0, 0)))
    row = lambda width: pl.BlockSpec((tm, width), lambda i: (i, 0))
    full = lambda a, b: pl.BlockSpec((a, b), lambda i: (0, 0))
    return pl.pallas_call(
        _outproj_kernel,
        grid=(t // tm,),
        in_specs=[row(D_MODEL), row(D_MODEL),
                  pl.BlockSpec((1, 8, D_MODEL), lambda i: (i * tm // SEG, 0, 0)),
                  full(8, D_MODEL), full(D_MODEL, D_MODEL), full(D_MODEL, LANES), full(D_MODEL, LANES), full(1, LANES)],
        out_specs=[row(D_MODEL), row(D_MODEL), row(LANES)],
        out_shape=[jax.ShapeDtypeStruct((t, D_MODEL), F32), jax.ShapeDtypeStruct((t, D_MODEL), F32),
                   jax.ShapeDtypeStruct((t, LANES), F32)],
        compiler_params=_cparams("parallel"),
        name="outproj_residual",
    )(merged, x, modseg, gains8, w_out_bf16, wr_hi, wr_lo, br)


R_TM = 512


def _route_kernel(lg_ref, ti_ref, tw_ref, rk_ref, cnt_ref, carry_ref):
    @pl.when(pl.program_id(0) == 0)
    def _():
        carry_ref[...] = jnp.zeros(carry_ref.shape, F32)

    lane = _lane_iota((R_TM, LANES))
    l = jnp.where(lane < N_EXPERTS, lg_ref[...], -jnp.inf)
    vals, idxs = [], []
    for _ in range(TOP_K):
        m = jnp.max(l, axis=-1, keepdims=True)
        idx = jnp.min(jnp.where(l == m, lane.astype(F32), float(LANES)), axis=-1, keepdims=True).astype(I32)
        vals.append(m)
        idxs.append(idx)
        l = jnp.where(lane == idx, -jnp.inf, l)
    es = [jnp.exp(v - vals[0]) for v in vals]
    den = es[0] + es[1] + es[2] + es[3]
    ti = jnp.zeros((R_TM, LANES), I32)
    tw = jnp.zeros((R_TM, LANES), F32)
    cnt = jnp.zeros((R_TM, LANES), F32)
    for k in range(TOP_K):
        ti = jnp.where(lane == k, idxs[k], ti)
        tw = jnp.where(lane == k, es[k] / den, tw)
        cnt = cnt + jnp.where(lane == idxs[k], 1.0, 0.0)
    ti_ref[...] = ti
    tw_ref[...] = tw
    r = lax.broadcasted_iota(I32, (R_TM, R_TM), 0)
    c = lax.broadcasted_iota(I32, (R_TM, R_TM), 1)
    tri = jnp.where(c < r, 1.0, 0.0).astype(BF16)
    before = jnp.dot(tri, cnt.astype(BF16), preferred_element_type=F32) + carry_ref[...]
    rk = jnp.zeros((R_TM, LANES), I32)
    for k in range(TOP_K):
        pos = jnp.sum(jnp.where(lane == idxs[k], before, 0.0), axis=-1, keepdims=True)
        rk = jnp.where(lane == k, pos.astype(I32), rk)
    rk_ref[...] = rk
    carry_ref[...] = carry_ref[...] + jnp.sum(cnt, axis=0, keepdims=True)
    cnt_ref[...] = jnp.broadcast_to(carry_ref[...], cnt_ref.shape)


def _route(logits):
    t = logits.shape[0]
    row = pl.BlockSpec((R_TM, LANES), lambda i: (i, 0))
    return pl.pallas_call(
        _route_kernel,
        grid=(t // R_TM,),
        in_specs=[row],
        out_specs=[row, row, row, pl.BlockSpec((8, LANES), lambda i: (0, 0))],
        out_shape=[jax.ShapeDtypeStruct((t, LANES), I32), jax.ShapeDtypeStruct((t, LANES), F32),
                   jax.ShapeDtypeStruct((t, LANES), I32), jax.ShapeDtypeStruct((8, LANES), F32)],
        scratch_shapes=[pltpu.VMEM((1, LANES), F32)],
        compiler_params=_cparams("arbitrary"),
        name="moe_route",
    )(logits)


E_TM = 512
DISP_TT = 512
COMB_TT = 256


def _dispatch_kernel(off_ref, ti_ref, rk_ref, h_ref, xs_in, xs_ref, sem):
    del xs_in
    base = pl.program_id(0) * DISP_TT

    def copy(a):
        dst = off_ref[ti_ref[a]] + rk_ref[a]
        return pltpu.make_async_copy(h_ref.at[pl.ds(base + a // TOP_K, 1)], xs_ref.at[pl.ds(dst, 1)], sem)

    def start(a, c):
        copy(a).start()
        return c

    def wait(a, c):
        copy(a).wait()
        return c

    lax.fori_loop(0, DISP_TT * TOP_K, start, 0)
    lax.fori_loop(0, DISP_TT * TOP_K, wait, 0)


def _dispatch(offsets, ti_flat, rk_flat, h2, n_rows):
    t = h2.shape[0]
    xs0 = jnp.zeros((n_rows, D_MODEL), h2.dtype)
    blk = pl.BlockSpec((DISP_TT * TOP_K,), lambda i, off: (i,), memory_space=pltpu.SMEM)
    return pl.pallas_call(
        _dispatch_kernel,
        grid_spec=pltpu.PrefetchScalarGridSpec(
            num_scalar_prefetch=1, grid=(t // DISP_TT,),
            in_specs=[blk, blk, pl.BlockSpec(memory_space=pl.ANY), pl.BlockSpec(memory_space=pl.ANY)],
            out_specs=pl.BlockSpec(memory_space=pl.ANY),
            scratch_shapes=[pltpu.SemaphoreType.DMA(())]),
        out_shape=jax.ShapeDtypeStruct((n_rows, D_MODEL), h2.dtype),
        input_output_aliases={4: 0},
        compiler_params=_cparams("arbitrary"),
        name="moe_dispatch",
    )(offsets, ti_flat, rk_flat, h2, xs0)


def _up_kernel(te_ref, nu_ref, x_ref, wg_ref, wl_ref, bg_ref, bl_ref, o_ref):
    @pl.when(pl.program_id(0) < nu_ref[0])
    def _():
        x = x_ref[...].astype(BF16)
        glu = jnp.dot(x, wg_ref[0], preferred_element_type=F32) + bg_ref[0]
        lin = jnp.dot(x, wl_ref[0], preferred_element_type=F32) + bl_ref[0]
        glu = jnp.minimum(glu, SWIGLU_LIMIT)
        lin = jnp.clip(lin, -SWIGLU_LIMIT, SWIGLU_LIMIT)
        o_ref[...] = (glu * jax.nn.sigmoid(SWIGLU_ALPHA * glu) * (lin + 1.0)).astype(o_ref.dtype)


def _expert_up(tile_e, n_used, xs, wg, wl, bg, bl):
    n_rows = xs.shape[0]
    tn = 512
    wspec = pl.BlockSpec((1, D_MODEL, tn), lambda i, j, te, nu: (te[i], 0, j))
    bspec = pl.BlockSpec((1, 1, tn), lambda i, j, te, nu: (te[i], 0, j))
    return pl.pallas_call(
        _up_kernel,
        grid_spec=pltpu.PrefetchScalarGridSpec(
            num_scalar_prefetch=2, grid=(n_rows // E_TM, D_FF // tn),
            in_specs=[pl.BlockSpec((E_TM, D_MODEL), lambda i, j, te, nu: (i, 0)), wspec, wspec, bspec, bspec],
            out_specs=pl.BlockSpec((E_TM, tn), lambda i, j, te, nu: (i, j))),
        out_shape=jax.ShapeDtypeStruct((n_rows, D_FF), BF16),
        compiler_params=_cparams("parallel", "arbitrary"),
        name="moe_up",
    )(tile_e, n_used, xs, wg, wl, bg, bl)


def _down_kernel(te_ref, nu_ref, a_ref, w_ref, b_ref, o_ref):
    @pl.when(pl.program_id(0) < nu_ref[0])
    def _():
        o_ref[...] = jnp.dot(a_ref[...], w_ref[0], preferred_element_type=F32) + b_ref[0]


def _expert_down(tile_e, n_used, act, wd, bd):
    n_rows = act.shape[0]
    tn = 1024
    return pl.pallas_call(
        _down_kernel,
        grid_spec=pltpu.PrefetchScalarGridSpec(
            num_scalar_prefetch=2, grid=(n_rows // E_TM, D_MODEL // tn),
            in_specs=[pl.BlockSpec((E_TM, D_FF), lambda i, j, te, nu: (i, 0)),
                      pl.BlockSpec((1, D_FF, tn), lambda i, j, te, nu: (te[i], 0, j)),
                      pl.BlockSpec((1, 1, tn), lambda i, j, te, nu: (te[i], 0, j))],
            out_specs=pl.BlockSpec((E_TM, tn), lambda i, j, te, nu: (i, j))),
        out_shape=jax.ShapeDtypeStruct((n_rows, D_MODEL), F32),
        compiler_params=_cparams("parallel", "arbitrary"),
        name="moe_down",
    )(tile_e, n_used, act, wd, bd)


def _combine_kernel(off_ref, ti_ref, rk_ref, ys_ref, tw_ref, x_ref, mod_ref, g_ref, o_ref, buf, sem):
    def copy(a):
        src = off_ref[ti_ref[a]] + rk_ref[a]
        return pltpu.make_async_copy(ys_ref.at[pl.ds(src, 1)], buf.at[a % TOP_K, pl.ds(a // TOP_K, 1)], sem)

    def start(a, c):
        copy(a).start()
        return c

    def wait(a, c):
        copy(a).wait()
        return c

    lax.fori_loop(0, COMB_TT * TOP_K, start, 0)
    lax.fori_loop(0, COMB_TT * TOP_K, wait, 0)
    tw = tw_ref[...]
    f = tw[:, 0:1] * buf[0]
    for k in range(1, TOP_K):
        f = f + tw[:, k:k + 1] * buf[k]
    m = mod_ref[0]
    fn = f * lax.rsqrt(jnp.mean(f * f, axis=-1, keepdims=True) + EPS) * g_ref[3:4]
    o_ref[...] = x_ref[...] + m[5:6] * fn


def _combine(offsets, ti_flat, rk_flat, ys, tw, x, modseg, gains):
    t = x.shape[0]
    gains8 = jnp.pad(gains.astype(F32), ((0, 4), (0, 0)))
    blk = pl.BlockSpec((COMB_TT * TOP_K,), lambda i, off: (i,), memory_space=pltpu.SMEM)
    return pl.pallas_call(
        _combine_kernel,
        grid_spec=pltpu.PrefetchScalarGridSpec(
            num_scalar_prefetch=1, grid=(t // COMB_TT,),
            in_specs=[blk, blk, pl.BlockSpec(memory_space=pl.ANY),
                      pl.BlockSpec((COMB_TT, LANES), lambda i, off: (i, 0)),
                      pl.BlockSpec((COMB_TT, D_MODEL), lambda i, off: (i, 0)),
                      pl.BlockSpec((1, 8, D_MODEL), lambda i, off: (i * COMB_TT // SEG, 0, 0)),
                      pl.BlockSpec((8, D_MODEL), lambda i, off: (0, 0))],
            out_specs=pl.BlockSpec((COMB_TT, D_MODEL), lambda i, off: (i, 0)),
            scratch_shapes=[pltpu.VMEM((TOP_K, COMB_TT, D_MODEL), F32), pltpu.SemaphoreType.DMA(())]),
        out_shape=jax.ShapeDtypeStruct((t, D_MODEL), F32),
        compiler_params=_cparams("arbitrary"),
        name="moe_combine",
    )(offsets, ti_flat, rk_flat, ys, tw, x, modseg, gains8)


def _moe(h2, logits, x, modseg, gains, wg, wl, bg, bl, wd, bd):
    t = h2.shape[0]
    ti, tw, rk, cnt = _route(logits)
    counts = cnt[0, :N_EXPERTS].astype(I32)
    padded = (counts + E_TM - 1) // E_TM * E_TM
    ends = jnp.cumsum(padded)
    offsets = ends - padded
    n_tiles = t * TOP_K // E_TM + N_EXPERTS
    tile_e = jnp.minimum(jnp.searchsorted(ends, jnp.arange(n_tiles, dtype=I32) * E_TM, side="right"),
                         N_EXPERTS - 1).astype(I32)
    n_used = (ends[-1:] // E_TM).astype(I32)
    ti_flat = ti[:, :TOP_K].reshape(-1)
    rk_flat = rk[:, :TOP_K].reshape(-1)
    xs = _dispatch(offsets, ti_flat, rk_flat, h2, n_tiles * E_TM)
    act = _expert_up(tile_e, n_used, xs, wg, wl, bg, bl)
    ys = _expert_down(tile_e, n_used, act, wd, bd)
    return _combine(offsets, ti_flat, rk_flat, ys, tw, x, modseg, gains)


def _forward(x, c8, seg_rows, lay, t5_bias, w_mod, b_mod, norm_gains, w_in, qk_norm_gains, na_rpb,
             ret_decay_logit, w_branch, w_out, w_router, b_router, w_up, b_up, w_down, b_down):
    mod = _modulation(c8, w_mod, b_mod)
    modseg = mod[:, seg_rows, :].reshape(DEPTH, len(seg_rows), 6, D_MODEL)
    modseg = jnp.pad(modseg, ((0, 0), (0, 0), (0, 2), (0, 0)))
    cos_a, sin_a = _axial_tables(lay)
    cos_r, sin_r = _rope_tables(lay)
    for l in range(DEPTH):
        gains = norm_gains[l]
        proj = _norm_inproj(x, modseg[l], gains[0], w_in[l].astype(BF16))
        a_parts = [_dilated_group(proj, t5_bias, lay, g) for g in range(A_GROUPS)]
        o_b = _axial_gqa(proj, _b_prepare(proj, qk_norm_gains[l], cos_a, sin_a), lay)
        o_c = _neighbourhood(proj, na_rpb[l], lay)
        o_d = _retention(proj, ret_decay_logit[l], cos_r, sin_r, lay)
        merged = _branch_merge(proj, a_parts, o_b, o_c, o_d, w_branch[l].astype(BF16))
        x, h2, logits = _outproj_residual(merged, x, modseg[l], gains, w_out[l].astype(BF16), w_router[l], b_router[l])
        wu = w_up[l].astype(BF16)
        bu = b_up[l].astype(F32)
        x = _moe(h2, logits, x, modseg[l], gains,
                 wu[:, :, 0::2], wu[:, :, 1::2], bu[:, None, 0::2], bu[:, None, 1::2],
                 w_down[l].astype(BF16), b_down[l].astype(F32)[:, None, :])
    return x


def kernel(x_prompt, x_sample, c_prompt, c_sample, t5_bias, w_mod, b_mod, norm_gains, w_in, qk_norm_gains,
           na_rpb, ret_decay_logit, w_branch, w_out, w_router, b_router, w_up, b_up, w_down, b_down):
    n_p, s_p, _ = x_prompt.shape
    n_s, s_s, _ = x_sample.shape
    lay = Layout(n_p, s_p, n_s, s_s)
    assert s_p % SEG == 0 and s_s % SEG == 0 and n_p + n_s <= 8
    x = jnp.concatenate([x_prompt.reshape(-1, D_MODEL), x_sample.reshape(-1, D_MODEL)], axis=0)
    c8 = jnp.concatenate([c_prompt, c_sample, jnp.zeros((8 - n_p - n_s, D_MODEL), F32)], axis=0)
    seg_rows = tuple([b for b in range(n_p) for _ in range(s_p // SEG)]
                     + [n_p + b for b in range(n_s) for _ in range(s_s // SEG)])
    y = _forward(x, c8, jnp.asarray(seg_rows, dtype=I32), lay, t5_bias, w_mod, b_mod, norm_gains, w_in,
                 qk_norm_gains, na_rpb, ret_decay_logit, w_branch, w_out, w_router, b_router,
                 w_up, b_up, w_down, b_down)
    return (y[:lay.t_p].reshape(n_p, s_p, D_MODEL), y[lay.t_p:].reshape(n_s, s_s, D_MODEL))
```

```python
import functools
import math
from typing import NamedTuple

import jax
import jax.numpy as jnp
from jax import lax
from jax.experimental import pallas as pl
from jax.experimental.pallas import tpu as pltpu

F32 = jnp.float32
BF16 = jnp.bfloat16
I32 = jnp.int32

D_MODEL = 2048
DEPTH = 2
HEAD_DIM = 128
GRID_W = 64
EPS = 1e-6
ROPE_THETA = 10000.0
NEG_INF = -1e30

DIL_CONFIGS = ((128, 1), (512, 4), (2048, 16))
A_GROUPS = 3
A_HEADS_PER_GROUP = 6
A_HEADS = A_GROUPS * A_HEADS_PER_GROUP
T5_BUCKETS = 32
T5_MAX_DIST = 1024
B_Q_HEADS = 6
B_KV_HEADS = 2
C_HEADS = 6
NA_ROWS = 8
NA_COLS = 16
D_HEADS = 4
D_QK = 128
D_V = 256
N_EXPERTS = 32
TOP_K = 4
D_FF = 2048
SWIGLU_LIMIT = 7.0
SWIGLU_ALPHA = 1.702
N_BRANCH = 4

A_OUT = A_HEADS_PER_GROUP * HEAD_DIM
B_OUT = B_Q_HEADS * HEAD_DIM
C_OUT = C_HEADS * HEAD_DIM
D_OUT = D_HEADS * D_V
MIX_WIDTH = A_OUT + B_OUT + C_OUT + D_OUT

IN_SPLITS = (A_HEADS * HEAD_DIM, A_HEADS * HEAD_DIM, A_HEADS * HEAD_DIM,
             B_Q_HEADS * HEAD_DIM, B_KV_HEADS * HEAD_DIM, B_KV_HEADS * HEAD_DIM,
             C_HEADS * HEAD_DIM, C_HEADS * HEAD_DIM, C_HEADS * HEAD_DIM,
             D_HEADS * D_QK, D_HEADS * D_QK, D_HEADS * D_V, D_HEADS * D_V,
             N_BRANCH * D_MODEL)
N_IN = sum(IN_SPLITS)
_OFF = [0]
for _w in IN_SPLITS:
    _OFF.append(_OFF[-1] + _w)
(OFF_AQ, OFF_AK, OFF_AV, OFF_BQ, OFF_BK, OFF_BV, OFF_CQ, OFF_CK, OFF_CV,
 OFF_DQ, OFF_DK, OFF_DV, OFF_DG, OFF_GATE) = _OFF[:-1]

SEG = 2048
LANES = 128
VMEM_LIMIT = 56 * 1024 * 1024


class Layout(NamedTuple):
    n_p: int
    s_p: int
    n_s: int
    s_s: int

    @property
    def t_p(self):
        return self.n_p * self.s_p

    @property
    def t(self):
        return self.n_p * self.s_p + self.n_s * self.s_s


def _cparams(*sem):
    return pltpu.CompilerParams(dimension_semantics=sem, vmem_limit_bytes=VMEM_LIMIT)


def _mod_kernel(c_ref, w_ref, b_ref, o_ref):
    c = c_ref[...]
    s = (c * jax.nn.sigmoid(c)).astype(BF16)
    o_ref[0] = jnp.dot(s, w_ref[0].astype(BF16), preferred_element_type=F32) + b_ref[0]


def _modulation(c8, w_mod, b_mod):
    tn = 1024
    n6 = 6 * D_MODEL
    return pl.pallas_call(
        _mod_kernel,
        grid=(DEPTH, n6 // tn),
        in_specs=[pl.BlockSpec((8, D_MODEL), lambda l, j: (0, 0)),
                  pl.BlockSpec((1, D_MODEL, tn), lambda l, j: (l, 0, j)),
                  pl.BlockSpec((1, 1, tn), lambda l, j: (l, 0, j))],
        out_specs=pl.BlockSpec((1, 8, tn), lambda l, j: (l, 0, j)),
        out_shape=jax.ShapeDtypeStruct((DEPTH, 8, n6), F32),
        compiler_params=_cparams("parallel", "parallel"),
        name="adaln_mod",
    )(c8, w_mod, b_mod.reshape(DEPTH, 1, n6))


def _inproj_kernel(x_ref, mod_ref, g_ref, w_ref, o_ref, h_ref):
    @pl.when(pl.program_id(1) == 0)
    def _():
        x = x_ref[...]
        y = x * lax.rsqrt(jnp.mean(x * x, axis=-1, keepdims=True) + EPS) * g_ref[...]
        m = mod_ref[0]
        h_ref[...] = (y * (1.0 + m[1:2]) + m[0:1]).astype(BF16)

    o_ref[...] = jnp.dot(h_ref[...], w_ref[...], preferred_element_type=F32).astype(o_ref.dtype)


def _norm_inproj(x, modseg, gain, w_in_bf16):
    t = x.shape[0]
    tm, tn = 1024, 1280
    assert t % tm == 0 and N_IN % tn == 0 and SEG % tm == 0
    return pl.pallas_call(
        _inproj_kernel,
        grid=(t // tm, N_IN // tn),
        in_specs=[pl.BlockSpec((tm, D_MODEL), lambda i, j: (i, 0)),
                  pl.BlockSpec((1, 8, D_MODEL), lambda i, j: (i * tm // SEG, 0, 0)),
                  pl.BlockSpec((1, D_MODEL), lambda i, j: (0, 0)),
                  pl.BlockSpec((D_MODEL, tn), lambda i, j: (0, j))],
        out_specs=pl.BlockSpec((tm, tn), lambda i, j: (i, j)),
        out_shape=jax.ShapeDtypeStruct((t, N_IN), BF16),
        scratch_shapes=[pltpu.VMEM((tm, D_MODEL), BF16)],
        compiler_params=_cparams("parallel", "arbitrary"),
        name="norm_inproj",
    )(x, modseg, gain.reshape(1, D_MODEL), w_in_bf16)


def _local_pos(lay):
    return jnp.concatenate([jnp.tile(jnp.arange(lay.s_p), lay.n_p), jnp.tile(jnp.arange(lay.s_s), lay.n_s)])


def _axial_tables(lay):
    pos = _local_pos(lay)
    lane = jnp.arange(LANES)
    quarter = HEAD_DIM // 4
    freqs = ROPE_THETA ** (-jnp.arange(quarter, dtype=F32) / quarter)
    f = freqs[lane % quarter]
    p = jnp.where(lane[None, :] < HEAD_DIM // 2, (pos // GRID_W)[:, None], (pos % GRID_W)[:, None]).astype(F32)
    ang = p * f[None, :]
    sign = jnp.where((lane % (2 * quarter)) < quarter, -1.0, 1.0).astype(F32)
    return jnp.cos(ang), jnp.sin(ang) * sign[None, :]


def _rope_tables(lay):
    pos = _local_pos(lay)
    lane = jnp.arange(LANES)
    half = D_QK // 2
    freqs = ROPE_THETA ** (-jnp.arange(half, dtype=F32) / half)
    ang = pos.astype(F32)[:, None] * freqs[lane % half][None, :]
    sign = jnp.where(lane < half, -1.0, 1.0).astype(F32)
    return jnp.cos(ang), jnp.sin(ang) * sign[None, :]


def _lane_iota(shape):
    return lax.broadcasted_iota(I32, shape, len(shape) - 1)


def _axial_rotate(x, cos, sin_signed):
    q = HEAD_DIM // 4
    lo = (_lane_iota(x.shape) % (2 * q)) < q
    partner = jnp.where(lo, pltpu.roll(x, LANES - q, 1), pltpu.roll(x, q, 1))
    return x * cos + partner * sin_signed


def _rope_rotate(x, cos, sin_signed):
    return x * cos + pltpu.roll(x, D_QK // 2, 1) * sin_signed


def _bprep_kernel(x_ref, g_ref, cos_ref, sin_ref, o_ref):
    x = x_ref[...].astype(F32)
    y = x * lax.rsqrt(jnp.mean(x * x, axis=-1, keepdims=True) + EPS) * g_ref[0]
    y = _axial_rotate(y, cos_ref[...], sin_ref[...])
    scale = jnp.where(pl.program_id(1) < B_Q_HEADS, HEAD_DIM ** -0.5, 1.0)
    o_ref[...] = (y * scale).astype(o_ref.dtype)


def _b_prepare(proj, qk_g, cos, sin):
    t = proj.shape[0]
    tr = 1024
    nh = B_Q_HEADS + B_KV_HEADS
    g8 = jnp.concatenate([jnp.tile(qk_g[0:1], (B_Q_HEADS, 1)), jnp.tile(qk_g[1:2], (B_KV_HEADS, 1))]).reshape(nh, 1, HEAD_DIM)
    return pl.pallas_call(
        _bprep_kernel,
        grid=(t // tr, nh),
        in_specs=[pl.BlockSpec((tr, HEAD_DIM), lambda i, h: (i, OFF_BQ // HEAD_DIM + h)),
                  pl.BlockSpec((1, 1, HEAD_DIM), lambda i, h: (h, 0, 0)),
                  pl.BlockSpec((tr, HEAD_DIM), lambda i, h: (i, 0)),
                  pl.BlockSpec((tr, HEAD_DIM), lambda i, h: (i, 0))],
        out_specs=pl.BlockSpec((tr, HEAD_DIM), lambda i, h: (i, h)),
        out_shape=jax.ShapeDtypeStruct((t, nh * HEAD_DIM), BF16),
        compiler_params=_cparams("parallel", "parallel"),
        name="b_prep",
    )(proj, g8, cos, sin)


def _flash_kernel(qt_ref, kt_ref, first_ref, last_ref, q_ref, k_ref, v_ref, o_ref, m_ref, acc_ref):
    s_id = pl.program_id(1)
    rep = B_Q_HEADS // B_KV_HEADS

    @pl.when(first_ref[s_id] == 1)
    def _():
        m_ref[...] = jnp.full(m_ref.shape, -jnp.inf, F32)
        acc_ref[...] = jnp.zeros(acc_ref.shape, F32)

    k = k_ref[...]
    v = v_ref[...]
    for r in range(rep):
        q = q_ref[:, r * HEAD_DIM:(r + 1) * HEAD_DIM]
        s = lax.dot_general(q, k, (((1,), (1,)), ((), ())), preferred_element_type=F32)
        m_prev = m_ref[r]
        m_cur = jnp.maximum(m_prev, jnp.max(s, axis=-1, keepdims=True))
        alpha = jnp.exp(m_prev - m_cur)
        p = jnp.exp(s - m_cur[:, 0:1]).astype(BF16)
        pv = jnp.dot(p, v, preferred_element_type=F32)
        acc_ref[r] = jnp.concatenate([alpha, alpha], axis=1) * acc_ref[r] + pv
        m_ref[r] = m_cur

    @pl.when(last_ref[s_id] == 1)
    def _():
        for r in range(rep):
            acc = acc_ref[r]
            o_ref[:, r * HEAD_DIM:(r + 1) * HEAD_DIM] = (acc[:, :HEAD_DIM] / acc[:, HEAD_DIM:]).astype(o_ref.dtype)


def _flash_tables(lay, tq, tk):
    qt, kt, first, last = [], [], [], []
    for (n, s, base) in ((lay.n_p, lay.s_p, 0), (lay.n_s, lay.s_s, lay.t_p)):
        for b in range(n):
            for qi in range(s // tq):
                nk = s // tk
                for ki in range(nk):
                    qt.append((base + b * s) // tq + qi)
                    kt.append((base + b * s) // tk + ki)
                    first.append(int(ki == 0))
                    last.append(int(ki == nk - 1))
    mk = lambda a: jnp.asarray(a, dtype=I32)
    return mk(qt), mk(kt), mk(first), mk(last)


def _axial_gqa(proj, bprep, lay):
    t = proj.shape[0]
    tq, tk = 512, 1024
    rep = B_Q_HEADS // B_KV_HEADS
    qt, kt, first, last = _flash_tables(lay, tq, tk)
    n_steps = qt.shape[0]
    v = proj[:, OFF_BV:OFF_BV + B_KV_HEADS * HEAD_DIM].reshape(t, B_KV_HEADS, HEAD_DIM)
    v_ones = jnp.concatenate([v, jnp.ones_like(v)], axis=-1).reshape(t, B_KV_HEADS * 2 * HEAD_DIM)
    gs = pltpu.PrefetchScalarGridSpec(
        num_scalar_prefetch=4,
        grid=(B_KV_HEADS, n_steps),
        in_specs=[pl.BlockSpec((tq, rep * HEAD_DIM), lambda g, s, qt, kt, f, l: (qt[s], g)),
                  pl.BlockSpec((tk, HEAD_DIM), lambda g, s, qt, kt, f, l: (kt[s], B_Q_HEADS + g)),
                  pl.BlockSpec((tk, 2 * HEAD_DIM), lambda g, s, qt, kt, f, l: (kt[s], g))],
        out_specs=pl.BlockSpec((tq, rep * HEAD_DIM), lambda g, s, qt, kt, f, l: (qt[s], g)),
        scratch_shapes=[pltpu.VMEM((rep, tq, LANES), F32), pltpu.VMEM((rep, tq, 2 * HEAD_DIM), F32)],
    )
    return pl.pallas_call(
        _flash_kernel,
        grid_spec=gs,
        out_shape=jax.ShapeDtypeStruct((t, B_OUT), BF16),
        compiler_params=_cparams("parallel", "arbitrary"),
        name="b_flash",
    )(qt, kt, first, last, bprep, bprep, v_ones)


A_BQ = 128


def _t5_bucket(rel):
    nb = T5_BUCKETS // 2
    max_exact = nb // 2
    n = jnp.abs(rel)
    large = max_exact + (jnp.log(jnp.maximum(n, 1).astype(F32) / max_exact)
                         / math.log(T5_MAX_DIST / max_exact) * (nb - max_exact)).astype(I32)
    large = jnp.minimum(large, nb - 1)
    return jnp.where(rel > 0, nb, 0) + jnp.where(n < max_exact, n, large)


def _dil_bias(t5_bias, g, d, half):
    rel = (jnp.arange(3 * A_BQ)[None, :] - A_BQ) - jnp.arange(A_BQ)[:, None]
    tab = t5_bias[:, g * A_HEADS_PER_GROUP:(g + 1) * A_HEADS_PER_GROUP].astype(F32)
    onehot = (_t5_bucket(rel * d)[:, :, None] == jnp.arange(T5_BUCKETS)[None, None, :]).astype(F32)
    bias = jnp.einsum("qkb,bh->hqk", onehot, tab, precision=lax.Precision.HIGHEST)
    return jnp.where((jnp.abs(rel) <= half)[None], bias, NEG_INF)


def _dil_kernel(q_ref, kp_ref, kc_ref, kn_ref, vp_ref, vc_ref, vn_ref, b_ref, o_ref, lse_ref, *, nblk_p, tblk_p, nblk_s):
    i = pl.program_id(1)
    in_p = i < tblk_p
    nblk = jnp.where(in_p, nblk_p, nblk_s)
    il = jnp.where(in_p, i, i - tblk_p) % nblk
    prev_ok = il > 0
    next_ok = il < nblk - 1
    scale = HEAD_DIM ** -0.5
    dn = (((1,), (1,)), ((), ()))
    for h in range(A_HEADS_PER_GROUP):
        sl = slice(h * HEAD_DIM, (h + 1) * HEAD_DIM)
        q = q_ref[:, sl]
        b = b_ref[h]
        sp = lax.dot_general(q, kp_ref[:, sl], dn, preferred_element_type=F32) * scale + b[:, 0:A_BQ]
        sc = lax.dot_general(q, kc_ref[:, sl], dn, preferred_element_type=F32) * scale + b[:, A_BQ:2 * A_BQ]
        sn = lax.dot_general(q, kn_ref[:, sl], dn, preferred_element_type=F32) * scale + b[:, 2 * A_BQ:3 * A_BQ]
        sp = jnp.where(prev_ok, sp, NEG_INF)
        sn = jnp.where(next_ok, sn, NEG_INF)
        m = jnp.maximum(jnp.maximum(jnp.max(sp, axis=-1, keepdims=True), jnp.max(sc, axis=-1, keepdims=True)),
                        jnp.max(sn, axis=-1, keepdims=True))
        pp, pc, pn = jnp.exp(sp - m), jnp.exp(sc - m), jnp.exp(sn - m)
        l = (jnp.sum(pp, axis=-1, keepdims=True) + jnp.sum(pc, axis=-1, keepdims=True)
             + jnp.sum(pn, axis=-1, keepdims=True))
        o = (jnp.dot(pp.astype(BF16), vp_ref[:, sl], preferred_element_type=F32)
             + jnp.dot(pc.astype(BF16), vc_ref[:, sl], preferred_element_type=F32)
             + jnp.dot(pn.astype(BF16), vn_ref[:, sl], preferred_element_type=F32))
        o_ref[:, sl] = (o / l).astype(o_ref.dtype)
        lse_ref[:, sl] = jnp.broadcast_to(m + jnp.log(l), (A_BQ, HEAD_DIM))


def _dilated_group(proj, bias, lay, g):
    w, d = DIL_CONFIGS[g]
    t = proj.shape[0]
    rows = t // d
    nblk_p = lay.s_p // d // A_BQ
    nblk_s = lay.s_s // d // A_BQ
    assert nblk_p >= 1 and nblk_s >= 1 and w // (2 * d) <= A_BQ
    tblk_p = lay.t_p // d // A_BQ
    tblk = rows // A_BQ
    gw = A_HEADS_PER_GROUP * HEAD_DIM
    q_c, k_c, v_c = [proj[:, off + g * gw:off + (g + 1) * gw].reshape(rows, d * gw)
                     for off in (OFF_AQ, OFF_AK, OFF_AV)]

    def spec(shift):
        return pl.BlockSpec((A_BQ, gw), lambda c, i: (jnp.clip(i + shift, 0, tblk - 1), c))

    kern = functools.partial(_dil_kernel, nblk_p=nblk_p, tblk_p=tblk_p, nblk_s=nblk_s)
    o, lse = pl.pallas_call(
        kern,
        grid=(d, tblk),
        in_specs=[spec(0), spec(-1), spec(0), spec(1), spec(-1), spec(0), spec(1),
                  pl.BlockSpec((A_HEADS_PER_GROUP, A_BQ, 3 * A_BQ), lambda c, i: (0, 0, 0))],
        out_specs=[spec(0), spec(0)],
        out_shape=[jax.ShapeDtypeStruct((rows, d * gw), BF16), jax.ShapeDtypeStruct((rows, d * gw), F32)],
        compiler_params=_cparams("parallel", "parallel"),
        name=f"a_dilated_g{g}",
    )(q_c, k_c, k_c, k_c, v_c, v_c, v_c, bias)
    return o.reshape(t, gw), lse.reshape(t, gw)


C_QROWS = 8
C_KROWS = 2 * NA_ROWS
C_TQ = C_QROWS * GRID_W
C_TK = C_KROWS * GRID_W


def _na_bias(rpb):
    hi = lax.Precision.HIGHEST
    qc = jnp.arange(GRID_W)[:, None]
    kc = jnp.arange(GRID_W)[None, :]
    cstart = jnp.clip(qc - NA_COLS // 2, 0, GRID_W - NA_COLS)
    col_ok = (kc >= cstart) & (kc < cstart + NA_COLS)
    ci = jnp.clip(kc - qc, -(NA_COLS - 1), NA_COLS - 1) + NA_COLS - 1
    oh_c = (ci[:, :, None] == jnp.arange(2 * NA_COLS - 1)[None, None, :]).astype(F32)
    by_col = jnp.einsum("hrc,abc->hrab", rpb.astype(F32), oh_c, precision=hi)
    out = []
    for off in (0, NA_ROWS // 2, NA_ROWS):
        qr = (off + jnp.arange(C_QROWS))[:, None]
        kr = jnp.arange(C_KROWS)[None, :]
        rstart = jnp.clip(qr - NA_ROWS // 2, 0, C_KROWS - NA_ROWS)
        row_ok = (kr >= rstart) & (kr < rstart + NA_ROWS)
        ri = jnp.clip(kr - qr + NA_ROWS - 1, 0, 2 * NA_ROWS - 2)
        oh_r = (ri[:, :, None] == jnp.arange(2 * NA_ROWS - 1)[None, None, :]).astype(F32)
        b = jnp.einsum("qkr,hrab->hqakb", oh_r, by_col, precision=hi)
        ok = row_ok[:, None, :, None] & col_ok[None, :, None, :]
        out.append(jnp.where(ok[None], b, NEG_INF).reshape(C_HEADS, C_TQ, C_TK))
    return jnp.stack(out)


def _na_kernel(q_ref, k_ref, v_ref, b_ref, o_ref):
    s = lax.dot_general(q_ref[...], k_ref[...], (((1,), (1,)), ((), ())), preferred_element_type=F32)
    s = s * (HEAD_DIM ** -0.5) + b_ref[...]
    m = jnp.max(s, axis=-1, keepdims=True)
    p = jnp.exp(s - m)
    l = jnp.sum(p, axis=-1, keepdims=True)
    o = jnp.dot(p.astype(BF16), v_ref[...], preferred_element_type=F32)
    o_ref[...] = (o / l).astype(o_ref.dtype)


def _neighbourhood(proj, rpb, lay):
    t = proj.shape[0]
    r_p, r_s = lay.s_p // GRID_W, lay.s_s // GRID_W
    assert r_p >= C_KROWS and r_s >= C_KROWS and r_p % C_QROWS == 0 and r_s % C_QROWS == 0
    blk_p = lay.t_p // C_TQ
    bias = _na_bias(rpb)

    def window(i):
        in_p = i < blk_p
        per_seq = jnp.where(in_p, lay.s_p // C_TQ, lay.s_s // C_TQ)
        rows = jnp.where(in_p, r_p, r_s)
        il = jnp.where(in_p, i, i - blk_p)
        seq0 = (i - il % per_seq) * C_TQ
        r0 = (il % per_seq) * C_QROWS
        w0 = jnp.clip(r0 - NA_ROWS // 2, 0, rows - C_KROWS)
        return seq0 + w0 * GRID_W, (r0 - w0) // (NA_ROWS // 2)

    def kv_spec(off):
        return pl.BlockSpec((pl.Element(C_TK), pl.Element(HEAD_DIM)),
                            lambda h, i: (pl.multiple_of(window(i)[0], GRID_W),
                                          pl.multiple_of(off + h * HEAD_DIM, LANES)))

    return pl.pallas_call(
        _na_kernel,
        grid=(C_HEADS, t // C_TQ),
        in_specs=[pl.BlockSpec((C_TQ, HEAD_DIM), lambda h, i: (i, OFF_CQ // HEAD_DIM + h)),
                  kv_spec(OFF_CK), kv_spec(OFF_CV),
                  pl.BlockSpec((None, None, C_TQ, C_TK), lambda h, i: (window(i)[1], h, 0, 0))],
        out_specs=pl.BlockSpec((C_TQ, HEAD_DIM), lambda h, i: (i, h)),
        out_shape=jax.ShapeDtypeStruct((t, C_OUT), BF16),
        compiler_params=_cparams("parallel", "parallel"),
        name="c_neighbourhood",
    )(proj, proj, proj, bias)


RET_C = 256


def _log_sigmoid(x):
    return jnp.minimum(x, 0.0) - jnp.log(1.0 + jnp.exp(-jnp.abs(x)))


def _ret_qk(q_ref, k_ref, cos_ref, sin_ref, h):
    sl = slice(h * D_QK, (h + 1) * D_QK)
    cos, sin = cos_ref[...], sin_ref[...]
    q = _rope_rotate(q_ref[:, sl].astype(F32), cos, sin)
    k = _rope_rotate(k_ref[:, sl].astype(F32), cos, sin) * (D_QK ** -0.5)
    return q, k


def _ret_bwd_kernel(cb_ref, first_ref, q_ref, k_ref, v_ref, cos_ref, sin_ref, dl_ref, o_ref, st_ref):
    s_id = pl.program_id(0)

    @pl.when(first_ref[s_id] == 1)
    def _():
        st_ref[...] = jnp.zeros(st_ref.shape, F32)

    row = lax.broadcasted_iota(I32, (RET_C, 1), 0).astype(F32)
    for h in range(D_HEADS):
        lg = _log_sigmoid(dl_ref[D_HEADS + h:D_HEADS + h + 1, :])
        q, k = _ret_qk(q_ref, k_ref, cos_ref, sin_ref, h)
        v = v_ref[:, h * D_V:(h + 1) * D_V]
        st = st_ref[h]
        q_dec = (q * jnp.exp((RET_C - row) * lg)).astype(BF16)
        o_ref[:, h * D_V:(h + 1) * D_V] = jnp.dot(q_dec, st.astype(BF16), preferred_element_type=F32)
        k_dec = (k * jnp.exp(row * lg)).astype(BF16)
        kv = lax.dot_general(k_dec, v, (((0,), (0,)), ((), ())), preferred_element_type=F32)
        st_ref[h] = st * jnp.exp(RET_C * lg[:, 0:1]) + kv


def _ret_fwd_kernel(cb_ref, first_ref, q_ref, k_ref, v_ref, g_ref, xb_ref, cos_ref, sin_ref, dl_ref, o_ref, st_ref):
    s_id = pl.program_id(0)

    @pl.when(first_ref[s_id] == 1)
    def _():
        st_ref[...] = jnp.zeros(st_ref.shape, F32)

    row = lax.broadcasted_iota(I32, (RET_C, 1), 0).astype(F32)
    diff = (lax.broadcasted_iota(I32, (RET_C, RET_C), 0) - lax.broadcasted_iota(I32, (RET_C, RET_C), 1)).astype(F32)
    for h in range(D_HEADS):
        lgf = _log_sigmoid(dl_ref[h:h + 1, :])
        lgb = _log_sigmoid(dl_ref[D_HEADS + h:D_HEADS + h + 1, :])
        q, k = _ret_qk(q_ref, k_ref, cos_ref, sin_ref, h)
        v = v_ref[:, h * D_V:(h + 1) * D_V]
        st = st_ref[h]
        dmat = jnp.where(diff >= 0, jnp.exp(jnp.maximum(diff, 0.0) * lgf[:, 0:1]),
                         jnp.exp(jnp.maximum(-diff, 0.0) * lgb[:, 0:1]))
        s = lax.dot_general(q.astype(BF16), k.astype(BF16), (((1,), (1,)), ((), ())), preferred_element_type=F32)
        o = jnp.dot((s * dmat).astype(BF16), v, preferred_element_type=F32)
        q_dec = (q * jnp.exp((row + 1.0) * lgf)).astype(BF16)
        o = o + jnp.dot(q_dec, st.astype(BF16), preferred_element_type=F32)
        o = o + xb_ref[:, h * D_V:(h + 1) * D_V]
        k_dec = (k * jnp.exp((RET_C - 1.0 - row) * lgf)).astype(BF16)
        kv = lax.dot_general(k_dec, v, (((0,), (0,)), ((), ())), preferred_element_type=F32)
        st_ref[h] = st * jnp.exp(RET_C * lgf[:, 0:1]) + kv
        mu = jnp.mean(o, axis=-1, keepdims=True)
        var = jnp.mean(jnp.square(o - mu), axis=-1, keepdims=True)
        on = (o - mu) * lax.rsqrt(var + EPS)
        g = g_ref[:, h * D_V:(h + 1) * D_V].astype(F32)
        o_ref[:, h * D_V:(h + 1) * D_V] = (g * jax.nn.sigmoid(g) * on).astype(o_ref.dtype)


def _ret_tables(lay, reverse):
    cb, first = [], []
    for (n, s, base) in ((lay.n_p, lay.s_p, 0), (lay.n_s, lay.s_s, lay.t_p)):
        for b in range(n):
            nc = s // RET_C
            order = range(nc - 1, -1, -1) if reverse else range(nc)
            for j, c in enumerate(order):
                cb.append((base + b * s) // RET_C + c)
                first.append(int(j == 0))
    return jnp.asarray(cb, dtype=I32), jnp.asarray(first, dtype=I32)


def _retention(proj, dlogit, cos, sin, lay):
    t = proj.shape[0]
    dl = jnp.broadcast_to(dlogit.astype(F32).reshape(2 * D_HEADS, 1), (2 * D_HEADS, LANES))
    qw, vw = D_HEADS * D_QK, D_HEADS * D_V
    row_spec = lambda width, off: pl.BlockSpec((pl.Element(RET_C), pl.Element(width)),
                                               lambda s, cb, f: (pl.multiple_of(cb[s] * RET_C, RET_C), off))
    tab_spec = pl.BlockSpec((RET_C, LANES), lambda s, cb, f: (cb[s], 0))
    dl_spec = pl.BlockSpec((2 * D_HEADS, LANES), lambda s, cb, f: (0, 0))
    out_spec = pl.BlockSpec((RET_C, vw), lambda s, cb, f: (cb[s], 0))
    state = pltpu.VMEM((D_HEADS, D_QK, D_V), F32)

    cb, first = _ret_tables(lay, True)
    xb = pl.pallas_call(
        _ret_bwd_kernel,
        grid_spec=pltpu.PrefetchScalarGridSpec(
            num_scalar_prefetch=2, grid=(cb.shape[0],),
            in_specs=[row_spec(qw, OFF_DQ), row_spec(qw, OFF_DK), row_spec(vw, OFF_DV), tab_spec, tab_spec, dl_spec],
            out_specs=out_spec, scratch_shapes=[state]),
        out_shape=jax.ShapeDtypeStruct((t, vw), F32),
        compiler_params=_cparams("arbitrary"),
        name="d_retention_bwd",
    )(cb, first, proj, proj, proj, cos, sin, dl)

    cb, first = _ret_tables(lay, False)
    return pl.pallas_call(
        _ret_fwd_kernel,
        grid_spec=pltpu.PrefetchScalarGridSpec(
            num_scalar_prefetch=2, grid=(cb.shape[0],),
            in_specs=[row_spec(qw, OFF_DQ), row_spec(qw, OFF_DK), row_spec(vw, OFF_DV), row_spec(vw, OFF_DG),
                      out_spec, tab_spec, tab_spec, dl_spec],
            out_specs=out_spec, scratch_shapes=[state]),
        out_shape=jax.ShapeDtypeStruct((t, vw), BF16),
        compiler_params=_cparams("arbitrary"),
        name="d_retention_fwd",
    )(cb, first, proj, proj, proj, proj, xb, cos, sin, dl)


def _merge_kernel(oa0, oa1, oa2, ls0, ls1, ls2, ob, oc, od, g0, g1, g2, g3, wa, wb, wc, wd, o_ref, oa_ref):
    @pl.when(pl.program_id(1) == 0)
    def _():
        l0, l1, l2 = ls0[...], ls1[...], ls2[...]
        m = jnp.maximum(jnp.maximum(l0, l1), l2)
        e0, e1, e2 = jnp.exp(l0 - m), jnp.exp(l1 - m), jnp.exp(l2 - m)
        num = e0 * oa0[...].astype(F32) + e1 * oa1[...].astype(F32) + e2 * oa2[...].astype(F32)
        oa_ref[...] = (num / (e0 + e1 + e2)).astype(BF16)

    def term(gate, o, w):
        return jax.nn.sigmoid(gate[...].astype(F32)) * jnp.dot(o, w[...], preferred_element_type=F32)

    acc = term(g0, oa_ref[...], wa) + term(g1, ob[...], wb) + term(g2, oc[...], wc) + term(g3, od[...], wd)
    o_ref[...] = acc.astype(o_ref.dtype)


def _branch_merge(proj, a_parts, o_b, o_c, o_d, w_branch_bf16):
    t = proj.shape[0]
    tm, tn = 512, 512
    (oa0, ls0), (oa1, ls1), (oa2, ls2) = a_parts
    row = lambda width: pl.BlockSpec((tm, width), lambda i, j: (i, 0))
    gate = lambda b: pl.BlockSpec((pl.Element(tm), pl.Element(tn)),
                                  lambda i, j: (pl.multiple_of(i * tm, tm),
                                                pl.multiple_of(OFF_GATE + b * D_MODEL + j * tn, LANES)))
    wspec = lambda width: pl.BlockSpec((width, tn), lambda i, j: (0, j))
    offs = (0, A_OUT, A_OUT + B_OUT, A_OUT + B_OUT + C_OUT, MIX_WIDTH)
    ws = [w_branch_bf16[offs[b]:offs[b + 1]] for b in range(N_BRANCH)]
    return pl.pallas_call(
        _merge_kernel,
        grid=(t // tm, D_MODEL // tn),
        in_specs=[row(A_OUT)] * 6 + [row(B_OUT), row(C_OUT), row(D_OUT)]
                 + [gate(0), gate(1), gate(2), gate(3)]
                 + [wspec(A_OUT), wspec(B_OUT), wspec(C_OUT), wspec(D_OUT)],
        out_specs=pl.BlockSpec((tm, tn), lambda i, j: (i, j)),
        out_shape=jax.ShapeDtypeStruct((t, D_MODEL), BF16),
        scratch_shapes=[pltpu.VMEM((tm, A_OUT), BF16)],
        compiler_params=_cparams("parallel", "arbitrary"),
        name="branch_merge",
    )(oa0, oa1, oa2, ls0, ls1, ls2, o_b, o_c, o_d, proj, proj, proj, proj, *ws)


def _split3(x):
    hi = x.astype(BF16)
    lo = (x - hi.astype(F32)).astype(BF16)
    return hi, lo


def _outproj_kernel(mg_ref, x_ref, mod_ref, g_ref, w_ref, wr_hi, wr_lo, br_ref, xo_ref, h2_ref, lg_ref):
    y = jnp.dot(mg_ref[...], w_ref[...], preferred_element_type=F32)
    m = mod_ref[0]
    g = g_ref[...]
    yn = y * lax.rsqrt(jnp.mean(y * y, axis=-1, keepdims=True) + EPS) * g[1:2]
    x = x_ref[...] + m[2:3] * yn
    xo_ref[...] = x
    h2 = x * lax.rsqrt(jnp.mean(x * x, axis=-1, keepdims=True) + EPS) * g[2:3] * (1.0 + m[4:5]) + m[3:4]
    h2b = h2.astype(BF16)
    half = D_MODEL // 2
    lo = lax.shift_right_logical(pltpu.bitcast(h2b[:, :half].astype(F32), jnp.uint32), jnp.uint32(16))
    hi = pltpu.bitcast(h2b[:, half:].astype(F32), jnp.uint32)
    h2_ref[...] = hi | lo
    lg_ref[...] = (jnp.dot(h2b, wr_hi[...], preferred_element_type=F32)
                   + jnp.dot(h2b, wr_lo[...], preferred_element_type=F32) + br_ref[...])


def _outproj_residual(merged, x, modseg, gains, w_out_bf16, w_router, b_router):
    t = x.shape[0]
    tm = 256
    wr = jnp.pad(w_router.astype(F32), ((0, 0), (0, LANES - N_EXPERTS)))
    wr_hi, wr_lo = _split3(wr)
    br = jnp.pad(b_router.astype(F32), (0, LANES - N_EXPERTS)).reshape(1, LANES)
    gains8 = jnp.pad(gains.astype(F32), ((0, 4), (0, 0)))
    row = lambda width: pl.BlockSpec((tm, width), lambda i: (i, 0))
    full = lambda a, b: pl.BlockSpec((a, b), lambda i: (0, 0))
    return pl.pallas_call(
        _outproj_kernel,
        grid=(t // tm,),
        in_specs=[row(D_MODEL), row(D_MODEL),
                  pl.BlockSpec((1, 8, D_MODEL), lambda i: (i * tm // SEG, 0, 0)),
                  full(8, D_MODEL), full(D_MODEL, D_MODEL), full(D_MODEL, LANES), full(D_MODEL, LANES), full(1, LANES)],
        out_specs=[row(D_MODEL), row(D_MODEL // 2), row(LANES)],
        out_shape=[jax.ShapeDtypeStruct((t, D_MODEL), F32), jax.ShapeDtypeStruct((t, D_MODEL // 2), jnp.uint32),
                   jax.ShapeDtypeStruct((t, LANES), F32)],
        compiler_params=_cparams("parallel"),
        name="outproj_residual",
    )(merged, x, modseg, gains8, w_out_bf16, wr_hi, wr_lo, br)


R_TM = 512


def _route_kernel(lg_ref, ti_ref, tw_ref, rk_ref, cnt_ref, carry_ref):
    @pl.when(pl.program_id(0) == 0)
    def _():
        carry_ref[...] = jnp.zeros(carry_ref.shape, F32)

    lane = _lane_iota((R_TM, LANES))
    l = jnp.where(lane < N_EXPERTS, lg_ref[...], -jnp.inf)
    vals, idxs = [], []
    for _ in range(TOP_K):
        m = jnp.max(l, axis=-1, keepdims=True)
        idx = jnp.min(jnp.where(l == m, lane.astype(F32), float(LANES)), axis=-1, keepdims=True).astype(I32)
        vals.append(m)
        idxs.append(idx)
        l = jnp.where(lane == idx, -jnp.inf, l)
    es = [jnp.exp(v - vals[0]) for v in vals]
    den = es[0] + es[1] + es[2] + es[3]
    ti = jnp.zeros((R_TM, LANES), I32)
    tw = jnp.zeros((R_TM, LANES), F32)
    cnt = jnp.zeros((R_TM, LANES), F32)
    for k in range(TOP_K):
        ti = jnp.where(lane == k, idxs[k], ti)
        tw = jnp.where(lane == k, es[k] / den, tw)
        cnt = cnt + jnp.where(lane == idxs[k], 1.0, 0.0)
    ti_ref[...] = ti
    tw_ref[...] = tw
    r = lax.broadcasted_iota(I32, (R_TM, R_TM), 0)
    c = lax.broadcasted_iota(I32, (R_TM, R_TM), 1)
    tri = jnp.where(c < r, 1.0, 0.0).astype(BF16)
    before = jnp.dot(tri, cnt.astype(BF16), preferred_element_type=F32) + carry_ref[...]
    rk = jnp.zeros((R_TM, LANES), I32)
    for k in range(TOP_K):
        pos = jnp.sum(jnp.where(lane == idxs[k], before, 0.0), axis=-1, keepdims=True)
        rk = jnp.where(lane == k, pos.astype(I32), rk)
    rk_ref[...] = rk
    carry_ref[...] = carry_ref[...] + jnp.sum(cnt, axis=0, keepdims=True)
    cnt_ref[...] = jnp.broadcast_to(carry_ref[...], cnt_ref.shape)


def _route(logits):
    t = logits.shape[0]
    row = pl.BlockSpec((R_TM, LANES), lambda i: (i, 0))
    return pl.pallas_call(
        _route_kernel,
        grid=(t // R_TM,),
        in_specs=[row],
        out_specs=[row, row, row, pl.BlockSpec((8, LANES), lambda i: (0, 0))],
        out_shape=[jax.ShapeDtypeStruct((t, LANES), I32), jax.ShapeDtypeStruct((t, LANES), F32),
                   jax.ShapeDtypeStruct((t, LANES), I32), jax.ShapeDtypeStruct((8, LANES), F32)],
        scratch_shapes=[pltpu.VMEM((1, LANES), F32)],
        compiler_params=_cparams("arbitrary"),
        name="moe_route",
    )(logits)


E_TM = 512
DISP_TT = 512
COMB_TT = 256


def _dispatch_kernel(off_ref, ti_ref, rk_ref, h_ref, xs_in, xs_ref, sem):
    del xs_in

    def copy(a):
        dst = off_ref[ti_ref[a]] + rk_ref[a]
        return pltpu.make_async_copy(h_ref.at[pl.ds(a // TOP_K, 1)], xs_ref.at[pl.ds(dst, 1)], sem)

    def start(a, c):
        copy(a).start()
        return c

    def wait(a, c):
        copy(a).wait()
        return c

    lax.fori_loop(0, DISP_TT * TOP_K, start, 0, unroll=8)
    lax.fori_loop(0, DISP_TT * TOP_K, wait, 0, unroll=8)


def _dispatch(offsets, ti_flat, rk_flat, h2p, n_rows):
    t, width = h2p.shape
    xs0 = jnp.zeros((n_rows, width), h2p.dtype)
    blk = pl.BlockSpec((DISP_TT * TOP_K,), lambda i, off: (i,), memory_space=pltpu.SMEM)
    return pl.pallas_call(
        _dispatch_kernel,
        grid_spec=pltpu.PrefetchScalarGridSpec(
            num_scalar_prefetch=1, grid=(t // DISP_TT,),
            in_specs=[blk, blk, pl.BlockSpec((DISP_TT, width), lambda i, off: (i, 0)),
                      pl.BlockSpec(memory_space=pl.ANY)],
            out_specs=pl.BlockSpec(memory_space=pl.ANY),
            scratch_shapes=[pltpu.SemaphoreType.DMA(())]),
        out_shape=jax.ShapeDtypeStruct((n_rows, width), h2p.dtype),
        input_output_aliases={4: 0},
        compiler_params=_cparams("arbitrary"),
        name="moe_dispatch",
    )(offsets, ti_flat, rk_flat, h2p, xs0)


UP_TN = 1024
DN_TN = 1024
SEL_K = 256


def _weights_changed(te_ref, i):
    return (i == 0) | (te_ref[i] != te_ref[jnp.maximum(i - 1, 0)])


def _up_kernel(te_ref, nu_ref, x_ref, w_ref, b_ref, sel_ref, o_ref, wbf_ref):
    i = pl.program_id(1)

    @pl.when(i < nu_ref[0])
    def _():
        @pl.when(_weights_changed(te_ref, i))
        def _():
            wbf_ref[...] = w_ref[0].astype(BF16)

        half = D_MODEL // 2
        xw = x_ref[...]
        x_lo = pltpu.bitcast(lax.shift_left(xw, jnp.uint32(16)), F32).astype(BF16)
        x_hi = pltpu.bitcast(xw & jnp.uint32(0xFFFF0000), F32).astype(BF16)
        gu = (jnp.dot(x_lo, wbf_ref[0:half, :], preferred_element_type=F32)
              + jnp.dot(x_hi, wbf_ref[half:, :], preferred_element_type=F32) + b_ref[0])
        glu = jnp.minimum(gu, SWIGLU_LIMIT)
        lin1 = jnp.clip(gu, -SWIGLU_LIMIT, SWIGLU_LIMIT) + 1.0
        act = (glu * jax.nn.sigmoid(SWIGLU_ALPHA * glu) * pltpu.roll(lin1, UP_TN - 1, 1)).astype(BF16)
        for c in range(UP_TN // SEL_K):
            o_ref[:, c * (SEL_K // 2):(c + 1) * (SEL_K // 2)] = jnp.dot(
                act[:, c * SEL_K:(c + 1) * SEL_K], sel_ref[...], preferred_element_type=F32).astype(o_ref.dtype)


def _expert_up(tile_e, n_used, xs, w_up, b_up, layer):
    n_rows = xs.shape[0]
    sel = (jnp.arange(SEL_K)[:, None] == 2 * jnp.arange(SEL_K // 2)[None, :]).astype(BF16)
    return pl.pallas_call(
        _up_kernel,
        grid_spec=pltpu.PrefetchScalarGridSpec(
            num_scalar_prefetch=2, grid=(2 * D_FF // UP_TN, n_rows // E_TM),
            in_specs=[pl.BlockSpec((E_TM, D_MODEL // 2), lambda j, i, te, nu: (i, 0)),
                      pl.BlockSpec((None, 1, D_MODEL, UP_TN), lambda j, i, te, nu: (layer, te[i], 0, j)),
                      pl.BlockSpec((None, 1, 1, UP_TN), lambda j, i, te, nu: (layer, te[i], 0, j)),
                      pl.BlockSpec((SEL_K, SEL_K // 2), lambda j, i, te, nu: (0, 0))],
            out_specs=pl.BlockSpec((E_TM, UP_TN // 2), lambda j, i, te, nu: (i, j)),
            scratch_shapes=[pltpu.VMEM((D_MODEL, UP_TN), BF16)]),
        out_shape=jax.ShapeDtypeStruct((n_rows, D_FF), BF16),
        compiler_params=_cparams("arbitrary", "arbitrary"),
        name="moe_up",
    )(tile_e, n_used, xs, w_up, b_up.reshape(DEPTH, N_EXPERTS, 1, 2 * D_FF), sel)


def _down_kernel(te_ref, nu_ref, a_ref, w_ref, b_ref, o_ref, wbf_ref):
    i = pl.program_id(1)

    @pl.when(i < nu_ref[0])
    def _():
        @pl.when(_weights_changed(te_ref, i))
        def _():
            wbf_ref[...] = w_ref[0].astype(BF16)

        o_ref[...] = jnp.dot(a_ref[...], wbf_ref[...], preferred_element_type=F32) + b_ref[0]


def _expert_down(tile_e, n_used, act, w_down, b_down, layer):
    n_rows = act.shape[0]
    return pl.pallas_call(
        _down_kernel,
        grid_spec=pltpu.PrefetchScalarGridSpec(
            num_scalar_prefetch=2, grid=(D_MODEL // DN_TN, n_rows // E_TM),
            in_specs=[pl.BlockSpec((E_TM, D_FF), lambda j, i, te, nu: (i, 0)),
                      pl.BlockSpec((None, 1, D_FF, DN_TN), lambda j, i, te, nu: (layer, te[i], 0, j)),
                      pl.BlockSpec((None, 1, 1, DN_TN), lambda j, i, te, nu: (layer, te[i], 0, j))],
            out_specs=pl.BlockSpec((E_TM, DN_TN), lambda j, i, te, nu: (i, j)),
            scratch_shapes=[pltpu.VMEM((D_FF, DN_TN), BF16)]),
        out_shape=jax.ShapeDtypeStruct((n_rows, D_MODEL), F32),
        compiler_params=_cparams("arbitrary", "arbitrary"),
        name="moe_down",
    )(tile_e, n_used, act, w_down, b_down.reshape(DEPTH, N_EXPERTS, 1, D_MODEL))


def _combine_kernel(off_ref, ti_ref, rk_ref, ys_ref, tw_ref, x_ref, mod_ref, g_ref, *rest, split_blk):
    o_refs, (buf, sem) = rest[:-2], rest[-2:]

    def copy(a):
        src = off_ref[ti_ref[a]] + rk_ref[a]
        return pltpu.make_async_copy(ys_ref.at[pl.ds(src, 1)], buf.at[a % TOP_K, pl.ds(a // TOP_K, 1)], sem)

    def start(a, c):
        copy(a).start()
        return c

    def wait(a, c):
        copy(a).wait()
        return c

    lax.fori_loop(0, COMB_TT * TOP_K, start, 0, unroll=8)
    lax.fori_loop(0, COMB_TT * TOP_K, wait, 0, unroll=8)
    tw = tw_ref[...]
    f = tw[:, 0:1] * buf[0]
    for k in range(1, TOP_K):
        f = f + tw[:, k:k + 1] * buf[k]
    m = mod_ref[0]
    fn = f * lax.rsqrt(jnp.mean(f * f, axis=-1, keepdims=True) + EPS) * g_ref[3:4]
    out = x_ref[...] + m[5:6] * fn
    if split_blk is None:
        o_refs[0][...] = out
    else:
        @pl.when(pl.program_id(0) < split_blk)
        def _():
            o_refs[0][...] = out

        @pl.when(pl.program_id(0) >= split_blk)
        def _():
            o_refs[1][...] = out


def _combine(offsets, ti_flat, rk_flat, ys, tw, x, modseg, gains, split_rows=None):
    t = x.shape[0]
    gains8 = jnp.pad(gains.astype(F32), ((0, 4), (0, 0)))
    blk = pl.BlockSpec((COMB_TT * TOP_K,), lambda i, off: (i,), memory_space=pltpu.SMEM)
    if split_rows is None:
        split_blk = None
        out_specs = pl.BlockSpec((COMB_TT, D_MODEL), lambda i, off: (i, 0))
        out_shape = jax.ShapeDtypeStruct((t, D_MODEL), F32)
    else:
        split_blk = split_rows // COMB_TT
        assert split_rows % COMB_TT == 0 and 0 < split_blk < t // COMB_TT
        out_specs = [pl.BlockSpec((COMB_TT, D_MODEL), lambda i, off: (jnp.minimum(i, split_blk - 1), 0)),
                     pl.BlockSpec((COMB_TT, D_MODEL), lambda i, off: (jnp.maximum(i - split_blk, 0), 0))]
        out_shape = [jax.ShapeDtypeStruct((split_rows, D_MODEL), F32),
                     jax.ShapeDtypeStruct((t - split_rows, D_MODEL), F32)]
    return pl.pallas_call(
        functools.partial(_combine_kernel, split_blk=split_blk),
        grid_spec=pltpu.PrefetchScalarGridSpec(
            num_scalar_prefetch=1, grid=(t // COMB_TT,),
            in_specs=[blk, blk, pl.BlockSpec(memory_space=pl.ANY),
                      pl.BlockSpec((COMB_TT, LANES), lambda i, off: (i, 0)),
                      pl.BlockSpec((COMB_TT, D_MODEL), lambda i, off: (i, 0)),
                      pl.BlockSpec((1, 8, D_MODEL), lambda i, off: (i * COMB_TT // SEG, 0, 0)),
                      pl.BlockSpec((8, D_MODEL), lambda i, off: (0, 0))],
            out_specs=out_specs,
            scratch_shapes=[pltpu.VMEM((TOP_K, COMB_TT, D_MODEL), F32), pltpu.SemaphoreType.DMA(())]),
        out_shape=out_shape,
        compiler_params=_cparams("arbitrary"),
        name="moe_combine",
    )(offsets, ti_flat, rk_flat, ys, tw, x, modseg, gains8)


def _moe(h2p, logits, x, modseg, gains, w_up, b_up, w_down, b_down, layer, split_rows=None):
    t = h2p.shape[0]
    ti, tw, rk, cnt = _route(logits)
    counts = cnt[0, :N_EXPERTS].astype(I32)
    padded = (counts + E_TM - 1) // E_TM * E_TM
    ends = jnp.cumsum(padded)
    offsets = ends - padded
    n_tiles = t * TOP_K // E_TM + N_EXPERTS
    tile_e = jnp.minimum(jnp.searchsorted(ends, jnp.arange(n_tiles, dtype=I32) * E_TM, side="right"),
                         N_EXPERTS - 1).astype(I32)
    n_used = (ends[-1:] // E_TM).astype(I32)
    ti_flat = ti[:, :TOP_K].reshape(-1)
    rk_flat = rk[:, :TOP_K].reshape(-1)
    xs = _dispatch(offsets, ti_flat, rk_flat, h2p, n_tiles * E_TM)
    act = _expert_up(tile_e, n_used, xs, w_up, b_up, layer)
    ys = _expert_down(tile_e, n_used, act, w_down, b_down, layer)
    return _combine(offsets, ti_flat, rk_flat, ys, tw, x, modseg, gains, split_rows)


def _forward(x, c8, seg_rows, lay, t5_bias, w_mod, b_mod, norm_gains, w_in, qk_norm_gains, na_rpb,
             ret_decay_logit, w_branch, w_out, w_router, b_router, w_up, b_up, w_down, b_down):
    mod = _modulation(c8, w_mod, b_mod)
    modseg = mod[:, seg_rows, :].reshape(DEPTH, len(seg_rows), 6, D_MODEL)
    modseg = jnp.pad(modseg, ((0, 0), (0, 0), (0, 2), (0, 0)))
    cos_a, sin_a = _axial_tables(lay)
    cos_r, sin_r = _rope_tables(lay)
    dil_bias = [_dil_bias(t5_bias, g, d, w // (2 * d)) for g, (w, d) in enumerate(DIL_CONFIGS)]
    for l in range(DEPTH):
        gains = norm_gains[l]
        proj = _norm_inproj(x, modseg[l], gains[0], w_in[l].astype(BF16))
        a_parts = [_dilated_group(proj, dil_bias[g], lay, g) for g in range(A_GROUPS)]
        o_b = _axial_gqa(proj, _b_prepare(proj, qk_norm_gains[l], cos_a, sin_a), lay)
        o_c = _neighbourhood(proj, na_rpb[l], lay)
        o_d = _retention(proj, ret_decay_logit[l], cos_r, sin_r, lay)
        merged = _branch_merge(proj, a_parts, o_b, o_c, o_d, w_branch[l].astype(BF16))
        x, h2p, logits = _outproj_residual(merged, x, modseg[l], gains, w_out[l].astype(BF16), w_router[l], b_router[l])
        x = _moe(h2p, logits, x, modseg[l], gains, w_up, b_up, w_down, b_down, l,
                 split_rows=lay.t_p if l == DEPTH - 1 else None)
    return x


def kernel(x_prompt, x_sample, c_prompt, c_sample, t5_bias, w_mod, b_mod, norm_gains, w_in, qk_norm_gains,
           na_rpb, ret_decay_logit, w_branch, w_out, w_router, b_router, w_up, b_up, w_down, b_down):
    n_p, s_p, _ = x_prompt.shape
    n_s, s_s, _ = x_sample.shape
    lay = Layout(n_p, s_p, n_s, s_s)
    assert s_p % SEG == 0 and s_s % SEG == 0 and n_p + n_s <= 8
    x = jnp.concatenate([x_prompt.reshape(-1, D_MODEL), x_sample.reshape(-1, D_MODEL)], axis=0)
    c8 = jnp.concatenate([c_prompt, c_sample, jnp.zeros((8 - n_p - n_s, D_MODEL), F32)], axis=0)
    seg_rows = tuple([b for b in range(n_p) for _ in range(s_p // SEG)]
                     + [n_p + b for b in range(n_s) for _ in range(s_s // SEG)])
    y_p, y_s = _forward(x, c8, jnp.asarray(seg_rows, dtype=I32), lay, t5_bias, w_mod, b_mod, norm_gains, w_in,
                        qk_norm_gains, na_rpb, ret_decay_logit, w_branch, w_out, w_router, b_router,
                        w_up, b_up, w_down, b_down)
    return (y_p.reshape(n_p, s_p, D_MODEL), y_s.reshape(n_s, s_s, D_MODEL))
```

```python
import functools
import math
from typing import NamedTuple

import jax
import jax.numpy as jnp
from jax import lax
from jax.experimental import pallas as pl
from jax.experimental.pallas import tpu as pltpu

F32 = jnp.float32
BF16 = jnp.bfloat16
I32 = jnp.int32

D_MODEL = 2048
DEPTH = 2
HEAD_DIM = 128
GRID_W = 64
EPS = 1e-6
ROPE_THETA = 10000.0
NEG_INF = -1e30
LOG2E = 1.4426950408889634

DIL_CONFIGS = ((128, 1), (512, 4), (2048, 16))
A_GROUPS = 3
A_HEADS_PER_GROUP = 6
A_HEADS = A_GROUPS * A_HEADS_PER_GROUP
T5_BUCKETS = 32
T5_MAX_DIST = 1024
B_Q_HEADS = 6
B_KV_HEADS = 2
C_HEADS = 6
NA_ROWS = 8
NA_COLS = 16
D_HEADS = 4
D_QK = 128
D_V = 256
N_EXPERTS = 32
TOP_K = 4
D_FF = 2048
SWIGLU_LIMIT = 7.0
SWIGLU_ALPHA = 1.702
N_BRANCH = 4

A_OUT = A_HEADS_PER_GROUP * HEAD_DIM
B_OUT = B_Q_HEADS * HEAD_DIM
C_OUT = C_HEADS * HEAD_DIM
D_OUT = D_HEADS * D_V
MIX_WIDTH = A_OUT + B_OUT + C_OUT + D_OUT

IN_SPLITS = (A_HEADS * HEAD_DIM, A_HEADS * HEAD_DIM, A_HEADS * HEAD_DIM,
             B_Q_HEADS * HEAD_DIM, B_KV_HEADS * HEAD_DIM, B_KV_HEADS * HEAD_DIM,
             C_HEADS * HEAD_DIM, C_HEADS * HEAD_DIM, C_HEADS * HEAD_DIM,
             D_HEADS * D_QK, D_HEADS * D_QK, D_HEADS * D_V, D_HEADS * D_V,
             N_BRANCH * D_MODEL)
N_IN = sum(IN_SPLITS)
_OFF = [0]
for _w in IN_SPLITS:
    _OFF.append(_OFF[-1] + _w)
(OFF_AQ, OFF_AK, OFF_AV, OFF_BQ, OFF_BK, OFF_BV, OFF_CQ, OFF_CK, OFF_CV,
 OFF_DQ, OFF_DK, OFF_DV, OFF_DG, OFF_GATE) = _OFF[:-1]

SEG = 2048
LANES = 128
VMEM_LIMIT = 56 * 1024 * 1024


class Layout(NamedTuple):
    n_p: int
    s_p: int
    n_s: int
    s_s: int

    @property
    def t_p(self):
        return self.n_p * self.s_p

    @property
    def t(self):
        return self.n_p * self.s_p + self.n_s * self.s_s


def _cparams(*sem):
    return pltpu.CompilerParams(dimension_semantics=sem, vmem_limit_bytes=VMEM_LIMIT)


def _mod_kernel(c_ref, w_ref, b_ref, o_ref):
    c = c_ref[...]
    s = (c * jax.nn.sigmoid(c)).astype(BF16)
    o_ref[0] = jnp.dot(s, w_ref[0].astype(BF16), preferred_element_type=F32) + b_ref[0]


def _modulation(c8, w_mod, b_mod):
    tn = 1024
    n6 = 6 * D_MODEL
    return pl.pallas_call(
        _mod_kernel,
        grid=(DEPTH, n6 // tn),
        in_specs=[pl.BlockSpec((8, D_MODEL), lambda l, j: (0, 0)),
                  pl.BlockSpec((1, D_MODEL, tn), lambda l, j: (l, 0, j)),
                  pl.BlockSpec((1, 1, tn), lambda l, j: (l, 0, j))],
        out_specs=pl.BlockSpec((1, 8, tn), lambda l, j: (l, 0, j)),
        out_shape=jax.ShapeDtypeStruct((DEPTH, 8, n6), F32),
        compiler_params=_cparams("parallel", "parallel"),
        name="adaln_mod",
    )(c8, w_mod, b_mod.reshape(DEPTH, 1, n6))


def _inproj_kernel(x_ref, mod_ref, g_ref, w_ref, o_ref, h_ref):
    @pl.when(pl.program_id(1) == 0)
    def _():
        x = x_ref[...]
        y = x * lax.rsqrt(jnp.mean(x * x, axis=-1, keepdims=True) + EPS) * g_ref[...]
        m = mod_ref[0]
        h_ref[...] = (y * (1.0 + m[1:2]) + m[0:1]).astype(BF16)

    o_ref[...] = jnp.dot(h_ref[...], w_ref[...], preferred_element_type=F32).astype(o_ref.dtype)


def _norm_inproj(x, modseg, gain, w_in_bf16):
    t = x.shape[0]
    tm, tn = 1024, 1280
    assert t % tm == 0 and N_IN % tn == 0 and SEG % tm == 0
    return pl.pallas_call(
        _inproj_kernel,
        grid=(t // tm, N_IN // tn),
        in_specs=[pl.BlockSpec((tm, D_MODEL), lambda i, j: (i, 0)),
                  pl.BlockSpec((1, 8, D_MODEL), lambda i, j: (i * tm // SEG, 0, 0)),
                  pl.BlockSpec((1, D_MODEL), lambda i, j: (0, 0)),
                  pl.BlockSpec((D_MODEL, tn), lambda i, j: (0, j))],
        out_specs=pl.BlockSpec((tm, tn), lambda i, j: (i, j)),
        out_shape=jax.ShapeDtypeStruct((t, N_IN), BF16),
        scratch_shapes=[pltpu.VMEM((tm, D_MODEL), BF16)],
        compiler_params=_cparams("parallel", "arbitrary"),
        name="norm_inproj",
    )(x, modseg, gain.reshape(1, D_MODEL), w_in_bf16)


def _local_pos(lay):
    return jnp.concatenate([jnp.tile(jnp.arange(lay.s_p), lay.n_p), jnp.tile(jnp.arange(lay.s_s), lay.n_s)])


def _axial_tables(lay):
    pos = _local_pos(lay)
    lane = jnp.arange(LANES)
    quarter = HEAD_DIM // 4
    freqs = ROPE_THETA ** (-jnp.arange(quarter, dtype=F32) / quarter)
    f = freqs[lane % quarter]
    p = jnp.where(lane[None, :] < HEAD_DIM // 2, (pos // GRID_W)[:, None], (pos % GRID_W)[:, None]).astype(F32)
    ang = p * f[None, :]
    sign = jnp.where((lane % (2 * quarter)) < quarter, -1.0, 1.0).astype(F32)
    return jnp.cos(ang), jnp.sin(ang) * sign[None, :]


def _rope_tables(lay):
    pos = _local_pos(lay)
    lane = jnp.arange(LANES)
    half = D_QK // 2
    freqs = ROPE_THETA ** (-jnp.arange(half, dtype=F32) / half)
    ang = pos.astype(F32)[:, None] * freqs[lane % half][None, :]
    sign = jnp.where(lane < half, -1.0, 1.0).astype(F32)
    return jnp.cos(ang), jnp.sin(ang) * sign[None, :]


def _lane_iota(shape):
    return lax.broadcasted_iota(I32, shape, len(shape) - 1)


def _axial_rotate(x, cos, sin_signed):
    q = HEAD_DIM // 4
    lo = (_lane_iota(x.shape) % (2 * q)) < q
    partner = jnp.where(lo, pltpu.roll(x, LANES - q, 1), pltpu.roll(x, q, 1))
    return x * cos + partner * sin_signed


def _rope_rotate(x, cos, sin_signed):
    return x * cos + pltpu.roll(x, D_QK // 2, 1) * sin_signed


def _bprep_kernel(x_ref, g_ref, cos_ref, sin_ref, o_ref):
    x = x_ref[...].astype(F32)
    y = x * lax.rsqrt(jnp.mean(x * x, axis=-1, keepdims=True) + EPS) * g_ref[0]
    y = _axial_rotate(y, cos_ref[...], sin_ref[...])
    scale = jnp.where(pl.program_id(1) < B_Q_HEADS, HEAD_DIM ** -0.5 * LOG2E, 1.0)
    o_ref[...] = (y * scale).astype(o_ref.dtype)


def _b_prepare(proj, qk_g, cos, sin):
    t = proj.shape[0]
    tr = 1024
    nh = B_Q_HEADS + B_KV_HEADS
    g8 = jnp.concatenate([jnp.tile(qk_g[0:1], (B_Q_HEADS, 1)), jnp.tile(qk_g[1:2], (B_KV_HEADS, 1))]).reshape(nh, 1, HEAD_DIM)
    return pl.pallas_call(
        _bprep_kernel,
        grid=(t // tr, nh),
        in_specs=[pl.BlockSpec((tr, HEAD_DIM), lambda i, h: (i, OFF_BQ // HEAD_DIM + h)),
                  pl.BlockSpec((1, 1, HEAD_DIM), lambda i, h: (h, 0, 0)),
                  pl.BlockSpec((tr, HEAD_DIM), lambda i, h: (i, 0)),
                  pl.BlockSpec((tr, HEAD_DIM), lambda i, h: (i, 0))],
        out_specs=pl.BlockSpec((tr, HEAD_DIM), lambda i, h: (i, h)),
        out_shape=jax.ShapeDtypeStruct((t, nh * HEAD_DIM), BF16),
        compiler_params=_cparams("parallel", "parallel"),
        name="b_prep",
    )(proj, g8, cos, sin)


def _flash_kernel(qt_ref, kt_ref, first_ref, last_ref, q_ref, k_ref, v_ref, o_ref, m_ref, acc_ref):
    s_id = pl.program_id(1)
    rep = B_Q_HEADS // B_KV_HEADS

    @pl.when(first_ref[s_id] == 1)
    def _():
        m_ref[...] = jnp.full(m_ref.shape, -jnp.inf, F32)
        acc_ref[...] = jnp.zeros(acc_ref.shape, F32)

    k = k_ref[...]
    v = v_ref[...]
    for r in range(rep):
        q = q_ref[:, r * HEAD_DIM:(r + 1) * HEAD_DIM]
        s = lax.dot_general(q, k, (((1,), (1,)), ((), ())), preferred_element_type=F32)
        m_prev = m_ref[r]
        m_cur = jnp.maximum(m_prev, jnp.max(s, axis=-1, keepdims=True))
        alpha = jnp.exp2(m_prev - m_cur)
        p = jnp.exp2(s - m_cur[:, 0:1]).astype(BF16)
        pv = jnp.dot(p, v, preferred_element_type=F32)
        acc_ref[r] = jnp.concatenate([alpha, alpha], axis=1) * acc_ref[r] + pv
        m_ref[r] = m_cur

    @pl.when(last_ref[s_id] == 1)
    def _():
        for r in range(rep):
            acc = acc_ref[r]
            o_ref[:, r * HEAD_DIM:(r + 1) * HEAD_DIM] = (acc[:, :HEAD_DIM] / acc[:, HEAD_DIM:]).astype(o_ref.dtype)


def _flash_tables(lay, tq, tk):
    qt, kt, first, last = [], [], [], []
    for (n, s, base) in ((lay.n_p, lay.s_p, 0), (lay.n_s, lay.s_s, lay.t_p)):
        for b in range(n):
            for qi in range(s // tq):
                nk = s // tk
                for ki in range(nk):
                    qt.append((base + b * s) // tq + qi)
                    kt.append((base + b * s) // tk + ki)
                    first.append(int(ki == 0))
                    last.append(int(ki == nk - 1))
    mk = lambda a: jnp.asarray(a, dtype=I32)
    return mk(qt), mk(kt), mk(first), mk(last)


def _axial_gqa(proj, bprep, lay):
    t = proj.shape[0]
    tq, tk = 512, 2048
    assert lay.s_p % tk == 0 and lay.s_s % tk == 0
    rep = B_Q_HEADS // B_KV_HEADS
    qt, kt, first, last = _flash_tables(lay, tq, tk)
    n_steps = qt.shape[0]
    v = proj[:, OFF_BV:OFF_BV + B_KV_HEADS * HEAD_DIM].reshape(t, B_KV_HEADS, HEAD_DIM)
    v_ones = jnp.concatenate([v, jnp.ones_like(v)], axis=-1).reshape(t, B_KV_HEADS * 2 * HEAD_DIM)
    gs = pltpu.PrefetchScalarGridSpec(
        num_scalar_prefetch=4,
        grid=(B_KV_HEADS, n_steps),
        in_specs=[pl.BlockSpec((tq, rep * HEAD_DIM), lambda g, s, qt, kt, f, l: (qt[s], g)),
                  pl.BlockSpec((tk, HEAD_DIM), lambda g, s, qt, kt, f, l: (kt[s], B_Q_HEADS + g)),
                  pl.BlockSpec((tk, 2 * HEAD_DIM), lambda g, s, qt, kt, f, l: (kt[s], g))],
        out_specs=pl.BlockSpec((tq, rep * HEAD_DIM), lambda g, s, qt, kt, f, l: (qt[s], g)),
        scratch_shapes=[pltpu.VMEM((rep, tq, LANES), F32), pltpu.VMEM((rep, tq, 2 * HEAD_DIM), F32)],
    )
    return pl.pallas_call(
        _flash_kernel,
        grid_spec=gs,
        out_shape=jax.ShapeDtypeStruct((t, B_OUT), BF16),
        compiler_params=_cparams("parallel", "arbitrary"),
        name="b_flash",
    )(qt, kt, first, last, bprep, bprep, v_ones)


A_BQ = 128
A_NSUB = 2


def _t5_bucket(rel):
    nb = T5_BUCKETS // 2
    max_exact = nb // 2
    n = jnp.abs(rel)
    large = max_exact + (jnp.log(jnp.maximum(n, 1).astype(F32) / max_exact)
                         / math.log(T5_MAX_DIST / max_exact) * (nb - max_exact)).astype(I32)
    large = jnp.minimum(large, nb - 1)
    return jnp.where(rel > 0, nb, 0) + jnp.where(n < max_exact, n, large)


def _dil_bias(t5_bias, g, d, half):
    rel = (jnp.arange(3 * A_BQ)[None, :] - A_BQ) - jnp.arange(A_BQ)[:, None]
    tab = t5_bias[:, g * A_HEADS_PER_GROUP:(g + 1) * A_HEADS_PER_GROUP].astype(F32)
    onehot = (_t5_bucket(rel * d)[:, :, None] == jnp.arange(T5_BUCKETS)[None, None, :]).astype(F32)
    bias = jnp.einsum("qkb,bh->hqk", onehot, tab, precision=lax.Precision.HIGHEST)
    return jnp.where((jnp.abs(rel) <= half)[None], bias, NEG_INF)


def _dil_kernel(q_ref, kp_ref, kc_ref, kn_ref, vp_ref, vc_ref, vn_ref, b_ref, o_ref, lse_ref, *, nblk_p, tblk_p, nblk_s):
    scale = HEAD_DIM ** -0.5
    dn = (((1,), (1,)), ((), ()))
    for s in range(A_NSUB):
        r = pl.program_id(1) * A_NSUB + s
        in_p = r < tblk_p
        nblk = jnp.where(in_p, nblk_p, nblk_s)
        il = jnp.where(in_p, r, r - tblk_p) % nblk
        prev_ok = il > 0
        next_ok = il < nblk - 1
        rows = slice(s * A_BQ, (s + 1) * A_BQ)
        before = (kp_ref, vp_ref, slice((A_NSUB - 1) * A_BQ, A_NSUB * A_BQ)) if s == 0 else \
            (kc_ref, vc_ref, slice((s - 1) * A_BQ, s * A_BQ))
        after = (kn_ref, vn_ref, slice(0, A_BQ)) if s == A_NSUB - 1 else \
            (kc_ref, vc_ref, slice((s + 1) * A_BQ, (s + 2) * A_BQ))
        for h in range(A_HEADS_PER_GROUP):
            sl = slice(h * HEAD_DIM, (h + 1) * HEAD_DIM)
            q = q_ref[rows, sl]
            b = b_ref[h]
            sp = lax.dot_general(q, before[0][before[2], sl], dn, preferred_element_type=F32) * scale + b[:, 0:A_BQ]
            sc = lax.dot_general(q, kc_ref[rows, sl], dn, preferred_element_type=F32) * scale + b[:, A_BQ:2 * A_BQ]
            sn = lax.dot_general(q, after[0][after[2], sl], dn, preferred_element_type=F32) * scale + b[:, 2 * A_BQ:]
            sp = jnp.where(prev_ok, sp, NEG_INF)
            sn = jnp.where(next_ok, sn, NEG_INF)
            m = jnp.maximum(jnp.maximum(jnp.max(sp, axis=-1, keepdims=True), jnp.max(sc, axis=-1, keepdims=True)),
                            jnp.max(sn, axis=-1, keepdims=True))
            pp, pc, pn = jnp.exp(sp - m), jnp.exp(sc - m), jnp.exp(sn - m)
            l = (jnp.sum(pp, axis=-1, keepdims=True) + jnp.sum(pc, axis=-1, keepdims=True)
                 + jnp.sum(pn, axis=-1, keepdims=True))
            o = (jnp.dot(pp.astype(BF16), before[1][before[2], sl], preferred_element_type=F32)
                 + jnp.dot(pc.astype(BF16), vc_ref[rows, sl], preferred_element_type=F32)
                 + jnp.dot(pn.astype(BF16), after[1][after[2], sl], preferred_element_type=F32))
            o_ref[rows, sl] = (o / l).astype(o_ref.dtype)
            lse_ref[rows, sl] = jnp.broadcast_to(m + jnp.log(l), (A_BQ, HEAD_DIM))


def _dilated_group(proj, bias, lay, g):
    w, d = DIL_CONFIGS[g]
    t = proj.shape[0]
    rows = t // d
    nblk_p = lay.s_p // d // A_BQ
    nblk_s = lay.s_s // d // A_BQ
    assert nblk_p >= 1 and nblk_s >= 1 and w // (2 * d) <= A_BQ
    tblk_p = lay.t_p // d // A_BQ
    br = A_BQ * A_NSUB
    assert rows % br == 0
    tblk = rows // br
    gw = A_HEADS_PER_GROUP * HEAD_DIM

    def spec(shift):
        return pl.BlockSpec((br, gw), lambda c, i: (jnp.clip(i + shift, 0, tblk - 1), c))

    if d == 1:
        def win(off, shift):
            return pl.BlockSpec(
                (pl.Element(br), pl.Element(gw)),
                lambda c, i: (pl.multiple_of(jnp.clip(i + shift, 0, tblk - 1) * br, br), off + g * gw))
        q_c = k_c = v_c = proj
        qkv_specs = [win(OFF_AQ, 0), win(OFF_AK, -1), win(OFF_AK, 0), win(OFF_AK, 1),
                     win(OFF_AV, -1), win(OFF_AV, 0), win(OFF_AV, 1)]
    else:
        q_c, k_c, v_c = [proj[:, off + g * gw:off + (g + 1) * gw].reshape(rows, d * gw)
                         for off in (OFF_AQ, OFF_AK, OFF_AV)]
        qkv_specs = [spec(0), spec(-1), spec(0), spec(1), spec(-1), spec(0), spec(1)]

    kern = functools.partial(_dil_kernel, nblk_p=nblk_p, tblk_p=tblk_p, nblk_s=nblk_s)
    o, lse = pl.pallas_call(
        kern,
        grid=(d, tblk),
        in_specs=qkv_specs + [pl.BlockSpec((A_HEADS_PER_GROUP, A_BQ, 3 * A_BQ), lambda c, i: (0, 0, 0))],
        out_specs=[spec(0), spec(0)],
        out_shape=[jax.ShapeDtypeStruct((rows, d * gw), BF16), jax.ShapeDtypeStruct((rows, d * gw), F32)],
        compiler_params=_cparams("parallel", "parallel"),
        name=f"a_dilated_g{g}",
    )(q_c, k_c, k_c, k_c, v_c, v_c, v_c, bias)
    return o.reshape(t, gw), lse.reshape(t, gw)


C_QROWS = 8
C_KROWS = 2 * NA_ROWS
C_TQ = C_QROWS * GRID_W
C_TK = C_KROWS * GRID_W


def _na_bias(rpb):
    hi = lax.Precision.HIGHEST
    qc = jnp.arange(GRID_W)[:, None]
    kc = jnp.arange(GRID_W)[None, :]
    cstart = jnp.clip(qc - NA_COLS // 2, 0, GRID_W - NA_COLS)
    col_ok = (kc >= cstart) & (kc < cstart + NA_COLS)
    ci = jnp.clip(kc - qc, -(NA_COLS - 1), NA_COLS - 1) + NA_COLS - 1
    oh_c = (ci[:, :, None] == jnp.arange(2 * NA_COLS - 1)[None, None, :]).astype(F32)
    by_col = jnp.einsum("hrc,abc->hrab", rpb.astype(F32), oh_c, precision=hi)
    out = []
    for off in (0, NA_ROWS // 2, NA_ROWS):
        qr = (off + jnp.arange(C_QROWS))[:, None]
        kr = jnp.arange(C_KROWS)[None, :]
        rstart = jnp.clip(qr - NA_ROWS // 2, 0, C_KROWS - NA_ROWS)
        row_ok = (kr >= rstart) & (kr < rstart + NA_ROWS)
        ri = jnp.clip(kr - qr + NA_ROWS - 1, 0, 2 * NA_ROWS - 2)
        oh_r = (ri[:, :, None] == jnp.arange(2 * NA_ROWS - 1)[None, None, :]).astype(F32)
        b = jnp.einsum("qkr,hrab->hqakb", oh_r, by_col, precision=hi)
        ok = row_ok[:, None, :, None] & col_ok[None, :, None, :]
        out.append(jnp.where(ok[None], b, NEG_INF).reshape(C_HEADS, C_TQ, C_TK))
    return jnp.stack(out)


def _na_kernel(q_ref, k_ref, v_ref, b_ref, o_ref):
    s = lax.dot_general(q_ref[...], k_ref[...], (((1,), (1,)), ((), ())), preferred_element_type=F32)
    s = s * (HEAD_DIM ** -0.5) + b_ref[...]
    m = jnp.max(s, axis=-1, keepdims=True)
    p = jnp.exp(s - m)
    l = jnp.sum(p, axis=-1, keepdims=True)
    o = jnp.dot(p.astype(BF16), v_ref[...], preferred_element_type=F32)
    o_ref[...] = (o / l).astype(o_ref.dtype)


def _neighbourhood(proj, rpb, lay):
    t = proj.shape[0]
    r_p, r_s = lay.s_p // GRID_W, lay.s_s // GRID_W
    assert r_p >= C_KROWS and r_s >= C_KROWS and r_p % C_QROWS == 0 and r_s % C_QROWS == 0
    blk_p = lay.t_p // C_TQ
    bias = _na_bias(rpb)

    def window(i):
        in_p = i < blk_p
        per_seq = jnp.where(in_p, lay.s_p // C_TQ, lay.s_s // C_TQ)
        rows = jnp.where(in_p, r_p, r_s)
        il = jnp.where(in_p, i, i - blk_p)
        seq0 = (i - il % per_seq) * C_TQ
        r0 = (il % per_seq) * C_QROWS
        w0 = jnp.clip(r0 - NA_ROWS // 2, 0, rows - C_KROWS)
        return seq0 + w0 * GRID_W, (r0 - w0) // (NA_ROWS // 2)

    def kv_spec(off):
        return pl.BlockSpec((pl.Element(C_TK), pl.Element(HEAD_DIM)),
                            lambda h, i: (pl.multiple_of(window(i)[0], GRID_W),
                                          pl.multiple_of(off + h * HEAD_DIM, LANES)))

    return pl.pallas_call(
        _na_kernel,
        grid=(C_HEADS, t // C_TQ),
        in_specs=[pl.BlockSpec((C_TQ, HEAD_DIM), lambda h, i: (i, OFF_CQ // HEAD_DIM + h)),
                  kv_spec(OFF_CK), kv_spec(OFF_CV),
                  pl.BlockSpec((None, None, C_TQ, C_TK), lambda h, i: (window(i)[1], h, 0, 0))],
        out_specs=pl.BlockSpec((C_TQ, HEAD_DIM), lambda h, i: (i, h)),
        out_shape=jax.ShapeDtypeStruct((t, C_OUT), BF16),
        compiler_params=_cparams("parallel", "parallel"),
        name="c_neighbourhood",
    )(proj, proj, proj, bias)


RET_C = 256


def _log_sigmoid(x):
    return jnp.minimum(x, 0.0) - jnp.log(1.0 + jnp.exp(-jnp.abs(x)))


def _ret_qk(q_ref, k_ref, cos_ref, sin_ref, h):
    sl = slice(h * D_QK, (h + 1) * D_QK)
    cos, sin = cos_ref[...], sin_ref[...]
    q = _rope_rotate(q_ref[:, sl].astype(F32), cos, sin)
    k = _rope_rotate(k_ref[:, sl].astype(F32), cos, sin) * (D_QK ** -0.5)
    return q, k


def _ret_bwd_kernel(cb_ref, first_ref, q_ref, k_ref, v_ref, cos_ref, sin_ref, dl_ref, o_ref, st_ref):
    s_id = pl.program_id(0)

    @pl.when(first_ref[s_id] == 1)
    def _():
        st_ref[...] = jnp.zeros(st_ref.shape, F32)

    row = lax.broadcasted_iota(I32, (RET_C, 1), 0).astype(F32)
    for h in range(D_HEADS):
        lg = _log_sigmoid(dl_ref[D_HEADS + h:D_HEADS + h + 1, :])
        q, k = _ret_qk(q_ref, k_ref, cos_ref, sin_ref, h)
        v = v_ref[:, h * D_V:(h + 1) * D_V]
        st = st_ref[h]
        q_dec = (q * jnp.exp((RET_C - row) * lg)).astype(BF16)
        o_ref[:, h * D_V:(h + 1) * D_V] = jnp.dot(q_dec, st.astype(BF16), preferred_element_type=F32)
        k_dec = (k * jnp.exp(row * lg)).astype(BF16)
        kv = lax.dot_general(k_dec, v, (((0,), (0,)), ((), ())), preferred_element_type=F32)
        st_ref[h] = st * jnp.exp(RET_C * lg[:, 0:1]) + kv


def _ret_fwd_kernel(cb_ref, first_ref, q_ref, k_ref, v_ref, g_ref, xb_ref, cos_ref, sin_ref, dl_ref, o_ref, st_ref):
    s_id = pl.program_id(0)

    @pl.when(first_ref[s_id] == 1)
    def _():
        st_ref[...] = jnp.zeros(st_ref.shape, F32)

    row = lax.broadcasted_iota(I32, (RET_C, 1), 0).astype(F32)
    diff = (lax.broadcasted_iota(I32, (RET_C, RET_C), 0) - lax.broadcasted_iota(I32, (RET_C, RET_C), 1)).astype(F32)
    for h in range(D_HEADS):
        lgf = _log_sigmoid(dl_ref[h:h + 1, :])
        lgb = _log_sigmoid(dl_ref[D_HEADS + h:D_HEADS + h + 1, :])
        q, k = _ret_qk(q_ref, k_ref, cos_ref, sin_ref, h)
        v = v_ref[:, h * D_V:(h + 1) * D_V]
        st = st_ref[h]
        dmat = jnp.where(diff >= 0, jnp.exp(jnp.maximum(diff, 0.0) * lgf[:, 0:1]),
                         jnp.exp(jnp.maximum(-diff, 0.0) * lgb[:, 0:1]))
        s = lax.dot_general(q.astype(BF16), k.astype(BF16), (((1,), (1,)), ((), ())), preferred_element_type=F32)
        o = jnp.dot((s * dmat).astype(BF16), v, preferred_element_type=F32)
        q_dec = (q * jnp.exp((row + 1.0) * lgf)).astype(BF16)
        o = o + jnp.dot(q_dec, st.astype(BF16), preferred_element_type=F32)
        o = o + xb_ref[:, h * D_V:(h + 1) * D_V]
        k_dec = (k * jnp.exp((RET_C - 1.0 - row) * lgf)).astype(BF16)
        kv = lax.dot_general(k_dec, v, (((0,), (0,)), ((), ())), preferred_element_type=F32)
        st_ref[h] = st * jnp.exp(RET_C * lgf[:, 0:1]) + kv
        mu = jnp.mean(o, axis=-1, keepdims=True)
        var = jnp.mean(jnp.square(o - mu), axis=-1, keepdims=True)
        on = (o - mu) * lax.rsqrt(var + EPS)
        g = g_ref[:, h * D_V:(h + 1) * D_V].astype(F32)
        o_ref[:, h * D_V:(h + 1) * D_V] = (g * jax.nn.sigmoid(g) * on).astype(o_ref.dtype)


def _ret_tables(lay, reverse):
    cb, first = [], []
    for (n, s, base) in ((lay.n_p, lay.s_p, 0), (lay.n_s, lay.s_s, lay.t_p)):
        for b in range(n):
            nc = s // RET_C
            order = range(nc - 1, -1, -1) if reverse else range(nc)
            for j, c in enumerate(order):
                cb.append((base + b * s) // RET_C + c)
                first.append(int(j == 0))
    return jnp.asarray(cb, dtype=I32), jnp.asarray(first, dtype=I32)


def _retention(proj, dlogit, cos, sin, lay):
    t = proj.shape[0]
    dl = jnp.broadcast_to(dlogit.astype(F32).reshape(2 * D_HEADS, 1), (2 * D_HEADS, LANES))
    qw, vw = D_HEADS * D_QK, D_HEADS * D_V
    row_spec = lambda width, off: pl.BlockSpec((pl.Element(RET_C), pl.Element(width)),
                                               lambda s, cb, f: (pl.multiple_of(cb[s] * RET_C, RET_C), off))
    tab_spec = pl.BlockSpec((RET_C, LANES), lambda s, cb, f: (cb[s], 0))
    dl_spec = pl.BlockSpec((2 * D_HEADS, LANES), lambda s, cb, f: (0, 0))
    out_spec = pl.BlockSpec((RET_C, vw), lambda s, cb, f: (cb[s], 0))
    state = pltpu.VMEM((D_HEADS, D_QK, D_V), F32)

    cb, first = _ret_tables(lay, True)
    xb = pl.pallas_call(
        _ret_bwd_kernel,
        grid_spec=pltpu.PrefetchScalarGridSpec(
            num_scalar_prefetch=2, grid=(cb.shape[0],),
            in_specs=[row_spec(qw, OFF_DQ), row_spec(qw, OFF_DK), row_spec(vw, OFF_DV), tab_spec, tab_spec, dl_spec],
            out_specs=out_spec, scratch_shapes=[state]),
        out_shape=jax.ShapeDtypeStruct((t, vw), F32),
        compiler_params=_cparams("arbitrary"),
        name="d_retention_bwd",
    )(cb, first, proj, proj, proj, cos, sin, dl)

    cb, first = _ret_tables(lay, False)
    return pl.pallas_call(
        _ret_fwd_kernel,
        grid_spec=pltpu.PrefetchScalarGridSpec(
            num_scalar_prefetch=2, grid=(cb.shape[0],),
            in_specs=[row_spec(qw, OFF_DQ), row_spec(qw, OFF_DK), row_spec(vw, OFF_DV), row_spec(vw, OFF_DG),
                      out_spec, tab_spec, tab_spec, dl_spec],
            out_specs=out_spec, scratch_shapes=[state]),
        out_shape=jax.ShapeDtypeStruct((t, vw), BF16),
        compiler_params=_cparams("arbitrary"),
        name="d_retention_fwd",
    )(cb, first, proj, proj, proj, proj, xb, cos, sin, dl)


def _merge_kernel(oa0, oa1, oa2, ls0, ls1, ls2, ob, oc, od, g0, g1, g2, g3, wa, wb, wc, wd, o_ref, oa_ref):
    @pl.when(pl.program_id(1) == 0)
    def _():
        l0, l1, l2 = ls0[...], ls1[...], ls2[...]
        m = jnp.maximum(jnp.maximum(l0, l1), l2)
        e0, e1, e2 = jnp.exp(l0 - m), jnp.exp(l1 - m), jnp.exp(l2 - m)
        num = e0 * oa0[...].astype(F32) + e1 * oa1[...].astype(F32) + e2 * oa2[...].astype(F32)
        oa_ref[...] = (num / (e0 + e1 + e2)).astype(BF16)

    def term(gate, o, w):
        return jax.nn.sigmoid(gate[...].astype(F32)) * jnp.dot(o, w[...], preferred_element_type=F32)

    acc = term(g0, oa_ref[...], wa) + term(g1, ob[...], wb) + term(g2, oc[...], wc) + term(g3, od[...], wd)
    o_ref[...] = acc.astype(o_ref.dtype)


def _branch_merge(proj, a_parts, o_b, o_c, o_d, w_branch_bf16):
    t = proj.shape[0]
    tm, tn = 512, 512
    (oa0, ls0), (oa1, ls1), (oa2, ls2) = a_parts
    row = lambda width: pl.BlockSpec((tm, width), lambda i, j: (i, 0))
    gate = lambda b: pl.BlockSpec((pl.Element(tm), pl.Element(tn)),
                                  lambda i, j: (pl.multiple_of(i * tm, tm),
                                                pl.multiple_of(OFF_GATE + b * D_MODEL + j * tn, LANES)))
    wspec = lambda width: pl.BlockSpec((width, tn), lambda i, j: (0, j))
    offs = (0, A_OUT, A_OUT + B_OUT, A_OUT + B_OUT + C_OUT, MIX_WIDTH)
    ws = [w_branch_bf16[offs[b]:offs[b + 1]] for b in range(N_BRANCH)]
    return pl.pallas_call(
        _merge_kernel,
        grid=(t // tm, D_MODEL // tn),
        in_specs=[row(A_OUT)] * 6 + [row(B_OUT), row(C_OUT), row(D_OUT)]
                 + [gate(0), gate(1), gate(2), gate(3)]
                 + [wspec(A_OUT), wspec(B_OUT), wspec(C_OUT), wspec(D_OUT)],
        out_specs=pl.BlockSpec((tm, tn), lambda i, j: (i, j)),
        out_shape=jax.ShapeDtypeStruct((t, D_MODEL), BF16),
        scratch_shapes=[pltpu.VMEM((tm, A_OUT), BF16)],
        compiler_params=_cparams("parallel", "arbitrary"),
        name="branch_merge",
    )(oa0, oa1, oa2, ls0, ls1, ls2, o_b, o_c, o_d, proj, proj, proj, proj, *ws)


def _split3(x):
    hi = x.astype(BF16)
    lo = (x - hi.astype(F32)).astype(BF16)
    return hi, lo


def _outproj_kernel(mg_ref, x_ref, mod_ref, g_ref, w_ref, wr_hi, wr_lo, br_ref, xo_ref, h2_ref, lg_ref):
    y = jnp.dot(mg_ref[...], w_ref[...], preferred_element_type=F32)
    m = mod_ref[0]
    g = g_ref[...]
    yn = y * lax.rsqrt(jnp.mean(y * y, axis=-1, keepdims=True) + EPS) * g[1:2]
    x = x_ref[...] + m[2:3] * yn
    xo_ref[...] = x
    h2 = x * lax.rsqrt(jnp.mean(x * x, axis=-1, keepdims=True) + EPS) * g[2:3] * (1.0 + m[4:5]) + m[3:4]
    h2b = h2.astype(BF16)
    half = D_MODEL // 2
    lo = lax.shift_right_logical(pltpu.bitcast(h2b[:, :half].astype(F32), jnp.uint32), jnp.uint32(16))
    hi = pltpu.bitcast(h2b[:, half:].astype(F32), jnp.uint32)
    h2_ref[...] = hi | lo
    lg_ref[...] = (jnp.dot(h2b, wr_hi[...], preferred_element_type=F32)
                   + jnp.dot(h2b, wr_lo[...], preferred_element_type=F32) + br_ref[...])


def _outproj_residual(merged, x, modseg, gains, w_out_bf16, w_router, b_router):
    t = x.shape[0]
    tm = 256
    wr = jnp.pad(w_router.astype(F32), ((0, 0), (0, LANES - N_EXPERTS)))
    wr_hi, wr_lo = _split3(wr)
    br = jnp.pad(b_router.astype(F32), (0, LANES - N_EXPERTS)).reshape(1, LANES)
    gains8 = jnp.pad(gains.astype(F32), ((0, 4), (0, 0)))
    row = lambda width: pl.BlockSpec((tm, width), lambda i: (i, 0))
    full = lambda a, b: pl.BlockSpec((a, b), lambda i: (0, 0))
    return pl.pallas_call(
        _outproj_kernel,
        grid=(t // tm,),
        in_specs=[row(D_MODEL), row(D_MODEL),
                  pl.BlockSpec((1, 8, D_MODEL), lambda i: (i * tm // SEG, 0, 0)),
                  full(8, D_MODEL), full(D_MODEL, D_MODEL), full(D_MODEL, LANES), full(D_MODEL, LANES), full(1, LANES)],
        out_specs=[row(D_MODEL), row(D_MODEL // 2), row(LANES)],
        out_shape=[jax.ShapeDtypeStruct((t, D_MODEL), F32), jax.ShapeDtypeStruct((t, D_MODEL // 2), jnp.uint32),
                   jax.ShapeDtypeStruct((t, LANES), F32)],
        compiler_params=_cparams("parallel"),
        name="outproj_residual",
    )(merged, x, modseg, gains8, w_out_bf16, wr_hi, wr_lo, br)


R_TM = 512


def _route_kernel(lg_ref, ti_ref, tw_ref, rk_ref, cnt_ref, carry_ref):
    @pl.when(pl.program_id(0) == 0)
    def _():
        carry_ref[...] = jnp.zeros(carry_ref.shape, F32)

    lane = _lane_iota((R_TM, LANES))
    l = jnp.where(lane < N_EXPERTS, lg_ref[...], -jnp.inf)
    vals, idxs = [], []
    for _ in range(TOP_K):
        m = jnp.max(l, axis=-1, keepdims=True)
        idx = jnp.min(jnp.where(l == m, lane.astype(F32), float(LANES)), axis=-1, keepdims=True).astype(I32)
        vals.append(m)
        idxs.append(idx)
        l = jnp.where(lane == idx, -jnp.inf, l)
    es = [jnp.exp(v - vals[0]) for v in vals]
    den = es[0] + es[1] + es[2] + es[3]
    ti = jnp.zeros((R_TM, LANES), I32)
    tw = jnp.zeros((R_TM, LANES), F32)
    cnt = jnp.zeros((R_TM, LANES), F32)
    for k in range(TOP_K):
        ti = jnp.where(lane == k, idxs[k], ti)
        tw = jnp.where(lane == k, es[k] / den, tw)
        cnt = cnt + jnp.where(lane == idxs[k], 1.0, 0.0)
    ti_ref[...] = ti
    tw_ref[...] = tw
    r = lax.broadcasted_iota(I32, (R_TM, R_TM), 0)
    c = lax.broadcasted_iota(I32, (R_TM, R_TM), 1)
    tri = jnp.where(c < r, 1.0, 0.0).astype(BF16)
    before = jnp.dot(tri, cnt.astype(BF16), preferred_element_type=F32) + carry_ref[...]
    rk = jnp.zeros((R_TM, LANES), I32)
    for k in range(TOP_K):
        pos = jnp.sum(jnp.where(lane == idxs[k], before, 0.0), axis=-1, keepdims=True)
        rk = jnp.where(lane == k, pos.astype(I32), rk)
    rk_ref[...] = rk
    carry_ref[...] = carry_ref[...] + jnp.sum(cnt, axis=0, keepdims=True)
    cnt_ref[...] = jnp.broadcast_to(carry_ref[...], cnt_ref.shape)


def _route(logits):
    t = logits.shape[0]
    row = pl.BlockSpec((R_TM, LANES), lambda i: (i, 0))
    return pl.pallas_call(
        _route_kernel,
        grid=(t // R_TM,),
        in_specs=[row],
        out_specs=[row, row, row, pl.BlockSpec((8, LANES), lambda i: (0, 0))],
        out_shape=[jax.ShapeDtypeStruct((t, LANES), I32), jax.ShapeDtypeStruct((t, LANES), F32),
                   jax.ShapeDtypeStruct((t, LANES), I32), jax.ShapeDtypeStruct((8, LANES), F32)],
        scratch_shapes=[pltpu.VMEM((1, LANES), F32)],
        compiler_params=_cparams("arbitrary"),
        name="moe_route",
    )(logits)


E_TM = 512
DISP_TT = 512
COMB_TT = 256


ROW_DMA_UNROLL = 8


def _row_dma_burst(copy, n):
    def start(g, c):
        for u in range(ROW_DMA_UNROLL):
            copy(g * ROW_DMA_UNROLL + u).start(priority=u % 2)
        return c

    def wait(g, c):
        for u in range(ROW_DMA_UNROLL):
            copy(g * ROW_DMA_UNROLL + u).wait()
        return c

    lax.fori_loop(0, n // ROW_DMA_UNROLL, start, 0)
    lax.fori_loop(0, n // ROW_DMA_UNROLL, wait, 0)


def _dispatch_kernel(pos_ref, h_ref, xs_in, xs_ref, sem):
    del xs_in

    def copy(a):
        return pltpu.make_async_copy(h_ref.at[pl.ds(a // TOP_K, 1)], xs_ref.at[pl.ds(pos_ref[a], 1)], sem)

    _row_dma_burst(copy, DISP_TT * TOP_K)


def _dispatch(pos_flat, h2p, n_rows):
    t, width = h2p.shape
    xs0 = jnp.zeros((n_rows, width), h2p.dtype)
    return pl.pallas_call(
        _dispatch_kernel,
        grid=(t // DISP_TT,),
        in_specs=[pl.BlockSpec((DISP_TT * TOP_K,), lambda i: (i,), memory_space=pltpu.SMEM),
                  pl.BlockSpec((DISP_TT, width), lambda i: (i, 0)),
                  pl.BlockSpec(memory_space=pl.ANY)],
        out_specs=pl.BlockSpec(memory_space=pl.ANY),
        scratch_shapes=[pltpu.SemaphoreType.DMA(())],
        out_shape=jax.ShapeDtypeStruct((n_rows, width), h2p.dtype),
        input_output_aliases={2: 0},
        compiler_params=_cparams("arbitrary"),
        name="moe_dispatch",
    )(pos_flat, h2p, xs0)


UP_TN = 1024
DN_TN = 1024
SEL_K = 256


def _weights_changed(te_ref, i):
    return (i == 0) | (te_ref[i] != te_ref[jnp.maximum(i - 1, 0)])


def _up_kernel(te_ref, nu_ref, x_ref, w_ref, bg_ref, bl_ref, sel_ref, o_ref, wg_ref, wl_ref):
    i = pl.program_id(1)

    @pl.when(i < nu_ref[0])
    def _():
        @pl.when(_weights_changed(te_ref, i))
        def _():
            for c in range(UP_TN // SEL_K):
                wc = w_ref[0, :, c * SEL_K:(c + 1) * SEL_K].astype(BF16)
                cols = slice(c * (SEL_K // 2), (c + 1) * (SEL_K // 2))
                wg_ref[:, cols] = jnp.dot(wc, sel_ref[0], preferred_element_type=F32).astype(BF16)
                wl_ref[:, cols] = jnp.dot(wc, sel_ref[1], preferred_element_type=F32).astype(BF16)

        xw = x_ref[...]
        x = jnp.concatenate([pltpu.bitcast(lax.shift_left(xw, jnp.uint32(16)), F32).astype(BF16),
                             pltpu.bitcast(xw & jnp.uint32(0xFFFF0000), F32).astype(BF16)], axis=1)
        glu = jnp.dot(x, wg_ref[...], preferred_element_type=F32) + bg_ref[0]
        lin = jnp.dot(x, wl_ref[...], preferred_element_type=F32) + bl_ref[0]
        glu = jnp.minimum(glu, SWIGLU_LIMIT)
        lin = jnp.clip(lin, -SWIGLU_LIMIT, SWIGLU_LIMIT)
        o_ref[...] = (glu * jax.nn.sigmoid(SWIGLU_ALPHA * glu) * (lin + 1.0)).astype(o_ref.dtype)


def _expert_up(tile_e, n_used, xs, w_up, b_up, layer):
    n_rows = xs.shape[0]
    pick = jnp.arange(SEL_K)[:, None] - 2 * jnp.arange(SEL_K // 2)[None, :]
    sel = jnp.stack([pick == 0, pick == 1]).astype(BF16)
    bu = b_up[layer].astype(F32)
    bg, bl = bu[:, None, 0::2], bu[:, None, 1::2]
    half_tn = UP_TN // 2
    bspec = pl.BlockSpec((1, 1, half_tn), lambda j, i, te, nu: (te[i], 0, j))
    return pl.pallas_call(
        _up_kernel,
        grid_spec=pltpu.PrefetchScalarGridSpec(
            num_scalar_prefetch=2, grid=(2 * D_FF // UP_TN, n_rows // E_TM),
            in_specs=[pl.BlockSpec((E_TM, D_MODEL // 2), lambda j, i, te, nu: (i, 0)),
                      pl.BlockSpec((None, 1, D_MODEL, UP_TN), lambda j, i, te, nu: (layer, te[i], 0, j)),
                      bspec, bspec,
                      pl.BlockSpec((2, SEL_K, SEL_K // 2), lambda j, i, te, nu: (0, 0, 0))],
            out_specs=pl.BlockSpec((E_TM, half_tn), lambda j, i, te, nu: (i, j)),
            scratch_shapes=[pltpu.VMEM((D_MODEL, half_tn), BF16), pltpu.VMEM((D_MODEL, half_tn), BF16)]),
        out_shape=jax.ShapeDtypeStruct((n_rows, D_FF), BF16),
        compiler_params=_cparams("arbitrary", "arbitrary"),
        name="moe_up",
    )(tile_e, n_used, xs, w_up, bg, bl, sel)


def _down_kernel(te_ref, nu_ref, a_ref, w_ref, b_ref, o_ref, wbf_ref):
    i = pl.program_id(1)

    @pl.when(i < nu_ref[0])
    def _():
        @pl.when(_weights_changed(te_ref, i))
        def _():
            wbf_ref[...] = w_ref[0].astype(BF16)

        o_ref[...] = jnp.dot(a_ref[...], wbf_ref[...], preferred_element_type=F32) + b_ref[0]


def _expert_down(tile_e, n_used, act, w_down, b_down, layer):
    n_rows = act.shape[0]
    return pl.pallas_call(
        _down_kernel,
        grid_spec=pltpu.PrefetchScalarGridSpec(
            num_scalar_prefetch=2, grid=(D_MODEL // DN_TN, n_rows // E_TM),
            in_specs=[pl.BlockSpec((E_TM, D_FF), lambda j, i, te, nu: (i, 0)),
                      pl.BlockSpec((None, 1, D_FF, DN_TN), lambda j, i, te, nu: (layer, te[i], 0, j)),
                      pl.BlockSpec((None, 1, 1, DN_TN), lambda j, i, te, nu: (layer, te[i], 0, j))],
            out_specs=pl.BlockSpec((E_TM, DN_TN), lambda j, i, te, nu: (i, j)),
            scratch_shapes=[pltpu.VMEM((D_FF, DN_TN), BF16)]),
        out_shape=jax.ShapeDtypeStruct((n_rows, D_MODEL), F32),
        compiler_params=_cparams("arbitrary", "arbitrary"),
        name="moe_down",
    )(tile_e, n_used, act, w_down, b_down.reshape(DEPTH, N_EXPERTS, 1, D_MODEL))


def _combine_kernel(pos_ref, ys_ref, tw_ref, x_ref, mod_ref, g_ref, *rest, split_blk):
    o_refs, (buf, sem) = rest[:-2], rest[-2:]

    def copy(a):
        return pltpu.make_async_copy(ys_ref.at[pl.ds(pos_ref[a], 1)], buf.at[a % TOP_K, pl.ds(a // TOP_K, 1)], sem)

    _row_dma_burst(copy, COMB_TT * TOP_K)
    tw = tw_ref[...]
    f = tw[:, 0:1] * buf[0]
    for k in range(1, TOP_K):
        f = f + tw[:, k:k + 1] * buf[k]
    m = mod_ref[0]
    fn = f * lax.rsqrt(jnp.mean(f * f, axis=-1, keepdims=True) + EPS) * g_ref[3:4]
    out = x_ref[...] + m[5:6] * fn
    if split_blk is None:
        o_refs[0][...] = out
    else:
        @pl.when(pl.program_id(0) < split_blk)
        def _():
            o_refs[0][...] = out

        @pl.when(pl.program_id(0) >= split_blk)
        def _():
            o_refs[1][...] = out


def _combine(pos_flat, ys, tw, x, modseg, gains, split_rows=None):
    t = x.shape[0]
    gains8 = jnp.pad(gains.astype(F32), ((0, 4), (0, 0)))
    if split_rows is None:
        split_blk = None
        out_specs = pl.BlockSpec((COMB_TT, D_MODEL), lambda i: (i, 0))
        out_shape = jax.ShapeDtypeStruct((t, D_MODEL), F32)
    else:
        split_blk = split_rows // COMB_TT
        assert split_rows % COMB_TT == 0 and 0 < split_blk < t // COMB_TT
        out_specs = [pl.BlockSpec((COMB_TT, D_MODEL), lambda i: (jnp.minimum(i, split_blk - 1), 0)),
                     pl.BlockSpec((COMB_TT, D_MODEL), lambda i: (jnp.maximum(i - split_blk, 0), 0))]
        out_shape = [jax.ShapeDtypeStruct((split_rows, D_MODEL), F32),
                     jax.ShapeDtypeStruct((t - split_rows, D_MODEL), F32)]
    return pl.pallas_call(
        functools.partial(_combine_kernel, split_blk=split_blk),
        grid=(t // COMB_TT,),
        in_specs=[pl.BlockSpec((COMB_TT * TOP_K,), lambda i: (i,), memory_space=pltpu.SMEM),
                  pl.BlockSpec(memory_space=pl.ANY),
                  pl.BlockSpec((COMB_TT, LANES), lambda i: (i, 0)),
                  pl.BlockSpec((COMB_TT, D_MODEL), lambda i: (i, 0)),
                  pl.BlockSpec((1, 8, D_MODEL), lambda i: (i * COMB_TT // SEG, 0, 0)),
                  pl.BlockSpec((8, D_MODEL), lambda i: (0, 0))],
        out_specs=out_specs,
        scratch_shapes=[pltpu.VMEM((TOP_K, COMB_TT, D_MODEL), F32), pltpu.SemaphoreType.DMA(())],
        out_shape=out_shape,
        compiler_params=_cparams("arbitrary"),
        name="moe_combine",
    )(pos_flat, ys, tw, x, modseg, gains8)


def _moe(h2p, logits, x, modseg, gains, w_up, b_up, w_down, b_down, layer, split_rows=None):
    t = h2p.shape[0]
    ti, tw, rk, cnt = _route(logits)
    counts = cnt[0, :N_EXPERTS].astype(I32)
    padded = (counts + E_TM - 1) // E_TM * E_TM
    upto = jnp.arange(N_EXPERTS)[None, :] <= jnp.arange(N_EXPERTS)[:, None]
    ends = jnp.sum(jnp.where(upto, padded[None, :], 0), axis=1).astype(I32)
    offsets = ends - padded
    n_tiles = t * TOP_K // E_TM + N_EXPERTS
    starts = jnp.arange(n_tiles, dtype=I32) * E_TM
    tile_e = jnp.minimum(jnp.sum((ends[None, :] <= starts[:, None]).astype(I32), axis=1), N_EXPERTS - 1)
    n_used = (ends[-1:] // E_TM).astype(I32)
    ti4, rk4 = ti[:, :TOP_K], rk[:, :TOP_K]
    first_row = jnp.sum(jnp.where(ti4[:, :, None] == jnp.arange(N_EXPERTS)[None, None, :], offsets[None, None, :], 0), axis=-1)
    pos_flat = (first_row + rk4).astype(I32).reshape(-1)
    xs = _dispatch(pos_flat, h2p, n_tiles * E_TM)
    act = _expert_up(tile_e, n_used, xs, w_up, b_up, layer)
    ys = _expert_down(tile_e, n_used, act, w_down, b_down, layer)
    return _combine(pos_flat, ys, tw, x, modseg, gains, split_rows)


def _forward(x, c8, seg_rows, lay, t5_bias, w_mod, b_mod, norm_gains, w_in, qk_norm_gains, na_rpb,
             ret_decay_logit, w_branch, w_out, w_router, b_router, w_up, b_up, w_down, b_down):
    mod = _modulation(c8, w_mod, b_mod)
    modseg = mod[:, seg_rows, :].reshape(DEPTH, len(seg_rows), 6, D_MODEL)
    modseg = jnp.pad(modseg, ((0, 0), (0, 0), (0, 2), (0, 0)))
    cos_a, sin_a = _axial_tables(lay)
    cos_r, sin_r = _rope_tables(lay)
    dil_bias = [_dil_bias(t5_bias, g, d, w // (2 * d)) for g, (w, d) in enumerate(DIL_CONFIGS)]
    for l in range(DEPTH):
        gains = norm_gains[l]
        proj = _norm_inproj(x, modseg[l], gains[0], w_in[l].astype(BF16))
        a_parts = [_dilated_group(proj, dil_bias[g], lay, g) for g in range(A_GROUPS)]
        o_b = _axial_gqa(proj, _b_prepare(proj, qk_norm_gains[l], cos_a, sin_a), lay)
        o_c = _neighbourhood(proj, na_rpb[l], lay)
        o_d = _retention(proj, ret_decay_logit[l], cos_r, sin_r, lay)
        merged = _branch_merge(proj, a_parts, o_b, o_c, o_d, w_branch[l].astype(BF16))
        x, h2p, logits = _outproj_residual(merged, x, modseg[l], gains, w_out[l].astype(BF16), w_router[l], b_router[l])
        x = _moe(h2p, logits, x, modseg[l], gains, w_up, b_up, w_down, b_down, l,
                 split_rows=lay.t_p if l == DEPTH - 1 else None)
    return x


def kernel(x_prompt, x_sample, c_prompt, c_sample, t5_bias, w_mod, b_mod, norm_gains, w_in, qk_norm_gains,
           na_rpb, ret_decay_logit, w_branch, w_out, w_router, b_router, w_up, b_up, w_down, b_down):
    n_p, s_p, _ = x_prompt.shape
    n_s, s_s, _ = x_sample.shape
    lay = Layout(n_p, s_p, n_s, s_s)
    assert s_p % SEG == 0 and s_s % SEG == 0 and n_p + n_s <= 8
    x = jnp.concatenate([x_prompt.reshape(-1, D_MODEL), x_sample.reshape(-1, D_MODEL)], axis=0)
    c8 = jnp.concatenate([c_prompt, c_sample, jnp.zeros((8 - n_p - n_s, D_MODEL), F32)], axis=0)
    seg_rows = tuple([b for b in range(n_p) for _ in range(s_p // SEG)]
                     + [n_p + b for b in range(n_s) for _ in range(s_s // SEG)])
    y_p, y_s = _forward(x, c8, jnp.asarray(seg_rows, dtype=I32), lay, t5_bias, w_mod, b_mod, norm_gains, w_in,
                        qk_norm_gains, na_rpb, ret_decay_logit, w_branch, w_out, w_router, b_router,
                        w_up, b_up, w_down, b_down)
    return (y_p.reshape(n_p, s_p, D_MODEL), y_s.reshape(n_s, s_s, D_MODEL))
```

```python
import functools
import math
from typing import NamedTuple

import jax
import jax.numpy as jnp
from jax import lax
from jax.experimental import pallas as pl
from jax.experimental.pallas import tpu as pltpu

F32 = jnp.float32
BF16 = jnp.bfloat16
I32 = jnp.int32

D_MODEL = 2048
DEPTH = 2
HEAD_DIM = 128
GRID_W = 64
EPS = 1e-6
ROPE_THETA = 10000.0
NEG_INF = -1e30
LOG2E = 1.4426950408889634

DIL_CONFIGS = ((128, 1), (512, 4), (2048, 16))
A_GROUPS = 3
A_HEADS_PER_GROUP = 6
A_HEADS = A_GROUPS * A_HEADS_PER_GROUP
T5_BUCKETS = 32
T5_MAX_DIST = 1024
B_Q_HEADS = 6
B_KV_HEADS = 2
C_HEADS = 6
NA_ROWS = 8
NA_COLS = 16
D_HEADS = 4
D_QK = 128
D_V = 256
N_EXPERTS = 32
TOP_K = 4
D_FF = 2048
SWIGLU_LIMIT = 7.0
SWIGLU_ALPHA = 1.702
N_BRANCH = 4

A_OUT = A_HEADS_PER_GROUP * HEAD_DIM
B_OUT = B_Q_HEADS * HEAD_DIM
C_OUT = C_HEADS * HEAD_DIM
D_OUT = D_HEADS * D_V
MIX_WIDTH = A_OUT + B_OUT + C_OUT + D_OUT

IN_SPLITS = (A_HEADS * HEAD_DIM, A_HEADS * HEAD_DIM, A_HEADS * HEAD_DIM,
             B_Q_HEADS * HEAD_DIM, B_KV_HEADS * HEAD_DIM, B_KV_HEADS * HEAD_DIM,
             C_HEADS * HEAD_DIM, C_HEADS * HEAD_DIM, C_HEADS * HEAD_DIM,
             D_HEADS * D_QK, D_HEADS * D_QK, D_HEADS * D_V, D_HEADS * D_V,
             N_BRANCH * D_MODEL)
N_IN = sum(IN_SPLITS)
_OFF = [0]
for _w in IN_SPLITS:
    _OFF.append(_OFF[-1] + _w)
(OFF_AQ, OFF_AK, OFF_AV, OFF_BQ, OFF_BK, OFF_BV, OFF_CQ, OFF_CK, OFF_CV,
 OFF_DQ, OFF_DK, OFF_DV, OFF_DG, OFF_GATE) = _OFF[:-1]

SEG = 2048
LANES = 128
VMEM_LIMIT = 56 * 1024 * 1024


class Layout(NamedTuple):
    n_p: int
    s_p: int
    n_s: int
    s_s: int

    @property
    def t_p(self):
        return self.n_p * self.s_p

    @property
    def t(self):
        return self.n_p * self.s_p + self.n_s * self.s_s


def _cparams(*sem):
    return pltpu.CompilerParams(dimension_semantics=sem, vmem_limit_bytes=VMEM_LIMIT)


def _mod_kernel(c_ref, w_ref, b_ref, o_ref):
    c = c_ref[...]
    s = (c * jax.nn.sigmoid(c)).astype(BF16)
    o_ref[0] = jnp.dot(s, w_ref[0].astype(BF16), preferred_element_type=F32) + b_ref[0]


def _modulation(c8, w_mod, b_mod):
    tn = 1024
    n6 = 6 * D_MODEL
    return pl.pallas_call(
        _mod_kernel,
        grid=(DEPTH, n6 // tn),
        in_specs=[pl.BlockSpec((8, D_MODEL), lambda l, j: (0, 0)),
                  pl.BlockSpec((1, D_MODEL, tn), lambda l, j: (l, 0, j)),
                  pl.BlockSpec((1, 1, tn), lambda l, j: (l, 0, j))],
        out_specs=pl.BlockSpec((1, 8, tn), lambda l, j: (l, 0, j)),
        out_shape=jax.ShapeDtypeStruct((DEPTH, 8, n6), F32),
        compiler_params=_cparams("parallel", "parallel"),
        name="adaln_mod",
    )(c8, w_mod, b_mod.reshape(DEPTH, 1, n6))


def _inproj_kernel(x_ref, mod_ref, g_ref, w_ref, o_ref, h_ref):
    @pl.when(pl.program_id(1) == 0)
    def _():
        x = x_ref[...]
        y = x * lax.rsqrt(jnp.mean(x * x, axis=-1, keepdims=True) + EPS) * g_ref[...]
        m = mod_ref[0]
        h_ref[...] = (y * (1.0 + m[1:2]) + m[0:1]).astype(BF16)

    o_ref[...] = jnp.dot(h_ref[...], w_ref[...], preferred_element_type=F32).astype(o_ref.dtype)


def _norm_inproj(x, modseg, gain, w_in_bf16):
    t = x.shape[0]
    tm, tn = 1024, 1280
    assert t % tm == 0 and N_IN % tn == 0 and SEG % tm == 0
    return pl.pallas_call(
        _inproj_kernel,
        grid=(t // tm, N_IN // tn),
        in_specs=[pl.BlockSpec((tm, D_MODEL), lambda i, j: (i, 0)),
                  pl.BlockSpec((1, 8, D_MODEL), lambda i, j: (i * tm // SEG, 0, 0)),
                  pl.BlockSpec((1, D_MODEL), lambda i, j: (0, 0)),
                  pl.BlockSpec((D_MODEL, tn), lambda i, j: (0, j))],
        out_specs=pl.BlockSpec((tm, tn), lambda i, j: (i, j)),
        out_shape=jax.ShapeDtypeStruct((t, N_IN), BF16),
        scratch_shapes=[pltpu.VMEM((tm, D_MODEL), BF16)],
        compiler_params=_cparams("parallel", "arbitrary"),
        name="norm_inproj",
    )(x, modseg, gain.reshape(1, D_MODEL), w_in_bf16)


def _local_pos(lay):
    return jnp.concatenate([jnp.tile(jnp.arange(lay.s_p), lay.n_p), jnp.tile(jnp.arange(lay.s_s), lay.n_s)])


def _axial_tables(lay):
    pos = _local_pos(lay)
    lane = jnp.arange(LANES)
    quarter = HEAD_DIM // 4
    freqs = ROPE_THETA ** (-jnp.arange(quarter, dtype=F32) / quarter)
    f = freqs[lane % quarter]
    p = jnp.where(lane[None, :] < HEAD_DIM // 2, (pos // GRID_W)[:, None], (pos % GRID_W)[:, None]).astype(F32)
    ang = p * f[None, :]
    sign = jnp.where((lane % (2 * quarter)) < quarter, -1.0, 1.0).astype(F32)
    return jnp.cos(ang), jnp.sin(ang) * sign[None, :]


def _rope_tables(lay):
    pos = _local_pos(lay)
    lane = jnp.arange(LANES)
    half = D_QK // 2
    freqs = ROPE_THETA ** (-jnp.arange(half, dtype=F32) / half)
    ang = pos.astype(F32)[:, None] * freqs[lane % half][None, :]
    sign = jnp.where(lane < half, -1.0, 1.0).astype(F32)
    return jnp.cos(ang), jnp.sin(ang) * sign[None, :]


def _lane_iota(shape):
    return lax.broadcasted_iota(I32, shape, len(shape) - 1)


def _axial_rotate(x, cos, sin_signed):
    q = HEAD_DIM // 4
    lo = (_lane_iota(x.shape) % (2 * q)) < q
    partner = jnp.where(lo, pltpu.roll(x, LANES - q, 1), pltpu.roll(x, q, 1))
    return x * cos + partner * sin_signed


def _rope_rotate(x, cos, sin_signed):
    return x * cos + pltpu.roll(x, D_QK // 2, 1) * sin_signed


def _bprep_kernel(x_ref, g_ref, cos_ref, sin_ref, o_ref):
    x = x_ref[...].astype(F32)
    y = x * lax.rsqrt(jnp.mean(x * x, axis=-1, keepdims=True) + EPS) * g_ref[0]
    y = _axial_rotate(y, cos_ref[...], sin_ref[...])
    scale = jnp.where(pl.program_id(1) < B_Q_HEADS, HEAD_DIM ** -0.5 * LOG2E, 1.0)
    o_ref[...] = (y * scale).astype(o_ref.dtype)


def _b_prepare(proj, qk_g, cos, sin):
    t = proj.shape[0]
    tr = 1024
    nh = B_Q_HEADS + B_KV_HEADS
    g8 = jnp.concatenate([jnp.tile(qk_g[0:1], (B_Q_HEADS, 1)), jnp.tile(qk_g[1:2], (B_KV_HEADS, 1))]).reshape(nh, 1, HEAD_DIM)
    return pl.pallas_call(
        _bprep_kernel,
        grid=(t // tr, nh),
        in_specs=[pl.BlockSpec((tr, HEAD_DIM), lambda i, h: (i, OFF_BQ // HEAD_DIM + h)),
                  pl.BlockSpec((1, 1, HEAD_DIM), lambda i, h: (h, 0, 0)),
                  pl.BlockSpec((tr, HEAD_DIM), lambda i, h: (i, 0)),
                  pl.BlockSpec((tr, HEAD_DIM), lambda i, h: (i, 0))],
        out_specs=pl.BlockSpec((tr, HEAD_DIM), lambda i, h: (i, h)),
        out_shape=jax.ShapeDtypeStruct((t, nh * HEAD_DIM), BF16),
        compiler_params=_cparams("parallel", "parallel"),
        name="b_prep",
    )(proj, g8, cos, sin)


FLASH_ROW_SPLIT = 2


def _flash_kernel(qt_ref, kt_ref, first_ref, last_ref, q_ref, k_ref, v_ref, o_ref, m_ref, acc_ref):
    s_id = pl.program_id(1)
    rep = B_Q_HEADS // B_KV_HEADS

    @pl.when(first_ref[s_id] == 1)
    def _():
        m_ref[...] = jnp.full(m_ref.shape, -jnp.inf, F32)
        acc_ref[...] = jnp.zeros(acc_ref.shape, F32)

    k = k_ref[...]
    v = v_ref[...]
    rb = q_ref.shape[0] // FLASH_ROW_SPLIT
    units = [(r, slice(u * rb, (u + 1) * rb)) for r in range(rep) for u in range(FLASH_ROW_SPLIT)]
    scores = [lax.dot_general(q_ref[rows, r * HEAD_DIM:(r + 1) * HEAD_DIM], k, (((1,), (1,)), ((), ())),
                              preferred_element_type=F32) for (r, rows) in units]
    probs, alphas = [], []
    for (r, rows), s in zip(units, scores):
        m_prev = m_ref[r, rows, :]
        m_cur = jnp.maximum(m_prev, jnp.max(s, axis=-1, keepdims=True))
        alphas.append(jnp.exp2(m_prev - m_cur))
        probs.append(jnp.exp2(s - m_cur[:, 0:1]).astype(BF16))
        m_ref[r, rows, :] = m_cur
    for (r, rows), p, alpha in zip(units, probs, alphas):
        pv = jnp.dot(p, v, preferred_element_type=F32)
        acc_ref[r, rows, :] = jnp.concatenate([alpha, alpha], axis=1) * acc_ref[r, rows, :] + pv

    @pl.when(last_ref[s_id] == 1)
    def _():
        for r in range(rep):
            acc = acc_ref[r]
            o_ref[:, r * HEAD_DIM:(r + 1) * HEAD_DIM] = (acc[:, :HEAD_DIM] / acc[:, HEAD_DIM:]).astype(o_ref.dtype)


def _flash_tables(lay, tq, tk):
    qt, kt, first, last = [], [], [], []
    for (n, s, base) in ((lay.n_p, lay.s_p, 0), (lay.n_s, lay.s_s, lay.t_p)):
        for b in range(n):
            for qi in range(s // tq):
                nk = s // tk
                for ki in range(nk):
                    qt.append((base + b * s) // tq + qi)
                    kt.append((base + b * s) // tk + ki)
                    first.append(int(ki == 0))
                    last.append(int(ki == nk - 1))
    mk = lambda a: jnp.asarray(a, dtype=I32)
    return mk(qt), mk(kt), mk(first), mk(last)


def _axial_gqa(proj, bprep, lay):
    t = proj.shape[0]
    tq, tk = 512, 2048
    assert lay.s_p % tk == 0 and lay.s_s % tk == 0
    rep = B_Q_HEADS // B_KV_HEADS
    qt, kt, first, last = _flash_tables(lay, tq, tk)
    n_steps = qt.shape[0]
    v = proj[:, OFF_BV:OFF_BV + B_KV_HEADS * HEAD_DIM].reshape(t, B_KV_HEADS, HEAD_DIM)
    v_ones = jnp.concatenate([v, jnp.ones_like(v)], axis=-1).reshape(t, B_KV_HEADS * 2 * HEAD_DIM)
    gs = pltpu.PrefetchScalarGridSpec(
        num_scalar_prefetch=4,
        grid=(B_KV_HEADS, n_steps),
        in_specs=[pl.BlockSpec((tq, rep * HEAD_DIM), lambda g, s, qt, kt, f, l: (qt[s], g)),
                  pl.BlockSpec((tk, HEAD_DIM), lambda g, s, qt, kt, f, l: (kt[s], B_Q_HEADS + g)),
                  pl.BlockSpec((tk, 2 * HEAD_DIM), lambda g, s, qt, kt, f, l: (kt[s], g))],
        out_specs=pl.BlockSpec((tq, rep * HEAD_DIM), lambda g, s, qt, kt, f, l: (qt[s], g)),
        scratch_shapes=[pltpu.VMEM((rep, tq, LANES), F32), pltpu.VMEM((rep, tq, 2 * HEAD_DIM), F32)],
    )
    return pl.pallas_call(
        _flash_kernel,
        grid_spec=gs,
        out_shape=jax.ShapeDtypeStruct((t, B_OUT), BF16),
        compiler_params=_cparams("parallel", "arbitrary"),
        name="b_flash",
    )(qt, kt, first, last, bprep, bprep, v_ones)


A_BQ = 128
A_NSUB = 2


def _t5_bucket(rel):
    nb = T5_BUCKETS // 2
    max_exact = nb // 2
    n = jnp.abs(rel)
    large = max_exact + (jnp.log(jnp.maximum(n, 1).astype(F32) / max_exact)
                         / math.log(T5_MAX_DIST / max_exact) * (nb - max_exact)).astype(I32)
    large = jnp.minimum(large, nb - 1)
    return jnp.where(rel > 0, nb, 0) + jnp.where(n < max_exact, n, large)


def _dil_bias(t5_bias, g, d, half):
    rel = (jnp.arange(3 * A_BQ)[None, :] - A_BQ) - jnp.arange(A_BQ)[:, None]
    tab = t5_bias[:, g * A_HEADS_PER_GROUP:(g + 1) * A_HEADS_PER_GROUP].astype(F32)
    onehot = (_t5_bucket(rel * d)[:, :, None] == jnp.arange(T5_BUCKETS)[None, None, :]).astype(F32)
    bias = jnp.einsum("qkb,bh->hqk", onehot, tab, precision=lax.Precision.HIGHEST)
    return jnp.where((jnp.abs(rel) <= half)[None], bias, NEG_INF)


def _dil_kernel(q_ref, kp_ref, kc_ref, kn_ref, vp_ref, vc_ref, vn_ref, b_ref, o_ref, lse_ref, *, nblk_p, tblk_p, nblk_s):
    scale = HEAD_DIM ** -0.5
    dn = (((1,), (1,)), ((), ()))
    units = []
    for s in range(A_NSUB):
        r = pl.program_id(1) * A_NSUB + s
        in_p = r < tblk_p
        nblk = jnp.where(in_p, nblk_p, nblk_s)
        il = jnp.where(in_p, r, r - tblk_p) % nblk
        prev_ok = il > 0
        next_ok = il < nblk - 1
        rows = slice(s * A_BQ, (s + 1) * A_BQ)
        before = (kp_ref, vp_ref, slice((A_NSUB - 1) * A_BQ, A_NSUB * A_BQ)) if s == 0 else \
            (kc_ref, vc_ref, slice((s - 1) * A_BQ, s * A_BQ))
        after = (kn_ref, vn_ref, slice(0, A_BQ)) if s == A_NSUB - 1 else \
            (kc_ref, vc_ref, slice((s + 1) * A_BQ, (s + 2) * A_BQ))
        for h in range(A_HEADS_PER_GROUP):
            sl = slice(h * HEAD_DIM, (h + 1) * HEAD_DIM)
            q = q_ref[rows, sl]
            b = b_ref[h]
            sp = lax.dot_general(q, before[0][before[2], sl], dn, preferred_element_type=F32) * scale + b[:, 0:A_BQ]
            sc = lax.dot_general(q, kc_ref[rows, sl], dn, preferred_element_type=F32) * scale + b[:, A_BQ:2 * A_BQ]
            sn = lax.dot_general(q, after[0][after[2], sl], dn, preferred_element_type=F32) * scale + b[:, 2 * A_BQ:]
            sp = jnp.where(prev_ok, sp, NEG_INF)
            sn = jnp.where(next_ok, sn, NEG_INF)
            units.append((rows, sl, before, after, sp, sc, sn))
    soft = []
    for (_, _, _, _, sp, sc, sn) in units:
        m = jnp.maximum(jnp.maximum(jnp.max(sp, axis=-1, keepdims=True), jnp.max(sc, axis=-1, keepdims=True)),
                        jnp.max(sn, axis=-1, keepdims=True))
        pp, pc, pn = jnp.exp(sp - m), jnp.exp(sc - m), jnp.exp(sn - m)
        l = (jnp.sum(pp, axis=-1, keepdims=True) + jnp.sum(pc, axis=-1, keepdims=True)
             + jnp.sum(pn, axis=-1, keepdims=True))
        soft.append((m, l, pp, pc, pn))
    for (rows, sl, before, after, _, _, _), (m, l, pp, pc, pn) in zip(units, soft):
        o = (jnp.dot(pp.astype(BF16), before[1][before[2], sl], preferred_element_type=F32)
             + jnp.dot(pc.astype(BF16), vc_ref[rows, sl], preferred_element_type=F32)
             + jnp.dot(pn.astype(BF16), after[1][after[2], sl], preferred_element_type=F32))
        o_ref[rows, sl] = (o / l).astype(o_ref.dtype)
        lse_ref[rows, sl] = jnp.broadcast_to(m + jnp.log(l), (A_BQ, HEAD_DIM))


def _dilated_group(proj, bias, lay, g):
    w, d = DIL_CONFIGS[g]
    t = proj.shape[0]
    rows = t // d
    nblk_p = lay.s_p // d // A_BQ
    nblk_s = lay.s_s // d // A_BQ
    assert nblk_p >= 1 and nblk_s >= 1 and w // (2 * d) <= A_BQ
    tblk_p = lay.t_p // d // A_BQ
    br = A_BQ * A_NSUB
    assert rows % br == 0
    tblk = rows // br
    gw = A_HEADS_PER_GROUP * HEAD_DIM

    def spec(shift):
        return pl.BlockSpec((br, gw), lambda c, i: (jnp.clip(i + shift, 0, tblk - 1), c))

    if d == 1:
        def win(off, shift):
            return pl.BlockSpec(
                (pl.Element(br), pl.Element(gw)),
                lambda c, i: (pl.multiple_of(jnp.clip(i + shift, 0, tblk - 1) * br, br), off + g * gw))
        q_c = k_c = v_c = proj
        qkv_specs = [win(OFF_AQ, 0), win(OFF_AK, -1), win(OFF_AK, 0), win(OFF_AK, 1),
                     win(OFF_AV, -1), win(OFF_AV, 0), win(OFF_AV, 1)]
    else:
        q_c, k_c, v_c = [proj[:, off + g * gw:off + (g + 1) * gw].reshape(rows, d * gw)
                         for off in (OFF_AQ, OFF_AK, OFF_AV)]
        qkv_specs = [spec(0), spec(-1), spec(0), spec(1), spec(-1), spec(0), spec(1)]

    kern = functools.partial(_dil_kernel, nblk_p=nblk_p, tblk_p=tblk_p, nblk_s=nblk_s)
    o, lse = pl.pallas_call(
        kern,
        grid=(d, tblk),
        in_specs=qkv_specs + [pl.BlockSpec((A_HEADS_PER_GROUP, A_BQ, 3 * A_BQ), lambda c, i: (0, 0, 0))],
        out_specs=[spec(0), spec(0)],
        out_shape=[jax.ShapeDtypeStruct((rows, d * gw), BF16), jax.ShapeDtypeStruct((rows, d * gw), F32)],
        compiler_params=_cparams("parallel", "parallel"),
        name=f"a_dilated_g{g}",
    )(q_c, k_c, k_c, k_c, v_c, v_c, v_c, bias)
    return o.reshape(t, gw), lse.reshape(t, gw)


C_QROWS = 8
C_KROWS = 2 * NA_ROWS
C_TQ = C_QROWS * GRID_W
C_TK = C_KROWS * GRID_W


def _na_bias(rpb):
    hi = lax.Precision.HIGHEST
    qc = jnp.arange(GRID_W)[:, None]
    kc = jnp.arange(GRID_W)[None, :]
    cstart = jnp.clip(qc - NA_COLS // 2, 0, GRID_W - NA_COLS)
    col_ok = (kc >= cstart) & (kc < cstart + NA_COLS)
    ci = jnp.clip(kc - qc, -(NA_COLS - 1), NA_COLS - 1) + NA_COLS - 1
    oh_c = (ci[:, :, None] == jnp.arange(2 * NA_COLS - 1)[None, None, :]).astype(F32)
    by_col = jnp.einsum("hrc,abc->hrab", rpb.astype(F32), oh_c, precision=hi)
    out = []
    for off in (0, NA_ROWS // 2, NA_ROWS):
        qr = (off + jnp.arange(C_QROWS))[:, None]
        kr = jnp.arange(C_KROWS)[None, :]
        rstart = jnp.clip(qr - NA_ROWS // 2, 0, C_KROWS - NA_ROWS)
        row_ok = (kr >= rstart) & (kr < rstart + NA_ROWS)
        ri = jnp.clip(kr - qr + NA_ROWS - 1, 0, 2 * NA_ROWS - 2)
        oh_r = (ri[:, :, None] == jnp.arange(2 * NA_ROWS - 1)[None, None, :]).astype(F32)
        b = jnp.einsum("qkr,hrab->hqakb", oh_r, by_col, precision=hi)
        ok = row_ok[:, None, :, None] & col_ok[None, :, None, :]
        out.append(jnp.where(ok[None], b, NEG_INF).reshape(C_HEADS, C_TQ, C_TK))
    return jnp.stack(out)


C_ROW_SPLIT = 2


def _na_kernel(q_ref, k_ref, v_ref, b_ref, o_ref):
    k = k_ref[...]
    v = v_ref[...]
    rb = C_TQ // C_ROW_SPLIT
    units = [slice(u * rb, (u + 1) * rb) for u in range(C_ROW_SPLIT)]
    scores = [lax.dot_general(q_ref[rows, :], k, (((1,), (1,)), ((), ())), preferred_element_type=F32)
              * (HEAD_DIM ** -0.5) + b_ref[rows, :] for rows in units]
    soft = []
    for s in scores:
        m = jnp.max(s, axis=-1, keepdims=True)
        p = jnp.exp(s - m)
        soft.append((p.astype(BF16), jnp.sum(p, axis=-1, keepdims=True)))
    for rows, (p, l) in zip(units, soft):
        o_ref[rows, :] = (jnp.dot(p, v, preferred_element_type=F32) / l).astype(o_ref.dtype)


def _neighbourhood(proj, rpb, lay):
    t = proj.shape[0]
    r_p, r_s = lay.s_p // GRID_W, lay.s_s // GRID_W
    assert r_p >= C_KROWS and r_s >= C_KROWS and r_p % C_QROWS == 0 and r_s % C_QROWS == 0
    blk_p = lay.t_p // C_TQ
    bias = _na_bias(rpb)

    def window(i):
        in_p = i < blk_p
        per_seq = jnp.where(in_p, lay.s_p // C_TQ, lay.s_s // C_TQ)
        rows = jnp.where(in_p, r_p, r_s)
        il = jnp.where(in_p, i, i - blk_p)
        seq0 = (i - il % per_seq) * C_TQ
        r0 = (il % per_seq) * C_QROWS
        w0 = jnp.clip(r0 - NA_ROWS // 2, 0, rows - C_KROWS)
        return seq0 + w0 * GRID_W, (r0 - w0) // (NA_ROWS // 2)

    def kv_spec(off):
        return pl.BlockSpec((pl.Element(C_TK), pl.Element(HEAD_DIM)),
                            lambda h, i: (pl.multiple_of(window(i)[0], GRID_W),
                                          pl.multiple_of(off + h * HEAD_DIM, LANES)))

    return pl.pallas_call(
        _na_kernel,
        grid=(C_HEADS, t // C_TQ),
        in_specs=[pl.BlockSpec((C_TQ, HEAD_DIM), lambda h, i: (i, OFF_CQ // HEAD_DIM + h)),
                  kv_spec(OFF_CK), kv_spec(OFF_CV),
                  pl.BlockSpec((None, None, C_TQ, C_TK), lambda h, i: (window(i)[1], h, 0, 0))],
        out_specs=pl.BlockSpec((C_TQ, HEAD_DIM), lambda h, i: (i, h)),
        out_shape=jax.ShapeDtypeStruct((t, C_OUT), BF16),
        compiler_params=_cparams("parallel", "parallel"),
        name="c_neighbourhood",
    )(proj, proj, proj, bias)


RET_C = 256


def _log_sigmoid(x):
    return jnp.minimum(x, 0.0) - jnp.log(1.0 + jnp.exp(-jnp.abs(x)))


def _ret_qk(q_ref, k_ref, cos_ref, sin_ref, h):
    sl = slice(h * D_QK, (h + 1) * D_QK)
    cos, sin = cos_ref[...], sin_ref[...]
    q = _rope_rotate(q_ref[:, sl].astype(F32), cos, sin)
    k = _rope_rotate(k_ref[:, sl].astype(F32), cos, sin) * (D_QK ** -0.5)
    return q, k


def _ret_bwd_kernel(cb_ref, first_ref, q_ref, k_ref, v_ref, cos_ref, sin_ref, dl_ref, o_ref, st_ref):
    s_id = pl.program_id(0)

    @pl.when(first_ref[s_id] == 1)
    def _():
        st_ref[...] = jnp.zeros(st_ref.shape, F32)

    row = lax.broadcasted_iota(I32, (RET_C, 1), 0).astype(F32)
    for h in range(D_HEADS):
        lg = _log_sigmoid(dl_ref[D_HEADS + h:D_HEADS + h + 1, :])
        q, k = _ret_qk(q_ref, k_ref, cos_ref, sin_ref, h)
        v = v_ref[:, h * D_V:(h + 1) * D_V]
        st = st_ref[h]
        q_dec = (q * jnp.exp((RET_C - row) * lg)).astype(BF16)
        o_ref[:, h * D_V:(h + 1) * D_V] = jnp.dot(q_dec, st.astype(BF16), preferred_element_type=F32)
        k_dec = (k * jnp.exp(row * lg)).astype(BF16)
        kv = lax.dot_general(k_dec, v, (((0,), (0,)), ((), ())), preferred_element_type=F32)
        st_ref[h] = st * jnp.exp(RET_C * lg[:, 0:1]) + kv


def _ret_fwd_kernel(cb_ref, first_ref, q_ref, k_ref, v_ref, g_ref, xb_ref, cos_ref, sin_ref, dl_ref, o_ref, st_ref):
    s_id = pl.program_id(0)

    @pl.when(first_ref[s_id] == 1)
    def _():
        st_ref[...] = jnp.zeros(st_ref.shape, F32)

    row = lax.broadcasted_iota(I32, (RET_C, 1), 0).astype(F32)
    diff = (lax.broadcasted_iota(I32, (RET_C, RET_C), 0) - lax.broadcasted_iota(I32, (RET_C, RET_C), 1)).astype(F32)
    for h in range(D_HEADS):
        lgf = _log_sigmoid(dl_ref[h:h + 1, :])
        lgb = _log_sigmoid(dl_ref[D_HEADS + h:D_HEADS + h + 1, :])
        q, k = _ret_qk(q_ref, k_ref, cos_ref, sin_ref, h)
        v = v_ref[:, h * D_V:(h + 1) * D_V]
        st = st_ref[h]
        dmat = jnp.where(diff >= 0, jnp.exp(jnp.maximum(diff, 0.0) * lgf[:, 0:1]),
                         jnp.exp(jnp.maximum(-diff, 0.0) * lgb[:, 0:1]))
        s = lax.dot_general(q.astype(BF16), k.astype(BF16), (((1,), (1,)), ((), ())), preferred_element_type=F32)
        o = jnp.dot((s * dmat).astype(BF16), v, preferred_element_type=F32)
        q_dec = (q * jnp.exp((row + 1.0) * lgf)).astype(BF16)
        o = o + jnp.dot(q_dec, st.astype(BF16), preferred_element_type=F32)
        o = o + xb_ref[:, h * D_V:(h + 1) * D_V]
        k_dec = (k * jnp.exp((RET_C - 1.0 - row) * lgf)).astype(BF16)
        kv = lax.dot_general(k_dec, v, (((0,), (0,)), ((), ())), preferred_element_type=F32)
        st_ref[h] = st * jnp.exp(RET_C * lgf[:, 0:1]) + kv
        mu = jnp.mean(o, axis=-1, keepdims=True)
        var = jnp.mean(jnp.square(o - mu), axis=-1, keepdims=True)
        on = (o - mu) * lax.rsqrt(var + EPS)
        g = g_ref[:, h * D_V:(h + 1) * D_V].astype(F32)
        o_ref[:, h * D_V:(h + 1) * D_V] = (g * jax.nn.sigmoid(g) * on).astype(o_ref.dtype)


def _ret_tables(lay, reverse):
    cb, first = [], []
    for (n, s, base) in ((lay.n_p, lay.s_p, 0), (lay.n_s, lay.s_s, lay.t_p)):
        for b in range(n):
            nc = s // RET_C
            order = range(nc - 1, -1, -1) if reverse else range(nc)
            for j, c in enumerate(order):
                cb.append((base + b * s) // RET_C + c)
                first.append(int(j == 0))
    return jnp.asarray(cb, dtype=I32), jnp.asarray(first, dtype=I32)


def _retention(proj, dlogit, cos, sin, lay):
    t = proj.shape[0]
    dl = jnp.broadcast_to(dlogit.astype(F32).reshape(2 * D_HEADS, 1), (2 * D_HEADS, LANES))
    qw, vw = D_HEADS * D_QK, D_HEADS * D_V
    row_spec = lambda width, off: pl.BlockSpec((pl.Element(RET_C), pl.Element(width)),
                                               lambda s, cb, f: (pl.multiple_of(cb[s] * RET_C, RET_C), off))
    tab_spec = pl.BlockSpec((RET_C, LANES), lambda s, cb, f: (cb[s], 0))
    dl_spec = pl.BlockSpec((2 * D_HEADS, LANES), lambda s, cb, f: (0, 0))
    out_spec = pl.BlockSpec((RET_C, vw), lambda s, cb, f: (cb[s], 0))
    state = pltpu.VMEM((D_HEADS, D_QK, D_V), F32)

    cb, first = _ret_tables(lay, True)
    xb = pl.pallas_call(
        _ret_bwd_kernel,
        grid_spec=pltpu.PrefetchScalarGridSpec(
            num_scalar_prefetch=2, grid=(cb.shape[0],),
            in_specs=[row_spec(qw, OFF_DQ), row_spec(qw, OFF_DK), row_spec(vw, OFF_DV), tab_spec, tab_spec, dl_spec],
            out_specs=out_spec, scratch_shapes=[state]),
        out_shape=jax.ShapeDtypeStruct((t, vw), F32),
        compiler_params=_cparams("arbitrary"),
        name="d_retention_bwd",
    )(cb, first, proj, proj, proj, cos, sin, dl)

    cb, first = _ret_tables(lay, False)
    return pl.pallas_call(
        _ret_fwd_kernel,
        grid_spec=pltpu.PrefetchScalarGridSpec(
            num_scalar_prefetch=2, grid=(cb.shape[0],),
            in_specs=[row_spec(qw, OFF_DQ), row_spec(qw, OFF_DK), row_spec(vw, OFF_DV), row_spec(vw, OFF_DG),
                      out_spec, tab_spec, tab_spec, dl_spec],
            out_specs=out_spec, scratch_shapes=[state]),
        out_shape=jax.ShapeDtypeStruct((t, vw), BF16),
        compiler_params=_cparams("arbitrary"),
        name="d_retention_fwd",
    )(cb, first, proj, proj, proj, proj, xb, cos, sin, dl)


def _merge_kernel(oa0, oa1, oa2, ls0, ls1, ls2, ob, oc, od, g0, g1, g2, g3, wa, wb, wc, wd, o_ref, oa_ref):
    @pl.when(pl.program_id(1) == 0)
    def _():
        l0, l1, l2 = ls0[...], ls1[...], ls2[...]
        m = jnp.maximum(jnp.maximum(l0, l1), l2)
        e0, e1, e2 = jnp.exp(l0 - m), jnp.exp(l1 - m), jnp.exp(l2 - m)
        num = e0 * oa0[...].astype(F32) + e1 * oa1[...].astype(F32) + e2 * oa2[...].astype(F32)
        oa_ref[...] = (num / (e0 + e1 + e2)).astype(BF16)

    def term(gate, o, w):
        return jax.nn.sigmoid(gate[...].astype(F32)) * jnp.dot(o, w[...], preferred_element_type=F32)

    acc = term(g0, oa_ref[...], wa) + term(g1, ob[...], wb) + term(g2, oc[...], wc) + term(g3, od[...], wd)
    o_ref[...] = acc.astype(o_ref.dtype)


def _branch_merge(proj, a_parts, o_b, o_c, o_d, w_branch_bf16):
    t = proj.shape[0]
    tm, tn = 512, 512
    (oa0, ls0), (oa1, ls1), (oa2, ls2) = a_parts
    row = lambda width: pl.BlockSpec((tm, width), lambda i, j: (i, 0))
    gate = lambda b: pl.BlockSpec((pl.Element(tm), pl.Element(tn)),
                                  lambda i, j: (pl.multiple_of(i * tm, tm),
                                                pl.multiple_of(OFF_GATE + b * D_MODEL + j * tn, LANES)))
    wspec = lambda width: pl.BlockSpec((width, tn), lambda i, j: (0, j))
    offs = (0, A_OUT, A_OUT + B_OUT, A_OUT + B_OUT + C_OUT, MIX_WIDTH)
    ws = [w_branch_bf16[offs[b]:offs[b + 1]] for b in range(N_BRANCH)]
    return pl.pallas_call(
        _merge_kernel,
        grid=(t // tm, D_MODEL // tn),
        in_specs=[row(A_OUT)] * 6 + [row(B_OUT), row(C_OUT), row(D_OUT)]
                 + [gate(0), gate(1), gate(2), gate(3)]
                 + [wspec(A_OUT), wspec(B_OUT), wspec(C_OUT), wspec(D_OUT)],
        out_specs=pl.BlockSpec((tm, tn), lambda i, j: (i, j)),
        out_shape=jax.ShapeDtypeStruct((t, D_MODEL), BF16),
        scratch_shapes=[pltpu.VMEM((tm, A_OUT), BF16)],
        compiler_params=_cparams("parallel", "arbitrary"),
        name="branch_merge",
    )(oa0, oa1, oa2, ls0, ls1, ls2, o_b, o_c, o_d, proj, proj, proj, proj, *ws)


def _split3(x):
    hi = x.astype(BF16)
    lo = (x - hi.astype(F32)).astype(BF16)
    return hi, lo


def _outproj_kernel(mg_ref, x_ref, mod_ref, g_ref, w_ref, wr_hi, wr_lo, br_ref, xo_ref, h2_ref, lg_ref):
    y = jnp.dot(mg_ref[...], w_ref[...], preferred_element_type=F32)
    m = mod_ref[0]
    g = g_ref[...]
    yn = y * lax.rsqrt(jnp.mean(y * y, axis=-1, keepdims=True) + EPS) * g[1:2]
    x = x_ref[...] + m[2:3] * yn
    xo_ref[...] = x
    h2 = x * lax.rsqrt(jnp.mean(x * x, axis=-1, keepdims=True) + EPS) * g[2:3] * (1.0 + m[4:5]) + m[3:4]
    h2b = h2.astype(BF16)
    half = D_MODEL // 2
    lo = lax.shift_right_logical(pltpu.bitcast(h2b[:, :half].astype(F32), jnp.uint32), jnp.uint32(16))
    hi = pltpu.bitcast(h2b[:, half:].astype(F32), jnp.uint32)
    h2_ref[...] = hi | lo
    lg_ref[...] = (jnp.dot(h2b, wr_hi[...], preferred_element_type=F32)
                   + jnp.dot(h2b, wr_lo[...], preferred_element_type=F32) + br_ref[...])


def _outproj_residual(merged, x, modseg, gains, w_out_bf16, w_router, b_router):
    t = x.shape[0]
    tm = 256
    wr = jnp.pad(w_router.astype(F32), ((0, 0), (0, LANES - N_EXPERTS)))
    wr_hi, wr_lo = _split3(wr)
    br = jnp.pad(b_router.astype(F32), (0, LANES - N_EXPERTS)).reshape(1, LANES)
    gains8 = jnp.pad(gains.astype(F32), ((0, 4), (0, 0)))
    row = lambda width: pl.BlockSpec((tm, width), lambda i: (i, 0))
    full = lambda a, b: pl.BlockSpec((a, b), lambda i: (0, 0))
    return pl.pallas_call(
        _outproj_kernel,
        grid=(t // tm,),
        in_specs=[row(D_MODEL), row(D_MODEL),
                  pl.BlockSpec((1, 8, D_MODEL), lambda i: (i * tm // SEG, 0, 0)),
                  full(8, D_MODEL), full(D_MODEL, D_MODEL), full(D_MODEL, LANES), full(D_MODEL, LANES), full(1, LANES)],
        out_specs=[row(D_MODEL), row(D_MODEL // 2), row(LANES)],
        out_shape=[jax.ShapeDtypeStruct((t, D_MODEL), F32), jax.ShapeDtypeStruct((t, D_MODEL // 2), jnp.uint32),
                   jax.ShapeDtypeStruct((t, LANES), F32)],
        compiler_params=_cparams("parallel"),
        name="outproj_residual",
    )(merged, x, modseg, gains8, w_out_bf16, wr_hi, wr_lo, br)


R_TM = 512


def _route_kernel(lg_ref, ti_ref, tw_ref, rk_ref, cnt_ref, carry_ref):
    @pl.when(pl.program_id(0) == 0)
    def _():
        carry_ref[...] = jnp.zeros(carry_ref.shape, F32)

    lane = _lane_iota((R_TM, LANES))
    l = jnp.where(lane < N_EXPERTS, lg_ref[...], -jnp.inf)
    vals, idxs = [], []
    for _ in range(TOP_K):
        m = jnp.max(l, axis=-1, keepdims=True)
        idx = jnp.min(jnp.where(l == m, lane.astype(F32), float(LANES)), axis=-1, keepdims=True).astype(I32)
        vals.append(m)
        idxs.append(idx)
        l = jnp.where(lane == idx, -jnp.inf, l)
    es = [jnp.exp(v - vals[0]) for v in vals]
    den = es[0] + es[1] + es[2] + es[3]
    ti = jnp.zeros((R_TM, LANES), I32)
    tw = jnp.zeros((R_TM, LANES), F32)
    cnt = jnp.zeros((R_TM, LANES), F32)
    for k in range(TOP_K):
        ti = jnp.where(lane == k, idxs[k], ti)
        tw = jnp.where(lane == k, es[k] / den, tw)
        cnt = cnt + jnp.where(lane == idxs[k], 1.0, 0.0)
    ti_ref[...] = ti
    tw_ref[...] = tw
    r = lax.broadcasted_iota(I32, (R_TM, R_TM), 0)
    c = lax.broadcasted_iota(I32, (R_TM, R_TM), 1)
    tri = jnp.where(c < r, 1.0, 0.0).astype(BF16)
    before = jnp.dot(tri, cnt.astype(BF16), preferred_element_type=F32) + carry_ref[...]
    rk = jnp.zeros((R_TM, LANES), I32)
    for k in range(TOP_K):
        pos = jnp.sum(jnp.where(lane == idxs[k], before, 0.0), axis=-1, keepdims=True)
        rk = jnp.where(lane == k, pos.astype(I32), rk)
    rk_ref[...] = rk
    carry_ref[...] = carry_ref[...] + jnp.sum(cnt, axis=0, keepdims=True)
    cnt_ref[...] = jnp.broadcast_to(carry_ref[...], cnt_ref.shape)


def _route(logits):
    t = logits.shape[0]
    row = pl.BlockSpec((R_TM, LANES), lambda i: (i, 0))
    return pl.pallas_call(
        _route_kernel,
        grid=(t // R_TM,),
        in_specs=[row],
        out_specs=[row, row, row, pl.BlockSpec((8, LANES), lambda i: (0, 0))],
        out_shape=[jax.ShapeDtypeStruct((t, LANES), I32), jax.ShapeDtypeStruct((t, LANES), F32),
                   jax.ShapeDtypeStruct((t, LANES), I32), jax.ShapeDtypeStruct((8, LANES), F32)],
        scratch_shapes=[pltpu.VMEM((1, LANES), F32)],
        compiler_params=_cparams("arbitrary"),
        name="moe_route",
    )(logits)


E_TM = 512
DISP_TT = 512
COMB_TT = 256


ROW_DMA_UNROLL = 8


def _row_dma_burst(copy, n):
    def start(g, c):
        for u in range(ROW_DMA_UNROLL):
            copy(g * ROW_DMA_UNROLL + u).start(priority=u % 2)
        return c

    def wait(g, c):
        for u in range(ROW_DMA_UNROLL):
            copy(g * ROW_DMA_UNROLL + u).wait()
        return c

    lax.fori_loop(0, n // ROW_DMA_UNROLL, start, 0)
    lax.fori_loop(0, n // ROW_DMA_UNROLL, wait, 0)


def _dispatch_kernel(pos_ref, h_ref, xs_in, xs_ref, sem):
    del xs_in

    def copy(a):
        return pltpu.make_async_copy(h_ref.at[pl.ds(a // TOP_K, 1)], xs_ref.at[pl.ds(pos_ref[a], 1)], sem)

    _row_dma_burst(copy, DISP_TT * TOP_K)


def _dispatch(pos_flat, h2p, n_rows):
    t, width = h2p.shape
    xs0 = jnp.zeros((n_rows, width), h2p.dtype)
    return pl.pallas_call(
        _dispatch_kernel,
        grid=(t // DISP_TT,),
        in_specs=[pl.BlockSpec((DISP_TT * TOP_K,), lambda i: (i,), memory_space=pltpu.SMEM),
                  pl.BlockSpec((DISP_TT, width), lambda i: (i, 0)),
                  pl.BlockSpec(memory_space=pl.ANY)],
        out_specs=pl.BlockSpec(memory_space=pl.ANY),
        scratch_shapes=[pltpu.SemaphoreType.DMA(())],
        out_shape=jax.ShapeDtypeStruct((n_rows, width), h2p.dtype),
        input_output_aliases={2: 0},
        compiler_params=_cparams("arbitrary"),
        name="moe_dispatch",
    )(pos_flat, h2p, xs0)


UP_TN = 1024
DN_TN = 1024
SEL_K = 256


def _weights_changed(te_ref, i):
    return (i == 0) | (te_ref[i] != te_ref[jnp.maximum(i - 1, 0)])


def _up_kernel(te_ref, nu_ref, x_ref, w_ref, bg_ref, bl_ref, sel_ref, o_ref, wg_ref, wl_ref):
    i = pl.program_id(1)

    @pl.when(i < nu_ref[0])
    def _():
        @pl.when(_weights_changed(te_ref, i))
        def _():
            for c in range(UP_TN // SEL_K):
                wc = w_ref[0, :, c * SEL_K:(c + 1) * SEL_K].astype(BF16)
                cols = slice(c * (SEL_K // 2), (c + 1) * (SEL_K // 2))
                wg_ref[:, cols] = jnp.dot(wc, sel_ref[0], preferred_element_type=F32).astype(BF16)
                wl_ref[:, cols] = jnp.dot(wc, sel_ref[1], preferred_element_type=F32).astype(BF16)

        xw = x_ref[...]
        x = jnp.concatenate([pltpu.bitcast(lax.shift_left(xw, jnp.uint32(16)), F32).astype(BF16),
                             pltpu.bitcast(xw & jnp.uint32(0xFFFF0000), F32).astype(BF16)], axis=1)
        glu = jnp.dot(x, wg_ref[...], preferred_element_type=F32) + bg_ref[0]
        lin = jnp.dot(x, wl_ref[...], preferred_element_type=F32) + bl_ref[0]
        glu = jnp.minimum(glu, SWIGLU_LIMIT)
        lin = jnp.clip(lin, -SWIGLU_LIMIT, SWIGLU_LIMIT)
        o_ref[...] = (glu * jax.nn.sigmoid(SWIGLU_ALPHA * glu) * (lin + 1.0)).astype(o_ref.dtype)


def _expert_up(tile_e, n_used, xs, w_up, b_up, layer):
    n_rows = xs.shape[0]
    pick = jnp.arange(SEL_K)[:, None] - 2 * jnp.arange(SEL_K // 2)[None, :]
    sel = jnp.stack([pick == 0, pick == 1]).astype(BF16)
    bu = b_up[layer].astype(F32)
    bg, bl = bu[:, None, 0::2], bu[:, None, 1::2]
    half_tn = UP_TN // 2
    bspec = pl.BlockSpec((1, 1, half_tn), lambda j, i, te, nu: (te[i], 0, j))
    return pl.pallas_call(
        _up_kernel,
        grid_spec=pltpu.PrefetchScalarGridSpec(
            num_scalar_prefetch=2, grid=(2 * D_FF // UP_TN, n_rows // E_TM),
            in_specs=[pl.BlockSpec((E_TM, D_MODEL // 2), lambda j, i, te, nu: (i, 0)),
                      pl.BlockSpec((None, 1, D_MODEL, UP_TN), lambda j, i, te, nu: (layer, te[i], 0, j)),
                      bspec, bspec,
                      pl.BlockSpec((2, SEL_K, SEL_K // 2), lambda j, i, te, nu: (0, 0, 0))],
            out_specs=pl.BlockSpec((E_TM, half_tn), lambda j, i, te, nu: (i, j)),
            scratch_shapes=[pltpu.VMEM((D_MODEL, half_tn), BF16), pltpu.VMEM((D_MODEL, half_tn), BF16)]),
        out_shape=jax.ShapeDtypeStruct((n_rows, D_FF), BF16),
        compiler_params=_cparams("arbitrary", "arbitrary"),
        name="moe_up",
    )(tile_e, n_used, xs, w_up, bg, bl, sel)


def _down_kernel(te_ref, nu_ref, a_ref, w_ref, b_ref, o_ref, wbf_ref):
    i = pl.program_id(1)

    @pl.when(i < nu_ref[0])
    def _():
        @pl.when(_weights_changed(te_ref, i))
        def _():
            wbf_ref[...] = w_ref[0].astype(BF16)

        o_ref[...] = jnp.dot(a_ref[...], wbf_ref[...], preferred_element_type=F32) + b_ref[0]


def _expert_down(tile_e, n_used, act, w_down, b_down, layer):
    n_rows = act.shape[0]
    return pl.pallas_call(
        _down_kernel,
        grid_spec=pltpu.PrefetchScalarGridSpec(
            num_scalar_prefetch=2, grid=(D_MODEL // DN_TN, n_rows // E_TM),
            in_specs=[pl.BlockSpec((E_TM, D_FF), lambda j, i, te, nu: (i, 0)),
                      pl.BlockSpec((None, 1, D_FF, DN_TN), lambda j, i, te, nu: (layer, te[i], 0, j)),
                      pl.BlockSpec((None, 1, 1, DN_TN), lambda j, i, te, nu: (layer, te[i], 0, j))],
            out_specs=pl.BlockSpec((E_TM, DN_TN), lambda j, i, te, nu: (i, j)),
            scratch_shapes=[pltpu.VMEM((D_FF, DN_TN), BF16)]),
        out_shape=jax.ShapeDtypeStruct((n_rows, D_MODEL), F32),
        compiler_params=_cparams("arbitrary", "arbitrary"),
        name="moe_down",
    )(tile_e, n_used, act, w_down, b_down.reshape(DEPTH, N_EXPERTS, 1, D_MODEL))


def _combine_kernel(pos_ref, ys_ref, tw_ref, x_ref, mod_ref, g_ref, *rest, split_blk):
    o_refs, (buf, sem) = rest[:-2], rest[-2:]

    def copy(a):
        return pltpu.make_async_copy(ys_ref.at[pl.ds(pos_ref[a], 1)], buf.at[a % TOP_K, pl.ds(a // TOP_K, 1)], sem)

    _row_dma_burst(copy, COMB_TT * TOP_K)
    tw = tw_ref[...]
    f = tw[:, 0:1] * buf[0]
    for k in range(1, TOP_K):
        f = f + tw[:, k:k + 1] * buf[k]
    m = mod_ref[0]
    fn = f * lax.rsqrt(jnp.mean(f * f, axis=-1, keepdims=True) + EPS) * g_ref[3:4]
    out = x_ref[...] + m[5:6] * fn
    if split_blk is None:
        o_refs[0][...] = out
    else:
        @pl.when(pl.program_id(0) < split_blk)
        def _():
            o_refs[0][...] = out

        @pl.when(pl.program_id(0) >= split_blk)
        def _():
            o_refs[1][...] = out


def _combine(pos_flat, ys, tw, x, modseg, gains, split_rows=None):
    t = x.shape[0]
    gains8 = jnp.pad(gains.astype(F32), ((0, 4), (0, 0)))
    if split_rows is None:
        split_blk = None
        out_specs = pl.BlockSpec((COMB_TT, D_MODEL), lambda i: (i, 0))
        out_shape = jax.ShapeDtypeStruct((t, D_MODEL), F32)
    else:
        split_blk = split_rows // COMB_TT
        assert split_rows % COMB_TT == 0 and 0 < split_blk < t // COMB_TT
        out_specs = [pl.BlockSpec((COMB_TT, D_MODEL), lambda i: (jnp.minimum(i, split_blk - 1), 0)),
                     pl.BlockSpec((COMB_TT, D_MODEL), lambda i: (jnp.maximum(i - split_blk, 0), 0))]
        out_shape = [jax.ShapeDtypeStruct((split_rows, D_MODEL), F32),
                     jax.ShapeDtypeStruct((t - split_rows, D_MODEL), F32)]
    return pl.pallas_call(
        functools.partial(_combine_kernel, split_blk=split_blk),
        grid=(t // COMB_TT,),
        in_specs=[pl.BlockSpec((COMB_TT * TOP_K,), lambda i: (i,), memory_space=pltpu.SMEM),
                  pl.BlockSpec(memory_space=pl.ANY),
                  pl.BlockSpec((COMB_TT, LANES), lambda i: (i, 0)),
                  pl.BlockSpec((COMB_TT, D_MODEL), lambda i: (i, 0)),
                  pl.BlockSpec((1, 8, D_MODEL), lambda i: (i * COMB_TT // SEG, 0, 0)),
                  pl.BlockSpec((8, D_MODEL), lambda i: (0, 0))],
        out_specs=out_specs,
        scratch_shapes=[pltpu.VMEM((TOP_K, COMB_TT, D_MODEL), F32), pltpu.SemaphoreType.DMA(())],
        out_shape=out_shape,
        compiler_params=_cparams("arbitrary"),
        name="moe_combine",
    )(pos_flat, ys, tw, x, modseg, gains8)


def _moe(h2p, logits, x, modseg, gains, w_up, b_up, w_down, b_down, layer, split_rows=None):
    t = h2p.shape[0]
    ti, tw, rk, cnt = _route(logits)
    counts = cnt[0, :N_EXPERTS].astype(I32)
    padded = (counts + E_TM - 1) // E_TM * E_TM
    upto = jnp.arange(N_EXPERTS)[None, :] <= jnp.arange(N_EXPERTS)[:, None]
    ends = jnp.sum(jnp.where(upto, padded[None, :], 0), axis=1).astype(I32)
    offsets = ends - padded
    n_tiles = t * TOP_K // E_TM + N_EXPERTS
    starts = jnp.arange(n_tiles, dtype=I32) * E_TM
    tile_e = jnp.minimum(jnp.sum((ends[None, :] <= starts[:, None]).astype(I32), axis=1), N_EXPERTS - 1)
    n_used = (ends[-1:] // E_TM).astype(I32)
    ti4, rk4 = ti[:, :TOP_K], rk[:, :TOP_K]
    first_row = jnp.sum(jnp.where(ti4[:, :, None] == jnp.arange(N_EXPERTS)[None, None, :], offsets[None, None, :], 0), axis=-1)
    pos_flat = (first_row + rk4).astype(I32).reshape(-1)
    xs = _dispatch(pos_flat, h2p, n_tiles * E_TM)
    act = _expert_up(tile_e, n_used, xs, w_up, b_up, layer)
    ys = _expert_down(tile_e, n_used, act, w_down, b_down, layer)
    return _combine(pos_flat, ys, tw, x, modseg, gains, split_rows)


def _forward(x, c8, seg_rows, lay, t5_bias, w_mod, b_mod, norm_gains, w_in, qk_norm_gains, na_rpb,
             ret_decay_logit, w_branch, w_out, w_router, b_router, w_up, b_up, w_down, b_down):
    mod = _modulation(c8, w_mod, b_mod)
    modseg = mod[:, seg_rows, :].reshape(DEPTH, len(seg_rows), 6, D_MODEL)
    modseg = jnp.pad(modseg, ((0, 0), (0, 0), (0, 2), (0, 0)))
    cos_a, sin_a = _axial_tables(lay)
    cos_r, sin_r = _rope_tables(lay)
    dil_bias = [_dil_bias(t5_bias, g, d, w // (2 * d)) for g, (w, d) in enumerate(DIL_CONFIGS)]
    for l in range(DEPTH):
        gains = norm_gains[l]
        proj = _norm_inproj(x, modseg[l], gains[0], w_in[l].astype(BF16))
        a_parts = [_dilated_group(proj, dil_bias[g], lay, g) for g in range(A_GROUPS)]
        o_b = _axial_gqa(proj, _b_prepare(proj, qk_norm_gains[l], cos_a, sin_a), lay)
        o_c = _neighbourhood(proj, na_rpb[l], lay)
        o_d = _retention(proj, ret_decay_logit[l], cos_r, sin_r, lay)
        merged = _branch_merge(proj, a_parts, o_b, o_c, o_d, w_branch[l].astype(BF16))
        x, h2p, logits = _outproj_residual(merged, x, modseg[l], gains, w_out[l].astype(BF16), w_router[l], b_router[l])
        x = _moe(h2p, logits, x, modseg[l], gains, w_up, b_up, w_down, b_down, l,
                 split_rows=lay.t_p if l == DEPTH - 1 else None)
    return x


def kernel(x_prompt, x_sample, c_prompt, c_sample, t5_bias, w_mod, b_mod, norm_gains, w_in, qk_norm_gains,
           na_rpb, ret_decay_logit, w_branch, w_out, w_router, b_router, w_up, b_up, w_down, b_down):
    n_p, s_p, _ = x_prompt.shape
    n_s, s_s, _ = x_sample.shape
    lay = Layout(n_p, s_p, n_s, s_s)
    assert s_p % SEG == 0 and s_s % SEG == 0 and n_p + n_s <= 8
    x = jnp.concatenate([x_prompt.reshape(-1, D_MODEL), x_sample.reshape(-1, D_MODEL)], axis=0)
    c8 = jnp.concatenate([c_prompt, c_sample, jnp.zeros((8 - n_p - n_s, D_MODEL), F32)], axis=0)
    seg_rows = tuple([b for b in range(n_p) for _ in range(s_p // SEG)]
                     + [n_p + b for b in range(n_s) for _ in range(s_s // SEG)])
    y_p, y_s = _forward(x, c8, jnp.asarray(seg_rows, dtype=I32), lay, t5_bias, w_mod, b_mod, norm_gains, w_in,
                        qk_norm_gains, na_rpb, ret_decay_logit, w_branch, w_out, w_router, b_router,
                        w_up, b_up, w_down, b_down)
    return (y_p.reshape(n_p, s_p, D_MODEL), y_s.reshape(n_s, s_s, D_MODEL))
```

```python
import functools
import math
from typing import NamedTuple

import jax
import jax.numpy as jnp
from jax import lax
from jax.experimental import pallas as pl
from jax.experimental.pallas import tpu as pltpu

F32 = jnp.float32
BF16 = jnp.bfloat16
I32 = jnp.int32

D_MODEL = 2048
DEPTH = 2
HEAD_DIM = 128
GRID_W = 64
EPS = 1e-6
ROPE_THETA = 10000.0
NEG_INF = -1e30
LOG2E = 1.4426950408889634

DIL_CONFIGS = ((128, 1), (512, 4), (2048, 16))
A_GROUPS = 3
A_HEADS_PER_GROUP = 6
A_HEADS = A_GROUPS * A_HEADS_PER_GROUP
T5_BUCKETS = 32
T5_MAX_DIST = 1024
B_Q_HEADS = 6
B_KV_HEADS = 2
C_HEADS = 6
NA_ROWS = 8
NA_COLS = 16
D_HEADS = 4
D_QK = 128
D_V = 256
N_EXPERTS = 32
TOP_K = 4
D_FF = 2048
SWIGLU_LIMIT = 7.0
SWIGLU_ALPHA = 1.702
N_BRANCH = 4

A_OUT = A_HEADS_PER_GROUP * HEAD_DIM
B_OUT = B_Q_HEADS * HEAD_DIM
C_OUT = C_HEADS * HEAD_DIM
D_OUT = D_HEADS * D_V
MIX_WIDTH = A_OUT + B_OUT + C_OUT + D_OUT

IN_SPLITS = (A_HEADS * HEAD_DIM, A_HEADS * HEAD_DIM, A_HEADS * HEAD_DIM,
             B_Q_HEADS * HEAD_DIM, B_KV_HEADS * HEAD_DIM, B_KV_HEADS * HEAD_DIM,
             C_HEADS * HEAD_DIM, C_HEADS * HEAD_DIM, C_HEADS * HEAD_DIM,
             D_HEADS * D_QK, D_HEADS * D_QK, D_HEADS * D_V, D_HEADS * D_V,
             N_BRANCH * D_MODEL)
N_IN = sum(IN_SPLITS)
_OFF = [0]
for _w in IN_SPLITS:
    _OFF.append(_OFF[-1] + _w)
(OFF_AQ, OFF_AK, OFF_AV, OFF_BQ, OFF_BK, OFF_BV, OFF_CQ, OFF_CK, OFF_CV,
 OFF_DQ, OFF_DK, OFF_DV, OFF_DG, OFF_GATE) = _OFF[:-1]

SEG = 2048
LANES = 128
VMEM_LIMIT = 56 * 1024 * 1024


class Layout(NamedTuple):
    n_p: int
    s_p: int
    n_s: int
    s_s: int

    @property
    def t_p(self):
        return self.n_p * self.s_p

    @property
    def t(self):
        return self.n_p * self.s_p + self.n_s * self.s_s


def _cparams(*sem):
    return pltpu.CompilerParams(dimension_semantics=sem, vmem_limit_bytes=VMEM_LIMIT)


def _mod_kernel(c_ref, w_ref, b_ref, o_ref):
    c = c_ref[...]
    s = (c * jax.nn.sigmoid(c)).astype(BF16)
    o_ref[0] = jnp.dot(s, w_ref[0].astype(BF16), preferred_element_type=F32) + b_ref[0]


def _modulation(c8, w_mod, b_mod):
    tn = 1024
    n6 = 6 * D_MODEL
    return pl.pallas_call(
        _mod_kernel,
        grid=(DEPTH, n6 // tn),
        in_specs=[pl.BlockSpec((8, D_MODEL), lambda l, j: (0, 0)),
                  pl.BlockSpec((1, D_MODEL, tn), lambda l, j: (l, 0, j)),
                  pl.BlockSpec((1, 1, tn), lambda l, j: (l, 0, j))],
        out_specs=pl.BlockSpec((1, 8, tn), lambda l, j: (l, 0, j)),
        out_shape=jax.ShapeDtypeStruct((DEPTH, 8, n6), F32),
        compiler_params=_cparams("parallel", "parallel"),
        name="adaln_mod",
    )(c8, w_mod, b_mod.reshape(DEPTH, 1, n6))


def _inproj_kernel(x_ref, mod_ref, g_ref, w_ref, o_ref, h_ref):
    @pl.when(pl.program_id(1) == 0)
    def _():
        x = x_ref[...]
        y = x * lax.rsqrt(jnp.mean(x * x, axis=-1, keepdims=True) + EPS) * g_ref[...]
        m = mod_ref[0]
        h_ref[...] = (y * (1.0 + m[1:2]) + m[0:1]).astype(BF16)

    o_ref[...] = jnp.dot(h_ref[...], w_ref[...], preferred_element_type=F32).astype(o_ref.dtype)


def _norm_inproj(x, modseg, gain, w_in_bf16):
    t = x.shape[0]
    tm, tn = 1024, 1280
    assert t % tm == 0 and N_IN % tn == 0 and SEG % tm == 0
    return pl.pallas_call(
        _inproj_kernel,
        grid=(t // tm, N_IN // tn),
        in_specs=[pl.BlockSpec((tm, D_MODEL), lambda i, j: (i, 0)),
                  pl.BlockSpec((1, 8, D_MODEL), lambda i, j: (i * tm // SEG, 0, 0)),
                  pl.BlockSpec((1, D_MODEL), lambda i, j: (0, 0)),
                  pl.BlockSpec((D_MODEL, tn), lambda i, j: (0, j))],
        out_specs=pl.BlockSpec((tm, tn), lambda i, j: (i, j)),
        out_shape=jax.ShapeDtypeStruct((t, N_IN), BF16),
        scratch_shapes=[pltpu.VMEM((tm, D_MODEL), BF16)],
        compiler_params=_cparams("parallel", "arbitrary"),
        name="norm_inproj",
    )(x, modseg, gain.reshape(1, D_MODEL), w_in_bf16)


def _local_pos(lay):
    return jnp.concatenate([jnp.tile(jnp.arange(lay.s_p), lay.n_p), jnp.tile(jnp.arange(lay.s_s), lay.n_s)])


def _axial_tables(lay):
    pos = _local_pos(lay)
    lane = jnp.arange(LANES)
    quarter = HEAD_DIM // 4
    freqs = ROPE_THETA ** (-jnp.arange(quarter, dtype=F32) / quarter)
    f = freqs[lane % quarter]
    p = jnp.where(lane[None, :] < HEAD_DIM // 2, (pos // GRID_W)[:, None], (pos % GRID_W)[:, None]).astype(F32)
    ang = p * f[None, :]
    sign = jnp.where((lane % (2 * quarter)) < quarter, -1.0, 1.0).astype(F32)
    return jnp.cos(ang), jnp.sin(ang) * sign[None, :]


def _rope_tables(lay):
    pos = _local_pos(lay)
    lane = jnp.arange(LANES)
    half = D_QK // 2
    freqs = ROPE_THETA ** (-jnp.arange(half, dtype=F32) / half)
    ang = pos.astype(F32)[:, None] * freqs[lane % half][None, :]
    sign = jnp.where(lane < half, -1.0, 1.0).astype(F32)
    return jnp.cos(ang), jnp.sin(ang) * sign[None, :]


def _lane_iota(shape):
    return lax.broadcasted_iota(I32, shape, len(shape) - 1)


def _axial_rotate(x, cos, sin_signed):
    q = HEAD_DIM // 4
    lo = (_lane_iota(x.shape) % (2 * q)) < q
    partner = jnp.where(lo, pltpu.roll(x, LANES - q, 1), pltpu.roll(x, q, 1))
    return x * cos + partner * sin_signed


def _rope_rotate(x, cos, sin_signed):
    return x * cos + pltpu.roll(x, D_QK // 2, 1) * sin_signed


def _bprep_kernel(x_ref, g_ref, cos_ref, sin_ref, o_ref):
    x = x_ref[...].astype(F32)
    y = x * lax.rsqrt(jnp.mean(x * x, axis=-1, keepdims=True) + EPS) * g_ref[0]
    y = _axial_rotate(y, cos_ref[...], sin_ref[...])
    scale = jnp.where(pl.program_id(1) < B_Q_HEADS, HEAD_DIM ** -0.5 * LOG2E, 1.0)
    o_ref[...] = (y * scale).astype(o_ref.dtype)


def _b_prepare(proj, qk_g, cos, sin):
    t = proj.shape[0]
    tr = 1024
    nh = B_Q_HEADS + B_KV_HEADS
    g8 = jnp.concatenate([jnp.tile(qk_g[0:1], (B_Q_HEADS, 1)), jnp.tile(qk_g[1:2], (B_KV_HEADS, 1))]).reshape(nh, 1, HEAD_DIM)
    return pl.pallas_call(
        _bprep_kernel,
        grid=(t // tr, nh),
        in_specs=[pl.BlockSpec((tr, HEAD_DIM), lambda i, h: (i, OFF_BQ // HEAD_DIM + h)),
                  pl.BlockSpec((1, 1, HEAD_DIM), lambda i, h: (h, 0, 0)),
                  pl.BlockSpec((tr, HEAD_DIM), lambda i, h: (i, 0)),
                  pl.BlockSpec((tr, HEAD_DIM), lambda i, h: (i, 0))],
        out_specs=pl.BlockSpec((tr, HEAD_DIM), lambda i, h: (i, h)),
        out_shape=jax.ShapeDtypeStruct((t, nh * HEAD_DIM), BF16),
        compiler_params=_cparams("parallel", "parallel"),
        name="b_prep",
    )(proj, g8, cos, sin)


FLASH_ROW_SPLIT = 2


def _flash_kernel(qt_ref, kt_ref, first_ref, last_ref, q_ref, k_ref, v_ref, o_ref, m_ref, acc_ref):
    s_id = pl.program_id(1)
    rep = B_Q_HEADS // B_KV_HEADS

    @pl.when(first_ref[s_id] == 1)
    def _():
        m_ref[...] = jnp.full(m_ref.shape, -jnp.inf, F32)
        acc_ref[...] = jnp.zeros(acc_ref.shape, F32)

    k = k_ref[...]
    v = v_ref[...]
    rb = q_ref.shape[0] // FLASH_ROW_SPLIT
    units = [(r, slice(u * rb, (u + 1) * rb)) for r in range(rep) for u in range(FLASH_ROW_SPLIT)]
    scores = [lax.dot_general(q_ref[rows, r * HEAD_DIM:(r + 1) * HEAD_DIM], k, (((1,), (1,)), ((), ())),
                              preferred_element_type=F32) for (r, rows) in units]
    probs, alphas = [], []
    for (r, rows), s in zip(units, scores):
        m_prev = m_ref[r, rows, :]
        m_cur = jnp.maximum(m_prev, jnp.max(s, axis=-1, keepdims=True))
        alphas.append(jnp.exp2(m_prev - m_cur))
        probs.append(jnp.exp2(s - m_cur[:, 0:1]).astype(BF16))
        m_ref[r, rows, :] = m_cur
    for (r, rows), p, alpha in zip(units, probs, alphas):
        pv = jnp.dot(p, v, preferred_element_type=F32)
        acc_ref[r, rows, :] = jnp.concatenate([alpha, alpha], axis=1) * acc_ref[r, rows, :] + pv

    @pl.when(last_ref[s_id] == 1)
    def _():
        for r in range(rep):
            acc = acc_ref[r]
            o_ref[:, r * HEAD_DIM:(r + 1) * HEAD_DIM] = (acc[:, :HEAD_DIM] / acc[:, HEAD_DIM:]).astype(o_ref.dtype)


def _flash_tables(lay, tq, tk):
    qt, kt, first, last = [], [], [], []
    for (n, s, base) in ((lay.n_p, lay.s_p, 0), (lay.n_s, lay.s_s, lay.t_p)):
        for b in range(n):
            for qi in range(s // tq):
                nk = s // tk
                for ki in range(nk):
                    qt.append((base + b * s) // tq + qi)
                    kt.append((base + b * s) // tk + ki)
                    first.append(int(ki == 0))
                    last.append(int(ki == nk - 1))
    mk = lambda a: jnp.asarray(a, dtype=I32)
    return mk(qt), mk(kt), mk(first), mk(last)


def _axial_gqa(proj, bprep, lay):
    t = proj.shape[0]
    tq, tk = 512, 2048
    assert lay.s_p % tk == 0 and lay.s_s % tk == 0
    rep = B_Q_HEADS // B_KV_HEADS
    qt, kt, first, last = _flash_tables(lay, tq, tk)
    n_steps = qt.shape[0]
    v = proj[:, OFF_BV:OFF_BV + B_KV_HEADS * HEAD_DIM].reshape(t, B_KV_HEADS, HEAD_DIM)
    v_ones = jnp.concatenate([v, jnp.ones_like(v)], axis=-1).reshape(t, B_KV_HEADS * 2 * HEAD_DIM)
    gs = pltpu.PrefetchScalarGridSpec(
        num_scalar_prefetch=4,
        grid=(B_KV_HEADS, n_steps),
        in_specs=[pl.BlockSpec((tq, rep * HEAD_DIM), lambda g, s, qt, kt, f, l: (qt[s], g)),
                  pl.BlockSpec((tk, HEAD_DIM), lambda g, s, qt, kt, f, l: (kt[s], B_Q_HEADS + g)),
                  pl.BlockSpec((tk, 2 * HEAD_DIM), lambda g, s, qt, kt, f, l: (kt[s], g))],
        out_specs=pl.BlockSpec((tq, rep * HEAD_DIM), lambda g, s, qt, kt, f, l: (qt[s], g)),
        scratch_shapes=[pltpu.VMEM((rep, tq, LANES), F32), pltpu.VMEM((rep, tq, 2 * HEAD_DIM), F32)],
    )
    return pl.pallas_call(
        _flash_kernel,
        grid_spec=gs,
        out_shape=jax.ShapeDtypeStruct((t, B_OUT), BF16),
        compiler_params=_cparams("parallel", "arbitrary"),
        name="b_flash",
    )(qt, kt, first, last, bprep, bprep, v_ones)


A_BQ = 128
A_NSUB = 2


def _t5_bucket(rel):
    nb = T5_BUCKETS // 2
    max_exact = nb // 2
    n = jnp.abs(rel)
    large = max_exact + (jnp.log(jnp.maximum(n, 1).astype(F32) / max_exact)
                         / math.log(T5_MAX_DIST / max_exact) * (nb - max_exact)).astype(I32)
    large = jnp.minimum(large, nb - 1)
    return jnp.where(rel > 0, nb, 0) + jnp.where(n < max_exact, n, large)


def _dil_bias(t5_bias, g, d, half):
    rel = (jnp.arange(3 * A_BQ)[None, :] - A_BQ) - jnp.arange(A_BQ)[:, None]
    tab = t5_bias[:, g * A_HEADS_PER_GROUP:(g + 1) * A_HEADS_PER_GROUP].astype(F32)
    onehot = (_t5_bucket(rel * d)[:, :, None] == jnp.arange(T5_BUCKETS)[None, None, :]).astype(F32)
    bias = jnp.einsum("qkb,bh->hqk", onehot, tab, precision=lax.Precision.HIGHEST)
    return jnp.where((jnp.abs(rel) <= half)[None], bias, NEG_INF)


def _dil_kernel(q_ref, kp_ref, kc_ref, kn_ref, vp_ref, vc_ref, vn_ref, b_ref, o_ref, lse_ref, *, nblk_p, tblk_p, nblk_s):
    scale = HEAD_DIM ** -0.5
    dn = (((1,), (1,)), ((), ()))
    units = []
    for s in range(A_NSUB):
        r = pl.program_id(1) * A_NSUB + s
        in_p = r < tblk_p
        nblk = jnp.where(in_p, nblk_p, nblk_s)
        il = jnp.where(in_p, r, r - tblk_p) % nblk
        prev_ok = il > 0
        next_ok = il < nblk - 1
        rows = slice(s * A_BQ, (s + 1) * A_BQ)
        before = (kp_ref, vp_ref, slice((A_NSUB - 1) * A_BQ, A_NSUB * A_BQ)) if s == 0 else \
            (kc_ref, vc_ref, slice((s - 1) * A_BQ, s * A_BQ))
        after = (kn_ref, vn_ref, slice(0, A_BQ)) if s == A_NSUB - 1 else \
            (kc_ref, vc_ref, slice((s + 1) * A_BQ, (s + 2) * A_BQ))
        for h in range(A_HEADS_PER_GROUP):
            sl = slice(h * HEAD_DIM, (h + 1) * HEAD_DIM)
            q = q_ref[rows, sl]
            b = b_ref[h]
            sp = lax.dot_general(q, before[0][before[2], sl], dn, preferred_element_type=F32) * scale + b[:, 0:A_BQ]
            sc = lax.dot_general(q, kc_ref[rows, sl], dn, preferred_element_type=F32) * scale + b[:, A_BQ:2 * A_BQ]
            sn = lax.dot_general(q, after[0][after[2], sl], dn, preferred_element_type=F32) * scale + b[:, 2 * A_BQ:]
            sp = jnp.where(prev_ok, sp, NEG_INF)
            sn = jnp.where(next_ok, sn, NEG_INF)
            units.append((rows, sl, before, after, sp, sc, sn))
    soft = []
    for (_, _, _, _, sp, sc, sn) in units:
        m = jnp.maximum(jnp.maximum(jnp.max(sp, axis=-1, keepdims=True), jnp.max(sc, axis=-1, keepdims=True)),
                        jnp.max(sn, axis=-1, keepdims=True))
        pp, pc, pn = jnp.exp(sp - m), jnp.exp(sc - m), jnp.exp(sn - m)
        l = (jnp.sum(pp, axis=-1, keepdims=True) + jnp.sum(pc, axis=-1, keepdims=True)
             + jnp.sum(pn, axis=-1, keepdims=True))
        soft.append((m, l, pp, pc, pn))
    for (rows, sl, before, after, _, _, _), (m, l, pp, pc, pn) in zip(units, soft):
        o = (jnp.dot(pp.astype(BF16), before[1][before[2], sl], preferred_element_type=F32)
             + jnp.dot(pc.astype(BF16), vc_ref[rows, sl], preferred_element_type=F32)
             + jnp.dot(pn.astype(BF16), after[1][after[2], sl], preferred_element_type=F32))
        o_ref[rows, sl] = (o / l).astype(o_ref.dtype)
        lse_ref[rows, sl] = jnp.broadcast_to(m + jnp.log(l), (A_BQ, HEAD_DIM))


def _dilated_group(proj, bias, lay, g):
    w, d = DIL_CONFIGS[g]
    t = proj.shape[0]
    rows = t // d
    nblk_p = lay.s_p // d // A_BQ
    nblk_s = lay.s_s // d // A_BQ
    assert nblk_p >= 1 and nblk_s >= 1 and w // (2 * d) <= A_BQ
    tblk_p = lay.t_p // d // A_BQ
    br = A_BQ * A_NSUB
    assert rows % br == 0
    tblk = rows // br
    gw = A_HEADS_PER_GROUP * HEAD_DIM

    def spec(shift):
        return pl.BlockSpec((br, gw), lambda c, i: (jnp.clip(i + shift, 0, tblk - 1), c))

    if d == 1:
        def win(off, shift):
            return pl.BlockSpec(
                (pl.Element(br), pl.Element(gw)),
                lambda c, i: (pl.multiple_of(jnp.clip(i + shift, 0, tblk - 1) * br, br), off + g * gw))
        q_c = k_c = v_c = proj
        qkv_specs = [win(OFF_AQ, 0), win(OFF_AK, -1), win(OFF_AK, 0), win(OFF_AK, 1),
                     win(OFF_AV, -1), win(OFF_AV, 0), win(OFF_AV, 1)]
    else:
        q_c, k_c, v_c = [proj[:, off + g * gw:off + (g + 1) * gw].reshape(rows, d * gw)
                         for off in (OFF_AQ, OFF_AK, OFF_AV)]
        qkv_specs = [spec(0), spec(-1), spec(0), spec(1), spec(-1), spec(0), spec(1)]

    kern = functools.partial(_dil_kernel, nblk_p=nblk_p, tblk_p=tblk_p, nblk_s=nblk_s)
    o, lse = pl.pallas_call(
        kern,
        grid=(d, tblk),
        in_specs=qkv_specs + [pl.BlockSpec((A_HEADS_PER_GROUP, A_BQ, 3 * A_BQ), lambda c, i: (0, 0, 0))],
        out_specs=[spec(0), spec(0)],
        out_shape=[jax.ShapeDtypeStruct((rows, d * gw), BF16), jax.ShapeDtypeStruct((rows, d * gw), F32)],
        compiler_params=_cparams("parallel", "parallel"),
        name=f"a_dilated_g{g}",
    )(q_c, k_c, k_c, k_c, v_c, v_c, v_c, bias)
    return o.reshape(t, gw), lse.reshape(t, gw)


C_QROWS = 8
C_KROWS = 2 * NA_ROWS
C_TQ = C_QROWS * GRID_W
C_TK = C_KROWS * GRID_W


def _na_bias(rpb):
    hi = lax.Precision.HIGHEST
    qc = jnp.arange(GRID_W)[:, None]
    kc = jnp.arange(GRID_W)[None, :]
    cstart = jnp.clip(qc - NA_COLS // 2, 0, GRID_W - NA_COLS)
    col_ok = (kc >= cstart) & (kc < cstart + NA_COLS)
    ci = jnp.clip(kc - qc, -(NA_COLS - 1), NA_COLS - 1) + NA_COLS - 1
    oh_c = (ci[:, :, None] == jnp.arange(2 * NA_COLS - 1)[None, None, :]).astype(F32)
    by_col = jnp.einsum("hrc,abc->hrab", rpb.astype(F32), oh_c, precision=hi)
    out = []
    for off in (0, NA_ROWS // 2, NA_ROWS):
        qr = (off + jnp.arange(C_QROWS))[:, None]
        kr = jnp.arange(C_KROWS)[None, :]
        rstart = jnp.clip(qr - NA_ROWS // 2, 0, C_KROWS - NA_ROWS)
        row_ok = (kr >= rstart) & (kr < rstart + NA_ROWS)
        ri = jnp.clip(kr - qr + NA_ROWS - 1, 0, 2 * NA_ROWS - 2)
        oh_r = (ri[:, :, None] == jnp.arange(2 * NA_ROWS - 1)[None, None, :]).astype(F32)
        b = jnp.einsum("qkr,hrab->hqakb", oh_r, by_col, precision=hi)
        ok = row_ok[:, None, :, None] & col_ok[None, :, None, :]
        out.append(jnp.where(ok[None], b, NEG_INF).reshape(C_HEADS, C_TQ, C_TK))
    return jnp.stack(out)


C_ROW_SPLIT = 2


def _na_kernel(q_ref, k_ref, v_ref, b_ref, o_ref):
    k = k_ref[...]
    v = v_ref[...]
    rb = C_TQ // C_ROW_SPLIT
    units = [slice(u * rb, (u + 1) * rb) for u in range(C_ROW_SPLIT)]
    scores = [lax.dot_general(q_ref[rows, :], k, (((1,), (1,)), ((), ())), preferred_element_type=F32)
              * (HEAD_DIM ** -0.5) + b_ref[rows, :] for rows in units]
    soft = []
    for s in scores:
        m = jnp.max(s, axis=-1, keepdims=True)
        p = jnp.exp(s - m)
        soft.append((p.astype(BF16), jnp.sum(p, axis=-1, keepdims=True)))
    for rows, (p, l) in zip(units, soft):
        o_ref[rows, :] = (jnp.dot(p, v, preferred_element_type=F32) / l).astype(o_ref.dtype)


def _neighbourhood(proj, rpb, lay):
    t = proj.shape[0]
    r_p, r_s = lay.s_p // GRID_W, lay.s_s // GRID_W
    assert r_p >= C_KROWS and r_s >= C_KROWS and r_p % C_QROWS == 0 and r_s % C_QROWS == 0
    blk_p = lay.t_p // C_TQ
    bias = _na_bias(rpb)

    def window(i):
        in_p = i < blk_p
        per_seq = jnp.where(in_p, lay.s_p // C_TQ, lay.s_s // C_TQ)
        rows = jnp.where(in_p, r_p, r_s)
        il = jnp.where(in_p, i, i - blk_p)
        seq0 = (i - il % per_seq) * C_TQ
        r0 = (il % per_seq) * C_QROWS
        w0 = jnp.clip(r0 - NA_ROWS // 2, 0, rows - C_KROWS)
        return seq0 + w0 * GRID_W, (r0 - w0) // (NA_ROWS // 2)

    def kv_spec(off):
        return pl.BlockSpec((pl.Element(C_TK), pl.Element(HEAD_DIM)),
                            lambda h, i: (pl.multiple_of(window(i)[0], GRID_W),
                                          pl.multiple_of(off + h * HEAD_DIM, LANES)))

    return pl.pallas_call(
        _na_kernel,
        grid=(C_HEADS, t // C_TQ),
        in_specs=[pl.BlockSpec((C_TQ, HEAD_DIM), lambda h, i: (i, OFF_CQ // HEAD_DIM + h)),
                  kv_spec(OFF_CK), kv_spec(OFF_CV),
                  pl.BlockSpec((None, None, C_TQ, C_TK), lambda h, i: (window(i)[1], h, 0, 0))],
        out_specs=pl.BlockSpec((C_TQ, HEAD_DIM), lambda h, i: (i, h)),
        out_shape=jax.ShapeDtypeStruct((t, C_OUT), BF16),
        compiler_params=_cparams("parallel", "parallel"),
        name="c_neighbourhood",
    )(proj, proj, proj, bias)


RET_C = 256


def _log_sigmoid(x):
    return jnp.minimum(x, 0.0) - jnp.log(1.0 + jnp.exp(-jnp.abs(x)))


def _ret_qk(q_ref, k_ref, cos_ref, sin_ref, h):
    sl = slice(h * D_QK, (h + 1) * D_QK)
    cos, sin = cos_ref[...], sin_ref[...]
    q = _rope_rotate(q_ref[:, sl].astype(F32), cos, sin)
    k = _rope_rotate(k_ref[:, sl].astype(F32), cos, sin) * (D_QK ** -0.5)
    return q, k


def _ret_bwd_kernel(cb_ref, first_ref, q_ref, k_ref, v_ref, cos_ref, sin_ref, dl_ref, o_ref, st_ref):
    s_id = pl.program_id(0)

    @pl.when(first_ref[s_id] == 1)
    def _():
        st_ref[...] = jnp.zeros(st_ref.shape, F32)

    row = lax.broadcasted_iota(I32, (RET_C, 1), 0).astype(F32)
    for h in range(D_HEADS):
        lg = _log_sigmoid(dl_ref[D_HEADS + h:D_HEADS + h + 1, :])
        q, k = _ret_qk(q_ref, k_ref, cos_ref, sin_ref, h)
        v = v_ref[:, h * D_V:(h + 1) * D_V]
        st = st_ref[h]
        q_dec = (q * jnp.exp((RET_C - row) * lg)).astype(BF16)
        o_ref[:, h * D_V:(h + 1) * D_V] = jnp.dot(q_dec, st.astype(BF16), preferred_element_type=F32)
        k_dec = (k * jnp.exp(row * lg)).astype(BF16)
        kv = lax.dot_general(k_dec, v, (((0,), (0,)), ((), ())), preferred_element_type=F32)
        st_ref[h] = st * jnp.exp(RET_C * lg[:, 0:1]) + kv


def _ret_fwd_kernel(cb_ref, first_ref, q_ref, k_ref, v_ref, g_ref, xb_ref, cos_ref, sin_ref, dl_ref, o_ref, st_ref):
    s_id = pl.program_id(0)

    @pl.when(first_ref[s_id] == 1)
    def _():
        st_ref[...] = jnp.zeros(st_ref.shape, F32)

    row = lax.broadcasted_iota(I32, (RET_C, 1), 0).astype(F32)
    diff = (lax.broadcasted_iota(I32, (RET_C, RET_C), 0) - lax.broadcasted_iota(I32, (RET_C, RET_C), 1)).astype(F32)
    for h in range(D_HEADS):
        lgf = _log_sigmoid(dl_ref[h:h + 1, :])
        lgb = _log_sigmoid(dl_ref[D_HEADS + h:D_HEADS + h + 1, :])
        q, k = _ret_qk(q_ref, k_ref, cos_ref, sin_ref, h)
        v = v_ref[:, h * D_V:(h + 1) * D_V]
        st = st_ref[h]
        dmat = jnp.where(diff >= 0, jnp.exp(jnp.maximum(diff, 0.0) * lgf[:, 0:1]),
                         jnp.exp(jnp.maximum(-diff, 0.0) * lgb[:, 0:1]))
        s = lax.dot_general(q.astype(BF16), k.astype(BF16), (((1,), (1,)), ((), ())), preferred_element_type=F32)
        o = jnp.dot((s * dmat).astype(BF16), v, preferred_element_type=F32)
        q_dec = (q * jnp.exp((row + 1.0) * lgf)).astype(BF16)
        o = o + jnp.dot(q_dec, st.astype(BF16), preferred_element_type=F32)
        o = o + xb_ref[:, h * D_V:(h + 1) * D_V]
        k_dec = (k * jnp.exp((RET_C - 1.0 - row) * lgf)).astype(BF16)
        kv = lax.dot_general(k_dec, v, (((0,), (0,)), ((), ())), preferred_element_type=F32)
        st_ref[h] = st * jnp.exp(RET_C * lgf[:, 0:1]) + kv
        mu = jnp.mean(o, axis=-1, keepdims=True)
        var = jnp.mean(jnp.square(o - mu), axis=-1, keepdims=True)
        on = (o - mu) * lax.rsqrt(var + EPS)
        g = g_ref[:, h * D_V:(h + 1) * D_V].astype(F32)
        o_ref[:, h * D_V:(h + 1) * D_V] = (g * jax.nn.sigmoid(g) * on).astype(o_ref.dtype)


def _ret_tables(lay, reverse):
    cb, first = [], []
    for (n, s, base) in ((lay.n_p, lay.s_p, 0), (lay.n_s, lay.s_s, lay.t_p)):
        for b in range(n):
            nc = s // RET_C
            order = range(nc - 1, -1, -1) if reverse else range(nc)
            for j, c in enumerate(order):
                cb.append((base + b * s) // RET_C + c)
                first.append(int(j == 0))
    return jnp.asarray(cb, dtype=I32), jnp.asarray(first, dtype=I32)


def _retention(proj, dlogit, cos, sin, lay):
    t = proj.shape[0]
    dl = jnp.broadcast_to(dlogit.astype(F32).reshape(2 * D_HEADS, 1), (2 * D_HEADS, LANES))
    qw, vw = D_HEADS * D_QK, D_HEADS * D_V
    row_spec = lambda width, off: pl.BlockSpec((pl.Element(RET_C), pl.Element(width)),
                                               lambda s, cb, f: (pl.multiple_of(cb[s] * RET_C, RET_C), off))
    tab_spec = pl.BlockSpec((RET_C, LANES), lambda s, cb, f: (cb[s], 0))
    dl_spec = pl.BlockSpec((2 * D_HEADS, LANES), lambda s, cb, f: (0, 0))
    out_spec = pl.BlockSpec((RET_C, vw), lambda s, cb, f: (cb[s], 0))
    state = pltpu.VMEM((D_HEADS, D_QK, D_V), F32)

    cb, first = _ret_tables(lay, True)
    xb = pl.pallas_call(
        _ret_bwd_kernel,
        grid_spec=pltpu.PrefetchScalarGridSpec(
            num_scalar_prefetch=2, grid=(cb.shape[0],),
            in_specs=[row_spec(qw, OFF_DQ), row_spec(qw, OFF_DK), row_spec(vw, OFF_DV), tab_spec, tab_spec, dl_spec],
            out_specs=out_spec, scratch_shapes=[state]),
        out_shape=jax.ShapeDtypeStruct((t, vw), F32),
        compiler_params=_cparams("arbitrary"),
        name="d_retention_bwd",
    )(cb, first, proj, proj, proj, cos, sin, dl)

    cb, first = _ret_tables(lay, False)
    return pl.pallas_call(
        _ret_fwd_kernel,
        grid_spec=pltpu.PrefetchScalarGridSpec(
            num_scalar_prefetch=2, grid=(cb.shape[0],),
            in_specs=[row_spec(qw, OFF_DQ), row_spec(qw, OFF_DK), row_spec(vw, OFF_DV), row_spec(vw, OFF_DG),
                      out_spec, tab_spec, tab_spec, dl_spec],
            out_specs=out_spec, scratch_shapes=[state]),
        out_shape=jax.ShapeDtypeStruct((t, vw), BF16),
        compiler_params=_cparams("arbitrary"),
        name="d_retention_fwd",
    )(cb, first, proj, proj, proj, proj, xb, cos, sin, dl)


def _merge_kernel(oa0, oa1, oa2, ls0, ls1, ls2, ob, oc, od, g0, g1, g2, g3, wa, wb, wc, wd, o_ref, oa_ref):
    @pl.when(pl.program_id(1) == 0)
    def _():
        l0, l1, l2 = ls0[...], ls1[...], ls2[...]
        m = jnp.maximum(jnp.maximum(l0, l1), l2)
        e0, e1, e2 = jnp.exp(l0 - m), jnp.exp(l1 - m), jnp.exp(l2 - m)
        num = e0 * oa0[...].astype(F32) + e1 * oa1[...].astype(F32) + e2 * oa2[...].astype(F32)
        oa_ref[...] = (num / (e0 + e1 + e2)).astype(BF16)

    def term(gate, o, w):
        return jax.nn.sigmoid(gate[...].astype(F32)) * jnp.dot(o, w[...], preferred_element_type=F32)

    acc = term(g0, oa_ref[...], wa) + term(g1, ob[...], wb) + term(g2, oc[...], wc) + term(g3, od[...], wd)
    o_ref[...] = acc.astype(o_ref.dtype)


def _branch_merge(proj, a_parts, o_b, o_c, o_d, w_branch_bf16):
    t = proj.shape[0]
    tm, tn = 512, 512
    (oa0, ls0), (oa1, ls1), (oa2, ls2) = a_parts
    row = lambda width: pl.BlockSpec((tm, width), lambda i, j: (i, 0))
    gate = lambda b: pl.BlockSpec((pl.Element(tm), pl.Element(tn)),
                                  lambda i, j: (pl.multiple_of(i * tm, tm),
                                                pl.multiple_of(OFF_GATE + b * D_MODEL + j * tn, LANES)))
    wspec = lambda width: pl.BlockSpec((width, tn), lambda i, j: (0, j))
    offs = (0, A_OUT, A_OUT + B_OUT, A_OUT + B_OUT + C_OUT, MIX_WIDTH)
    ws = [w_branch_bf16[offs[b]:offs[b + 1]] for b in range(N_BRANCH)]
    return pl.pallas_call(
        _merge_kernel,
        grid=(t // tm, D_MODEL // tn),
        in_specs=[row(A_OUT)] * 6 + [row(B_OUT), row(C_OUT), row(D_OUT)]
                 + [gate(0), gate(1), gate(2), gate(3)]
                 + [wspec(A_OUT), wspec(B_OUT), wspec(C_OUT), wspec(D_OUT)],
        out_specs=pl.BlockSpec((tm, tn), lambda i, j: (i, j)),
        out_shape=jax.ShapeDtypeStruct((t, D_MODEL), BF16),
        scratch_shapes=[pltpu.VMEM((tm, A_OUT), BF16)],
        compiler_params=_cparams("parallel", "arbitrary"),
        name="branch_merge",
    )(oa0, oa1, oa2, ls0, ls1, ls2, o_b, o_c, o_d, proj, proj, proj, proj, *ws)


def _split3(x):
    hi = x.astype(BF16)
    lo = (x - hi.astype(F32)).astype(BF16)
    return hi, lo


def _outproj_kernel(mg_ref, x_ref, mod_ref, g_ref, w_ref, wr_hi, wr_lo, br_ref, xo_ref, h2_ref, lg_ref):
    y = jnp.dot(mg_ref[...], w_ref[...], preferred_element_type=F32)
    m = mod_ref[0]
    g = g_ref[...]
    yn = y * lax.rsqrt(jnp.mean(y * y, axis=-1, keepdims=True) + EPS) * g[1:2]
    x = x_ref[...] + m[2:3] * yn
    xo_ref[...] = x
    h2 = x * lax.rsqrt(jnp.mean(x * x, axis=-1, keepdims=True) + EPS) * g[2:3] * (1.0 + m[4:5]) + m[3:4]
    h2b = h2.astype(BF16)
    half = D_MODEL // 2
    lo = lax.shift_right_logical(pltpu.bitcast(h2b[:, :half].astype(F32), jnp.uint32), jnp.uint32(16))
    hi = pltpu.bitcast(h2b[:, half:].astype(F32), jnp.uint32)
    h2_ref[...] = hi | lo
    lg_ref[...] = (jnp.dot(h2b, wr_hi[...], preferred_element_type=F32)
                   + jnp.dot(h2b, wr_lo[...], preferred_element_type=F32) + br_ref[...])


def _outproj_residual(merged, x, modseg, gains, w_out_bf16, w_router, b_router):
    t = x.shape[0]
    tm = 256
    wr = jnp.pad(w_router.astype(F32), ((0, 0), (0, LANES - N_EXPERTS)))
    wr_hi, wr_lo = _split3(wr)
    br = jnp.pad(b_router.astype(F32), (0, LANES - N_EXPERTS)).reshape(1, LANES)
    gains8 = jnp.pad(gains.astype(F32), ((0, 4), (0, 0)))
    row = lambda width: pl.BlockSpec((tm, width), lambda i: (i, 0))
    full = lambda a, b: pl.BlockSpec((a, b), lambda i: (0, 0))
    return pl.pallas_call(
        _outproj_kernel,
        grid=(t // tm,),
        in_specs=[row(D_MODEL), row(D_MODEL),
                  pl.BlockSpec((1, 8, D_MODEL), lambda i: (i * tm // SEG, 0, 0)),
                  full(8, D_MODEL), full(D_MODEL, D_MODEL), full(D_MODEL, LANES), full(D_MODEL, LANES), full(1, LANES)],
        out_specs=[row(D_MODEL), row(D_MODEL // 2), row(LANES)],
        out_shape=[jax.ShapeDtypeStruct((t, D_MODEL), F32), jax.ShapeDtypeStruct((t, D_MODEL // 2), jnp.uint32),
                   jax.ShapeDtypeStruct((t, LANES), F32)],
        compiler_params=_cparams("parallel"),
        name="outproj_residual",
    )(merged, x, modseg, gains8, w_out_bf16, wr_hi, wr_lo, br)


R_TM = 512


def _route_kernel(lg_ref, ti_ref, tw_ref, rk_ref, cnt_ref, carry_ref):
    @pl.when(pl.program_id(0) == 0)
    def _():
        carry_ref[...] = jnp.zeros(carry_ref.shape, F32)

    lane = _lane_iota((R_TM, LANES))
    l = jnp.where(lane < N_EXPERTS, lg_ref[...], -jnp.inf)
    vals, idxs = [], []
    for _ in range(TOP_K):
        m = jnp.max(l, axis=-1, keepdims=True)
        idx = jnp.min(jnp.where(l == m, lane.astype(F32), float(LANES)), axis=-1, keepdims=True).astype(I32)
        vals.append(m)
        idxs.append(idx)
        l = jnp.where(lane == idx, -jnp.inf, l)
    es = [jnp.exp(v - vals[0]) for v in vals]
    den = es[0] + es[1] + es[2] + es[3]
    ti = jnp.zeros((R_TM, LANES), I32)
    tw = jnp.zeros((R_TM, LANES), F32)
    cnt = jnp.zeros((R_TM, LANES), F32)
    for k in range(TOP_K):
        ti = jnp.where(lane == k, idxs[k], ti)
        tw = jnp.where(lane == k, es[k] / den, tw)
        cnt = cnt + jnp.where(lane == idxs[k], 1.0, 0.0)
    ti_ref[...] = ti
    tw_ref[...] = tw
    r = lax.broadcasted_iota(I32, (R_TM, R_TM), 0)
    c = lax.broadcasted_iota(I32, (R_TM, R_TM), 1)
    tri = jnp.where(c < r, 1.0, 0.0).astype(BF16)
    before = jnp.dot(tri, cnt.astype(BF16), preferred_element_type=F32) + carry_ref[...]
    rk = jnp.zeros((R_TM, LANES), I32)
    for k in range(TOP_K):
        pos = jnp.sum(jnp.where(lane == idxs[k], before, 0.0), axis=-1, keepdims=True)
        rk = jnp.where(lane == k, pos.astype(I32), rk)
    rk_ref[...] = rk
    carry_ref[...] = carry_ref[...] + jnp.sum(cnt, axis=0, keepdims=True)
    cnt_ref[...] = jnp.broadcast_to(carry_ref[...], cnt_ref.shape)


def _route(logits):
    t = logits.shape[0]
    row = pl.BlockSpec((R_TM, LANES), lambda i: (i, 0))
    return pl.pallas_call(
        _route_kernel,
        grid=(t // R_TM,),
        in_specs=[row],
        out_specs=[row, row, row, pl.BlockSpec((8, LANES), lambda i: (0, 0))],
        out_shape=[jax.ShapeDtypeStruct((t, LANES), I32), jax.ShapeDtypeStruct((t, LANES), F32),
                   jax.ShapeDtypeStruct((t, LANES), I32), jax.ShapeDtypeStruct((8, LANES), F32)],
        scratch_shapes=[pltpu.VMEM((1, LANES), F32)],
        compiler_params=_cparams("arbitrary"),
        name="moe_route",
    )(logits)


E_TM = 512
DISP_TT = 512
COMB_TT = 256


ROW_DMA_UNROLL = 8


def _row_dma_burst(copy, n):
    per_trip = ROW_DMA_UNROLL // TOP_K

    def start(g, c):
        for u in range(ROW_DMA_UNROLL):
            copy(g * ROW_DMA_UNROLL + u, g * per_trip + u // TOP_K, u % TOP_K).start(priority=u % 2)
        return c

    def wait(g, c):
        for u in range(ROW_DMA_UNROLL):
            copy(g * ROW_DMA_UNROLL + u, g * per_trip + u // TOP_K, u % TOP_K).wait()
        return c

    lax.fori_loop(0, n // ROW_DMA_UNROLL, start, 0)
    lax.fori_loop(0, n // ROW_DMA_UNROLL, wait, 0)


def _dispatch_kernel(pos_ref, h_ref, xs_in, xs_ref, sem):
    del xs_in

    def copy(a, token, k):
        del k
        return pltpu.make_async_copy(h_ref.at[pl.ds(token, 1)], xs_ref.at[pl.ds(pos_ref[a], 1)], sem)

    _row_dma_burst(copy, DISP_TT * TOP_K)


def _dispatch(pos_flat, h2p, n_rows):
    t, width = h2p.shape
    xs0 = jnp.zeros((n_rows, width), h2p.dtype)
    return pl.pallas_call(
        _dispatch_kernel,
        grid=(t // DISP_TT,),
        in_specs=[pl.BlockSpec((DISP_TT * TOP_K,), lambda i: (i,), memory_space=pltpu.SMEM),
                  pl.BlockSpec((DISP_TT, width), lambda i: (i, 0)),
                  pl.BlockSpec(memory_space=pl.ANY)],
        out_specs=pl.BlockSpec(memory_space=pl.ANY),
        scratch_shapes=[pltpu.SemaphoreType.DMA(())],
        out_shape=jax.ShapeDtypeStruct((n_rows, width), h2p.dtype),
        input_output_aliases={2: 0},
        compiler_params=_cparams("arbitrary"),
        name="moe_dispatch",
    )(pos_flat, h2p, xs0)


UP_TN = 1024
DN_TN = 1024
SEL_K = 256


def _weights_changed(te_ref, i):
    return (i == 0) | (te_ref[i] != te_ref[jnp.maximum(i - 1, 0)])


def _up_kernel(te_ref, nu_ref, x_ref, w_ref, bg_ref, bl_ref, sel_ref, o_ref, wg_ref, wl_ref):
    i = pl.program_id(1)

    @pl.when(i < nu_ref[0])
    def _():
        @pl.when(_weights_changed(te_ref, i))
        def _():
            for c in range(UP_TN // SEL_K):
                wc = w_ref[0, :, c * SEL_K:(c + 1) * SEL_K].astype(BF16)
                cols = slice(c * (SEL_K // 2), (c + 1) * (SEL_K // 2))
                both = jnp.dot(wc, sel_ref[...], preferred_element_type=F32).astype(BF16)
                wg_ref[:, cols] = both[:, :SEL_K // 2]
                wl_ref[:, cols] = both[:, SEL_K // 2:]

        xw = x_ref[...]
        x = jnp.concatenate([pltpu.bitcast(lax.shift_left(xw, jnp.uint32(16)), F32).astype(BF16),
                             pltpu.bitcast(xw & jnp.uint32(0xFFFF0000), F32).astype(BF16)], axis=1)
        glu = jnp.dot(x, wg_ref[...], preferred_element_type=F32) + bg_ref[0]
        lin = jnp.dot(x, wl_ref[...], preferred_element_type=F32) + bl_ref[0]
        glu = jnp.minimum(glu, SWIGLU_LIMIT)
        lin = jnp.clip(lin, -SWIGLU_LIMIT, SWIGLU_LIMIT)
        o_ref[...] = (glu * jax.nn.sigmoid(SWIGLU_ALPHA * glu) * (lin + 1.0)).astype(o_ref.dtype)


def _expert_up(tile_e, n_used, xs, w_up, b_up, layer):
    n_rows = xs.shape[0]
    pick = jnp.arange(SEL_K)[:, None] - 2 * jnp.arange(SEL_K // 2)[None, :]
    sel = jnp.concatenate([pick == 0, pick == 1], axis=1).astype(BF16)
    bu = b_up[layer].astype(F32)
    bg, bl = bu[:, None, 0::2], bu[:, None, 1::2]
    half_tn = UP_TN // 2
    bspec = pl.BlockSpec((1, 1, half_tn), lambda j, i, te, nu: (te[i], 0, j))
    return pl.pallas_call(
        _up_kernel,
        grid_spec=pltpu.PrefetchScalarGridSpec(
            num_scalar_prefetch=2, grid=(2 * D_FF // UP_TN, n_rows // E_TM),
            in_specs=[pl.BlockSpec((E_TM, D_MODEL // 2), lambda j, i, te, nu: (i, 0)),
                      pl.BlockSpec((None, 1, D_MODEL, UP_TN), lambda j, i, te, nu: (layer, te[i], 0, j)),
                      bspec, bspec,
                      pl.BlockSpec((SEL_K, SEL_K), lambda j, i, te, nu: (0, 0))],
            out_specs=pl.BlockSpec((E_TM, half_tn), lambda j, i, te, nu: (i, j)),
            scratch_shapes=[pltpu.VMEM((D_MODEL, half_tn), BF16), pltpu.VMEM((D_MODEL, half_tn), BF16)]),
        out_shape=jax.ShapeDtypeStruct((n_rows, D_FF), BF16),
        compiler_params=_cparams("arbitrary", "arbitrary"),
        name="moe_up",
    )(tile_e, n_used, xs, w_up, bg, bl, sel)


def _down_kernel(te_ref, nu_ref, a_ref, w_ref, b_ref, o_ref, wbf_ref):
    i = pl.program_id(1)

    @pl.when(i < nu_ref[0])
    def _():
        @pl.when(_weights_changed(te_ref, i))
        def _():
            wbf_ref[...] = w_ref[0].astype(BF16)

        o_ref[...] = jnp.dot(a_ref[...], wbf_ref[...], preferred_element_type=F32) + b_ref[0]


def _expert_down(tile_e, n_used, act, w_down, b_down, layer):
    n_rows = act.shape[0]
    return pl.pallas_call(
        _down_kernel,
        grid_spec=pltpu.PrefetchScalarGridSpec(
            num_scalar_prefetch=2, grid=(D_MODEL // DN_TN, n_rows // E_TM),
            in_specs=[pl.BlockSpec((E_TM, D_FF), lambda j, i, te, nu: (i, 0)),
                      pl.BlockSpec((None, 1, D_FF, DN_TN), lambda j, i, te, nu: (layer, te[i], 0, j)),
                      pl.BlockSpec((None, 1, 1, DN_TN), lambda j, i, te, nu: (layer, te[i], 0, j))],
            out_specs=pl.BlockSpec((E_TM, DN_TN), lambda j, i, te, nu: (i, j)),
            scratch_shapes=[pltpu.VMEM((D_FF, DN_TN), BF16)]),
        out_shape=jax.ShapeDtypeStruct((n_rows, D_MODEL), F32),
        compiler_params=_cparams("arbitrary", "arbitrary"),
        name="moe_down",
    )(tile_e, n_used, act, w_down, b_down.reshape(DEPTH, N_EXPERTS, 1, D_MODEL))


def _combine_kernel(pos_ref, ys_ref, tw_ref, x_ref, mod_ref, g_ref, *rest, split_blk):
    o_refs, (buf, sem) = rest[:-2], rest[-2:]

    def copy(a, token, k):
        return pltpu.make_async_copy(ys_ref.at[pl.ds(pos_ref[a], 1)], buf.at[k, pl.ds(token, 1)], sem)

    _row_dma_burst(copy, COMB_TT * TOP_K)
    tw = tw_ref[...]
    f = tw[:, 0:1] * buf[0]
    for k in range(1, TOP_K):
        f = f + tw[:, k:k + 1] * buf[k]
    m = mod_ref[0]
    fn = f * lax.rsqrt(jnp.mean(f * f, axis=-1, keepdims=True) + EPS) * g_ref[3:4]
    out = x_ref[...] + m[5:6] * fn
    if split_blk is None:
        o_refs[0][...] = out
    else:
        @pl.when(pl.program_id(0) < split_blk)
        def _():
            o_refs[0][...] = out

        @pl.when(pl.program_id(0) >= split_blk)
        def _():
            o_refs[1][...] = out


def _combine(pos_flat, ys, tw, x, modseg, gains, split_rows=None):
    t = x.shape[0]
    gains8 = jnp.pad(gains.astype(F32), ((0, 4), (0, 0)))
    if split_rows is None:
        split_blk = None
        out_specs = pl.BlockSpec((COMB_TT, D_MODEL), lambda i: (i, 0))
        out_shape = jax.ShapeDtypeStruct((t, D_MODEL), F32)
    else:
        split_blk = split_rows // COMB_TT
        assert split_rows % COMB_TT == 0 and 0 < split_blk < t // COMB_TT
        out_specs = [pl.BlockSpec((COMB_TT, D_MODEL), lambda i: (jnp.minimum(i, split_blk - 1), 0)),
                     pl.BlockSpec((COMB_TT, D_MODEL), lambda i: (jnp.maximum(i - split_blk, 0), 0))]
        out_shape = [jax.ShapeDtypeStruct((split_rows, D_MODEL), F32),
                     jax.ShapeDtypeStruct((t - split_rows, D_MODEL), F32)]
    return pl.pallas_call(
        functools.partial(_combine_kernel, split_blk=split_blk),
        grid=(t // COMB_TT,),
        in_specs=[pl.BlockSpec((COMB_TT * TOP_K,), lambda i: (i,), memory_space=pltpu.SMEM),
                  pl.BlockSpec(memory_space=pl.ANY),
                  pl.BlockSpec((COMB_TT, LANES), lambda i: (i, 0)),
                  pl.BlockSpec((COMB_TT, D_MODEL), lambda i: (i, 0)),
                  pl.BlockSpec((1, 8, D_MODEL), lambda i: (i * COMB_TT // SEG, 0, 0)),
                  pl.BlockSpec((8, D_MODEL), lambda i: (0, 0))],
        out_specs=out_specs,
        scratch_shapes=[pltpu.VMEM((TOP_K, COMB_TT, D_MODEL), F32), pltpu.SemaphoreType.DMA(())],
        out_shape=out_shape,
        compiler_params=_cparams("arbitrary"),
        name="moe_combine",
    )(pos_flat, ys, tw, x, modseg, gains8)


def _moe(h2p, logits, x, modseg, gains, w_up, b_up, w_down, b_down, layer, split_rows=None):
    t = h2p.shape[0]
    ti, tw, rk, cnt = _route(logits)
    counts = cnt[0, :N_EXPERTS].astype(I32)
    padded = (counts + E_TM - 1) // E_TM * E_TM
    upto = jnp.arange(N_EXPERTS)[None, :] <= jnp.arange(N_EXPERTS)[:, None]
    ends = jnp.sum(jnp.where(upto, padded[None, :], 0), axis=1).astype(I32)
    offsets = ends - padded
    n_tiles = t * TOP_K // E_TM + N_EXPERTS
    starts = jnp.arange(n_tiles, dtype=I32) * E_TM
    tile_e = jnp.minimum(jnp.sum((ends[None, :] <= starts[:, None]).astype(I32), axis=1), N_EXPERTS - 1)
    n_used = (ends[-1:] // E_TM).astype(I32)
    ti4, rk4 = ti[:, :TOP_K], rk[:, :TOP_K]
    first_row = jnp.sum(jnp.where(ti4[:, :, None] == jnp.arange(N_EXPERTS)[None, None, :], offsets[None, None, :], 0), axis=-1)
    pos_flat = (first_row + rk4).astype(I32).reshape(-1)
    xs = _dispatch(pos_flat, h2p, n_tiles * E_TM)
    act = _expert_up(tile_e, n_used, xs, w_up, b_up, layer)
    ys = _expert_down(tile_e, n_used, act, w_down, b_down, layer)
    return _combine(pos_flat, ys, tw, x, modseg, gains, split_rows)


def _forward(x, c8, seg_rows, lay, t5_bias, w_mod, b_mod, norm_gains, w_in, qk_norm_gains, na_rpb,
             ret_decay_logit, w_branch, w_out, w_router, b_router, w_up, b_up, w_down, b_down):
    mod = _modulation(c8, w_mod, b_mod)
    modseg = mod[:, seg_rows, :].reshape(DEPTH, len(seg_rows), 6, D_MODEL)
    modseg = jnp.pad(modseg, ((0, 0), (0, 0), (0, 2), (0, 0)))
    cos_a, sin_a = _axial_tables(lay)
    cos_r, sin_r = _rope_tables(lay)
    dil_bias = [_dil_bias(t5_bias, g, d, w // (2 * d)) for g, (w, d) in enumerate(DIL_CONFIGS)]
    for l in range(DEPTH):
        gains = norm_gains[l]
        proj = _norm_inproj(x, modseg[l], gains[0], w_in[l].astype(BF16))
        a_parts = [_dilated_group(proj, dil_bias[g], lay, g) for g in range(A_GROUPS)]
        o_b = _axial_gqa(proj, _b_prepare(proj, qk_norm_gains[l], cos_a, sin_a), lay)
        o_c = _neighbourhood(proj, na_rpb[l], lay)
        o_d = _retention(proj, ret_decay_logit[l], cos_r, sin_r, lay)
        merged = _branch_merge(proj, a_parts, o_b, o_c, o_d, w_branch[l].astype(BF16))
        x, h2p, logits = _outproj_residual(merged, x, modseg[l], gains, w_out[l].astype(BF16), w_router[l], b_router[l])
        x = _moe(h2p, logits, x, modseg[l], gains, w_up, b_up, w_down, b_down, l,
                 split_rows=lay.t_p if l == DEPTH - 1 else None)
    return x


def kernel(x_prompt, x_sample, c_prompt, c_sample, t5_bias, w_mod, b_mod, norm_gains, w_in, qk_norm_gains,
           na_rpb, ret_decay_logit, w_branch, w_out, w_router, b_router, w_up, b_up, w_down, b_down):
    n_p, s_p, _ = x_prompt.shape
    n_s, s_s, _ = x_sample.shape
    lay = Layout(n_p, s_p, n_s, s_s)
    assert s_p % SEG == 0 and s_s % SEG == 0 and n_p + n_s <= 8
    x = jnp.concatenate([x_prompt.reshape(-1, D_MODEL), x_sample.reshape(-1, D_MODEL)], axis=0)
    c8 = jnp.concatenate([c_prompt, c_sample, jnp.zeros((8 - n_p - n_s, D_MODEL), F32)], axis=0)
    seg_rows = tuple([b for b in range(n_p) for _ in range(s_p // SEG)]
                     + [n_p + b for b in range(n_s) for _ in range(s_s // SEG)])
    y_p, y_s = _forward(x, c8, jnp.asarray(seg_rows, dtype=I32), lay, t5_bias, w_mod, b_mod, norm_gains, w_in,
                        qk_norm_gains, na_rpb, ret_decay_logit, w_branch, w_out, w_router, b_router,
                        w_up, b_up, w_down, b_down)
    return (y_p.reshape(n_p, s_p, D_MODEL), y_s.reshape(n_s, s_s, D_MODEL))
```

```python
import functools
import math
from typing import NamedTuple

import jax
import jax.numpy as jnp
from jax import lax
from jax.experimental import pallas as pl
from jax.experimental.pallas import tpu as pltpu

F32 = jnp.float32
BF16 = jnp.bfloat16
I32 = jnp.int32

D_MODEL = 2048
DEPTH = 2
HEAD_DIM = 128
GRID_W = 64
EPS = 1e-6
ROPE_THETA = 10000.0
NEG_INF = -1e30
LOG2E = 1.4426950408889634

DIL_CONFIGS = ((128, 1), (512, 4), (2048, 16))
A_GROUPS = 3
A_HEADS_PER_GROUP = 6
A_HEADS = A_GROUPS * A_HEADS_PER_GROUP
T5_BUCKETS = 32
T5_MAX_DIST = 1024
B_Q_HEADS = 6
B_KV_HEADS = 2
C_HEADS = 6
NA_ROWS = 8
NA_COLS = 16
D_HEADS = 4
D_QK = 128
D_V = 256
N_EXPERTS = 32
TOP_K = 4
D_FF = 2048
SWIGLU_LIMIT = 7.0
SWIGLU_ALPHA = 1.702
N_BRANCH = 4

A_OUT = A_HEADS_PER_GROUP * HEAD_DIM
B_OUT = B_Q_HEADS * HEAD_DIM
C_OUT = C_HEADS * HEAD_DIM
D_OUT = D_HEADS * D_V
MIX_WIDTH = A_OUT + B_OUT + C_OUT + D_OUT

IN_SPLITS = (A_HEADS * HEAD_DIM, A_HEADS * HEAD_DIM, A_HEADS * HEAD_DIM,
             B_Q_HEADS * HEAD_DIM, B_KV_HEADS * HEAD_DIM, B_KV_HEADS * HEAD_DIM,
             C_HEADS * HEAD_DIM, C_HEADS * HEAD_DIM, C_HEADS * HEAD_DIM,
             D_HEADS * D_QK, D_HEADS * D_QK, D_HEADS * D_V, D_HEADS * D_V,
             N_BRANCH * D_MODEL)
N_IN = sum(IN_SPLITS)
_OFF = [0]
for _w in IN_SPLITS:
    _OFF.append(_OFF[-1] + _w)
(OFF_AQ, OFF_AK, OFF_AV, OFF_BQ, OFF_BK, OFF_BV, OFF_CQ, OFF_CK, OFF_CV,
 OFF_DQ, OFF_DK, OFF_DV, OFF_DG, OFF_GATE) = _OFF[:-1]

SEG = 2048
LANES = 128
VMEM_LIMIT = 56 * 1024 * 1024


class Layout(NamedTuple):
    n_p: int
    s_p: int
    n_s: int
    s_s: int

    @property
    def t_p(self):
        return self.n_p * self.s_p

    @property
    def t(self):
        return self.n_p * self.s_p + self.n_s * self.s_s


def _cparams(*sem):
    return pltpu.CompilerParams(dimension_semantics=sem, vmem_limit_bytes=VMEM_LIMIT)


def _mod_kernel(c_ref, w_ref, b_ref, o_ref):
    c = c_ref[...]
    s = (c * jax.nn.sigmoid(c)).astype(BF16)
    o_ref[0] = jnp.dot(s, w_ref[0].astype(BF16), preferred_element_type=F32) + b_ref[0]


def _modulation(c8, w_mod, b_mod):
    tn = 1024
    n6 = 6 * D_MODEL
    return pl.pallas_call(
        _mod_kernel,
        grid=(DEPTH, n6 // tn),
        in_specs=[pl.BlockSpec((8, D_MODEL), lambda l, j: (0, 0)),
                  pl.BlockSpec((1, D_MODEL, tn), lambda l, j: (l, 0, j)),
                  pl.BlockSpec((1, 1, tn), lambda l, j: (l, 0, j))],
        out_specs=pl.BlockSpec((1, 8, tn), lambda l, j: (l, 0, j)),
        out_shape=jax.ShapeDtypeStruct((DEPTH, 8, n6), F32),
        compiler_params=_cparams("parallel", "parallel"),
        name="adaln_mod",
    )(c8, w_mod, b_mod.reshape(DEPTH, 1, n6))


def _inproj_kernel(x_ref, mod_ref, g_ref, w_ref, o_ref, h_ref):
    @pl.when(pl.program_id(1) == 0)
    def _():
        x = x_ref[...]
        y = x * lax.rsqrt(jnp.mean(x * x, axis=-1, keepdims=True) + EPS) * g_ref[...]
        m = mod_ref[0]
        h_ref[...] = (y * (1.0 + m[1:2]) + m[0:1]).astype(BF16)

    o_ref[...] = jnp.dot(h_ref[...], w_ref[...], preferred_element_type=F32).astype(o_ref.dtype)


def _norm_inproj(x, modseg, gain, w_in_bf16):
    t = x.shape[0]
    tm, tn = 1024, 1280
    assert t % tm == 0 and N_IN % tn == 0 and SEG % tm == 0
    return pl.pallas_call(
        _inproj_kernel,
        grid=(t // tm, N_IN // tn),
        in_specs=[pl.BlockSpec((tm, D_MODEL), lambda i, j: (i, 0)),
                  pl.BlockSpec((1, 8, D_MODEL), lambda i, j: (i * tm // SEG, 0, 0)),
                  pl.BlockSpec((1, D_MODEL), lambda i, j: (0, 0)),
                  pl.BlockSpec((D_MODEL, tn), lambda i, j: (0, j))],
        out_specs=pl.BlockSpec((tm, tn), lambda i, j: (i, j)),
        out_shape=jax.ShapeDtypeStruct((t, N_IN), BF16),
        scratch_shapes=[pltpu.VMEM((tm, D_MODEL), BF16)],
        compiler_params=_cparams("parallel", "arbitrary"),
        name="norm_inproj",
    )(x, modseg, gain.reshape(1, D_MODEL), w_in_bf16)


def _local_pos(lay):
    return jnp.concatenate([jnp.tile(jnp.arange(lay.s_p), lay.n_p), jnp.tile(jnp.arange(lay.s_s), lay.n_s)])


def _axial_tables(lay):
    pos = _local_pos(lay)
    lane = jnp.arange(LANES)
    quarter = HEAD_DIM // 4
    freqs = ROPE_THETA ** (-jnp.arange(quarter, dtype=F32) / quarter)
    f = freqs[lane % quarter]
    p = jnp.where(lane[None, :] < HEAD_DIM // 2, (pos // GRID_W)[:, None], (pos % GRID_W)[:, None]).astype(F32)
    ang = p * f[None, :]
    sign = jnp.where((lane % (2 * quarter)) < quarter, -1.0, 1.0).astype(F32)
    return jnp.cos(ang), jnp.sin(ang) * sign[None, :]


def _rope_tables(lay):
    pos = _local_pos(lay)
    lane = jnp.arange(LANES)
    half = D_QK // 2
    freqs = ROPE_THETA ** (-jnp.arange(half, dtype=F32) / half)
    ang = pos.astype(F32)[:, None] * freqs[lane % half][None, :]
    sign = jnp.where(lane < half, -1.0, 1.0).astype(F32)
    return jnp.cos(ang), jnp.sin(ang) * sign[None, :]


def _lane_iota(shape):
    return lax.broadcasted_iota(I32, shape, len(shape) - 1)


def _axial_rotate(x, cos, sin_signed):
    q = HEAD_DIM // 4
    lo = (_lane_iota(x.shape) % (2 * q)) < q
    partner = jnp.where(lo, pltpu.roll(x, LANES - q, 1), pltpu.roll(x, q, 1))
    return x * cos + partner * sin_signed


def _rope_rotate(x, cos, sin_signed):
    return x * cos + pltpu.roll(x, D_QK // 2, 1) * sin_signed


def _bprep_kernel(x_ref, g_ref, cos_ref, sin_ref, o_ref):
    x = x_ref[...].astype(F32)
    y = x * lax.rsqrt(jnp.mean(x * x, axis=-1, keepdims=True) + EPS) * g_ref[0]
    y = _axial_rotate(y, cos_ref[...], sin_ref[...])
    scale = jnp.where(pl.program_id(1) < B_Q_HEADS, HEAD_DIM ** -0.5 * LOG2E, 1.0)
    o_ref[...] = (y * scale).astype(o_ref.dtype)


def _b_prepare(proj, qk_g, cos, sin):
    t = proj.shape[0]
    tr = 1024
    nh = B_Q_HEADS + B_KV_HEADS
    g8 = jnp.concatenate([jnp.tile(qk_g[0:1], (B_Q_HEADS, 1)), jnp.tile(qk_g[1:2], (B_KV_HEADS, 1))]).reshape(nh, 1, HEAD_DIM)
    return pl.pallas_call(
        _bprep_kernel,
        grid=(t // tr, nh),
        in_specs=[pl.BlockSpec((tr, HEAD_DIM), lambda i, h: (i, OFF_BQ // HEAD_DIM + h)),
                  pl.BlockSpec((1, 1, HEAD_DIM), lambda i, h: (h, 0, 0)),
                  pl.BlockSpec((tr, HEAD_DIM), lambda i, h: (i, 0)),
                  pl.BlockSpec((tr, HEAD_DIM), lambda i, h: (i, 0))],
        out_specs=pl.BlockSpec((tr, HEAD_DIM), lambda i, h: (i, h)),
        out_shape=jax.ShapeDtypeStruct((t, nh * HEAD_DIM), BF16),
        compiler_params=_cparams("parallel", "parallel"),
        name="b_prep",
    )(proj, g8, cos, sin)


FLASH_ROW_SPLIT = 2


def _flash_kernel(qt_ref, kt_ref, first_ref, last_ref, q_ref, k_ref, v_ref, o_ref, m_ref, acc_ref):
    s_id = pl.program_id(1)
    rep = B_Q_HEADS // B_KV_HEADS

    @pl.when(first_ref[s_id] == 1)
    def _():
        m_ref[...] = jnp.full(m_ref.shape, -jnp.inf, F32)
        acc_ref[...] = jnp.zeros(acc_ref.shape, F32)

    k = k_ref[...]
    v = v_ref[...]
    rb = q_ref.shape[0] // FLASH_ROW_SPLIT
    units = [(r, slice(u * rb, (u + 1) * rb)) for r in range(rep) for u in range(FLASH_ROW_SPLIT)]
    scores = [lax.dot_general(q_ref[rows, r * HEAD_DIM:(r + 1) * HEAD_DIM], k, (((1,), (1,)), ((), ())),
                              preferred_element_type=F32) for (r, rows) in units]
    probs, alphas = [], []
    for (r, rows), s in zip(units, scores):
        m_prev = m_ref[r, rows, :]
        m_cur = jnp.maximum(m_prev, jnp.max(s, axis=-1, keepdims=True))
        alphas.append(jnp.exp2(m_prev - m_cur))
        probs.append(jnp.exp2(s - m_cur[:, 0:1]).astype(BF16))
        m_ref[r, rows, :] = m_cur
    for (r, rows), p, alpha in zip(units, probs, alphas):
        pv = jnp.dot(p, v, preferred_element_type=F32)
        acc_ref[r, rows, :] = jnp.concatenate([alpha, alpha], axis=1) * acc_ref[r, rows, :] + pv

    @pl.when(last_ref[s_id] == 1)
    def _():
        for r in range(rep):
            acc = acc_ref[r]
            o_ref[:, r * HEAD_DIM:(r + 1) * HEAD_DIM] = (acc[:, :HEAD_DIM] / acc[:, HEAD_DIM:]).astype(o_ref.dtype)


def _flash_tables(lay, tq, tk):
    qt, kt, first, last = [], [], [], []
    for (n, s, base) in ((lay.n_p, lay.s_p, 0), (lay.n_s, lay.s_s, lay.t_p)):
        for b in range(n):
            for qi in range(s // tq):
                nk = s // tk
                for ki in range(nk):
                    qt.append((base + b * s) // tq + qi)
                    kt.append((base + b * s) // tk + ki)
                    first.append(int(ki == 0))
                    last.append(int(ki == nk - 1))
    mk = lambda a: jnp.asarray(a, dtype=I32)
    return mk(qt), mk(kt), mk(first), mk(last)


def _axial_gqa(proj, bprep, lay):
    t = proj.shape[0]
    tq, tk = 512, 2048
    assert lay.s_p % tk == 0 and lay.s_s % tk == 0
    rep = B_Q_HEADS // B_KV_HEADS
    qt, kt, first, last = _flash_tables(lay, tq, tk)
    n_steps = qt.shape[0]
    v = proj[:, OFF_BV:OFF_BV + B_KV_HEADS * HEAD_DIM].reshape(t, B_KV_HEADS, HEAD_DIM)
    v_ones = jnp.concatenate([v, jnp.ones_like(v)], axis=-1).reshape(t, B_KV_HEADS * 2 * HEAD_DIM)
    gs = pltpu.PrefetchScalarGridSpec(
        num_scalar_prefetch=4,
        grid=(B_KV_HEADS, n_steps),
        in_specs=[pl.BlockSpec((tq, rep * HEAD_DIM), lambda g, s, qt, kt, f, l: (qt[s], g)),
                  pl.BlockSpec((tk, HEAD_DIM), lambda g, s, qt, kt, f, l: (kt[s], B_Q_HEADS + g)),
                  pl.BlockSpec((tk, 2 * HEAD_DIM), lambda g, s, qt, kt, f, l: (kt[s], g))],
        out_specs=pl.BlockSpec((tq, rep * HEAD_DIM), lambda g, s, qt, kt, f, l: (qt[s], g)),
        scratch_shapes=[pltpu.VMEM((rep, tq, LANES), F32), pltpu.VMEM((rep, tq, 2 * HEAD_DIM), F32)],
    )
    return pl.pallas_call(
        _flash_kernel,
        grid_spec=gs,
        out_shape=jax.ShapeDtypeStruct((t, B_OUT), BF16),
        compiler_params=_cparams("parallel", "arbitrary"),
        name="b_flash",
    )(qt, kt, first, last, bprep, bprep, v_ones)


A_BQ = 128
A_NSUB = 2


def _t5_bucket(rel):
    nb = T5_BUCKETS // 2
    max_exact = nb // 2
    n = jnp.abs(rel)
    large = max_exact + (jnp.log(jnp.maximum(n, 1).astype(F32) / max_exact)
                         / math.log(T5_MAX_DIST / max_exact) * (nb - max_exact)).astype(I32)
    large = jnp.minimum(large, nb - 1)
    return jnp.where(rel > 0, nb, 0) + jnp.where(n < max_exact, n, large)


def _dil_bias(t5_bias, g, d, half):
    rel = (jnp.arange(3 * A_BQ)[None, :] - A_BQ) - jnp.arange(A_BQ)[:, None]
    tab = t5_bias[:, g * A_HEADS_PER_GROUP:(g + 1) * A_HEADS_PER_GROUP].astype(F32)
    onehot = (_t5_bucket(rel * d)[:, :, None] == jnp.arange(T5_BUCKETS)[None, None, :]).astype(F32)
    bias = jnp.einsum("qkb,bh->hqk", onehot, tab, precision=lax.Precision.HIGHEST)
    return jnp.where((jnp.abs(rel) <= half)[None], bias, NEG_INF)


def _dil_kernel(q_ref, kp_ref, kc_ref, kn_ref, vp_ref, vc_ref, vn_ref, b_ref, o_ref, lse_ref, *, nblk_p, tblk_p, nblk_s):
    scale = HEAD_DIM ** -0.5
    dn = (((1,), (1,)), ((), ()))
    units = []
    for s in range(A_NSUB):
        r = pl.program_id(1) * A_NSUB + s
        in_p = r < tblk_p
        nblk = jnp.where(in_p, nblk_p, nblk_s)
        il = jnp.where(in_p, r, r - tblk_p) % nblk
        prev_ok = il > 0
        next_ok = il < nblk - 1
        rows = slice(s * A_BQ, (s + 1) * A_BQ)
        before = (kp_ref, vp_ref, slice((A_NSUB - 1) * A_BQ, A_NSUB * A_BQ)) if s == 0 else \
            (kc_ref, vc_ref, slice((s - 1) * A_BQ, s * A_BQ))
        after = (kn_ref, vn_ref, slice(0, A_BQ)) if s == A_NSUB - 1 else \
            (kc_ref, vc_ref, slice((s + 1) * A_BQ, (s + 2) * A_BQ))
        for h in range(A_HEADS_PER_GROUP):
            sl = slice(h * HEAD_DIM, (h + 1) * HEAD_DIM)
            q = q_ref[rows, sl]
            b = b_ref[h]
            sp = lax.dot_general(q, before[0][before[2], sl], dn, preferred_element_type=F32) * scale + b[:, 0:A_BQ]
            sc = lax.dot_general(q, kc_ref[rows, sl], dn, preferred_element_type=F32) * scale + b[:, A_BQ:2 * A_BQ]
            sn = lax.dot_general(q, after[0][after[2], sl], dn, preferred_element_type=F32) * scale + b[:, 2 * A_BQ:]
            sp = jnp.where(prev_ok, sp, NEG_INF)
            sn = jnp.where(next_ok, sn, NEG_INF)
            units.append((rows, sl, before, after, sp, sc, sn))
    soft = []
    for (_, _, _, _, sp, sc, sn) in units:
        m = jnp.maximum(jnp.maximum(jnp.max(sp, axis=-1, keepdims=True), jnp.max(sc, axis=-1, keepdims=True)),
                        jnp.max(sn, axis=-1, keepdims=True))
        pp, pc, pn = jnp.exp(sp - m), jnp.exp(sc - m), jnp.exp(sn - m)
        l = (jnp.sum(pp, axis=-1, keepdims=True) + jnp.sum(pc, axis=-1, keepdims=True)
             + jnp.sum(pn, axis=-1, keepdims=True))
        soft.append((m, l, pp, pc, pn))
    for (rows, sl, before, after, _, _, _), (m, l, pp, pc, pn) in zip(units, soft):
        o = (jnp.dot(pp.astype(BF16), before[1][before[2], sl], preferred_element_type=F32)
             + jnp.dot(pc.astype(BF16), vc_ref[rows, sl], preferred_element_type=F32)
             + jnp.dot(pn.astype(BF16), after[1][after[2], sl], preferred_element_type=F32))
        o_ref[rows, sl] = (o / l).astype(o_ref.dtype)
        lse_ref[rows, sl] = jnp.broadcast_to(m + jnp.log(l), (A_BQ, HEAD_DIM))


def _dilated_group(proj, bias, lay, g):
    w, d = DIL_CONFIGS[g]
    t = proj.shape[0]
    rows = t // d
    nblk_p = lay.s_p // d // A_BQ
    nblk_s = lay.s_s // d // A_BQ
    assert nblk_p >= 1 and nblk_s >= 1 and w // (2 * d) <= A_BQ
    tblk_p = lay.t_p // d // A_BQ
    br = A_BQ * A_NSUB
    assert rows % br == 0
    tblk = rows // br
    gw = A_HEADS_PER_GROUP * HEAD_DIM

    def spec(shift):
        return pl.BlockSpec((br, gw), lambda c, i: (jnp.clip(i + shift, 0, tblk - 1), c))

    if d == 1:
        def win(off, shift):
            return pl.BlockSpec(
                (pl.Element(br), pl.Element(gw)),
                lambda c, i: (pl.multiple_of(jnp.clip(i + shift, 0, tblk - 1) * br, br), off + g * gw))
        q_c = k_c = v_c = proj
        qkv_specs = [win(OFF_AQ, 0), win(OFF_AK, -1), win(OFF_AK, 0), win(OFF_AK, 1),
                     win(OFF_AV, -1), win(OFF_AV, 0), win(OFF_AV, 1)]
    else:
        q_c, k_c, v_c = [proj[:, off + g * gw:off + (g + 1) * gw].reshape(rows, d * gw)
                         for off in (OFF_AQ, OFF_AK, OFF_AV)]
        qkv_specs = [spec(0), spec(-1), spec(0), spec(1), spec(-1), spec(0), spec(1)]

    kern = functools.partial(_dil_kernel, nblk_p=nblk_p, tblk_p=tblk_p, nblk_s=nblk_s)
    o, lse = pl.pallas_call(
        kern,
        grid=(d, tblk),
        in_specs=qkv_specs + [pl.BlockSpec((A_HEADS_PER_GROUP, A_BQ, 3 * A_BQ), lambda c, i: (0, 0, 0))],
        out_specs=[spec(0), spec(0)],
        out_shape=[jax.ShapeDtypeStruct((rows, d * gw), BF16), jax.ShapeDtypeStruct((rows, d * gw), F32)],
        compiler_params=_cparams("parallel", "parallel"),
        name=f"a_dilated_g{g}",
    )(q_c, k_c, k_c, k_c, v_c, v_c, v_c, bias)
    return o.reshape(t, gw), lse.reshape(t, gw)


C_QROWS = 8
C_KROWS = 2 * NA_ROWS
C_TQ = C_QROWS * GRID_W
C_TK = C_KROWS * GRID_W


def _na_bias(rpb):
    hi = lax.Precision.HIGHEST
    qc = jnp.arange(GRID_W)[:, None]
    kc = jnp.arange(GRID_W)[None, :]
    cstart = jnp.clip(qc - NA_COLS // 2, 0, GRID_W - NA_COLS)
    col_ok = (kc >= cstart) & (kc < cstart + NA_COLS)
    ci = jnp.clip(kc - qc, -(NA_COLS - 1), NA_COLS - 1) + NA_COLS - 1
    oh_c = (ci[:, :, None] == jnp.arange(2 * NA_COLS - 1)[None, None, :]).astype(F32)
    by_col = jnp.einsum("hrc,abc->hrab", rpb.astype(F32), oh_c, precision=hi)
    out = []
    for off in (0, NA_ROWS // 2, NA_ROWS):
        qr = (off + jnp.arange(C_QROWS))[:, None]
        kr = jnp.arange(C_KROWS)[None, :]
        rstart = jnp.clip(qr - NA_ROWS // 2, 0, C_KROWS - NA_ROWS)
        row_ok = (kr >= rstart) & (kr < rstart + NA_ROWS)
        ri = jnp.clip(kr - qr + NA_ROWS - 1, 0, 2 * NA_ROWS - 2)
        oh_r = (ri[:, :, None] == jnp.arange(2 * NA_ROWS - 1)[None, None, :]).astype(F32)
        b = jnp.einsum("qkr,hrab->hqakb", oh_r, by_col, precision=hi)
        ok = row_ok[:, None, :, None] & col_ok[None, :, None, :]
        out.append(jnp.where(ok[None], b, NEG_INF).reshape(C_HEADS, C_TQ, C_TK))
    return jnp.stack(out)


C_ROW_SPLIT = 2


def _na_kernel(q_ref, k_ref, v_ref, b_ref, o_ref):
    k = k_ref[...]
    v = v_ref[...]
    rb = C_TQ // C_ROW_SPLIT
    units = [slice(u * rb, (u + 1) * rb) for u in range(C_ROW_SPLIT)]
    scores = [lax.dot_general(q_ref[rows, :], k, (((1,), (1,)), ((), ())), preferred_element_type=F32)
              * (HEAD_DIM ** -0.5) + b_ref[rows, :] for rows in units]
    soft = []
    for s in scores:
        m = jnp.max(s, axis=-1, keepdims=True)
        p = jnp.exp(s - m)
        soft.append((p.astype(BF16), jnp.sum(p, axis=-1, keepdims=True)))
    for rows, (p, l) in zip(units, soft):
        o_ref[rows, :] = (jnp.dot(p, v, preferred_element_type=F32) / l).astype(o_ref.dtype)


def _neighbourhood(proj, rpb, lay):
    t = proj.shape[0]
    r_p, r_s = lay.s_p // GRID_W, lay.s_s // GRID_W
    assert r_p >= C_KROWS and r_s >= C_KROWS and r_p % C_QROWS == 0 and r_s % C_QROWS == 0
    blk_p = lay.t_p // C_TQ
    bias = _na_bias(rpb)

    def window(i):
        in_p = i < blk_p
        per_seq = jnp.where(in_p, lay.s_p // C_TQ, lay.s_s // C_TQ)
        rows = jnp.where(in_p, r_p, r_s)
        il = jnp.where(in_p, i, i - blk_p)
        seq0 = (i - il % per_seq) * C_TQ
        r0 = (il % per_seq) * C_QROWS
        w0 = jnp.clip(r0 - NA_ROWS // 2, 0, rows - C_KROWS)
        return seq0 + w0 * GRID_W, (r0 - w0) // (NA_ROWS // 2)

    def kv_spec(off):
        return pl.BlockSpec((pl.Element(C_TK), pl.Element(HEAD_DIM)),
                            lambda h, i: (pl.multiple_of(window(i)[0], GRID_W),
                                          pl.multiple_of(off + h * HEAD_DIM, LANES)))

    return pl.pallas_call(
        _na_kernel,
        grid=(C_HEADS, t // C_TQ),
        in_specs=[pl.BlockSpec((C_TQ, HEAD_DIM), lambda h, i: (i, OFF_CQ // HEAD_DIM + h)),
                  kv_spec(OFF_CK), kv_spec(OFF_CV),
                  pl.BlockSpec((None, None, C_TQ, C_TK), lambda h, i: (window(i)[1], h, 0, 0))],
        out_specs=pl.BlockSpec((C_TQ, HEAD_DIM), lambda h, i: (i, h)),
        out_shape=jax.ShapeDtypeStruct((t, C_OUT), BF16),
        compiler_params=_cparams("parallel", "parallel"),
        name="c_neighbourhood",
    )(proj, proj, proj, bias)


RET_C = 256


def _log_sigmoid(x):
    return jnp.minimum(x, 0.0) - jnp.log(1.0 + jnp.exp(-jnp.abs(x)))


def _ret_qk(q_ref, k_ref, cos_ref, sin_ref, h):
    sl = slice(h * D_QK, (h + 1) * D_QK)
    cos, sin = cos_ref[...], sin_ref[...]
    q = _rope_rotate(q_ref[:, sl].astype(F32), cos, sin)
    k = _rope_rotate(k_ref[:, sl].astype(F32), cos, sin) * (D_QK ** -0.5)
    return q, k


def _ret_bwd_kernel(cb_ref, first_ref, q_ref, k_ref, v_ref, cos_ref, sin_ref, dl_ref, o_ref, st_ref):
    s_id = pl.program_id(0)

    @pl.when(first_ref[s_id] == 1)
    def _():
        st_ref[...] = jnp.zeros(st_ref.shape, F32)

    row = lax.broadcasted_iota(I32, (RET_C, 1), 0).astype(F32)
    for h in range(D_HEADS):
        lg = _log_sigmoid(dl_ref[D_HEADS + h:D_HEADS + h + 1, :])
        q, k = _ret_qk(q_ref, k_ref, cos_ref, sin_ref, h)
        v = v_ref[:, h * D_V:(h + 1) * D_V]
        st = st_ref[h]
        q_dec = (q * jnp.exp((RET_C - row) * lg)).astype(BF16)
        o_ref[:, h * D_V:(h + 1) * D_V] = jnp.dot(q_dec, st.astype(BF16), preferred_element_type=F32)
        k_dec = (k * jnp.exp(row * lg)).astype(BF16)
        kv = lax.dot_general(k_dec, v, (((0,), (0,)), ((), ())), preferred_element_type=F32)
        st_ref[h] = st * jnp.exp(RET_C * lg[:, 0:1]) + kv


def _ret_fwd_kernel(cb_ref, first_ref, q_ref, k_ref, v_ref, g_ref, xb_ref, cos_ref, sin_ref, dl_ref, o_ref, st_ref):
    s_id = pl.program_id(0)

    @pl.when(first_ref[s_id] == 1)
    def _():
        st_ref[...] = jnp.zeros(st_ref.shape, F32)

    row = lax.broadcasted_iota(I32, (RET_C, 1), 0).astype(F32)
    diff = (lax.broadcasted_iota(I32, (RET_C, RET_C), 0) - lax.broadcasted_iota(I32, (RET_C, RET_C), 1)).astype(F32)
    for h in range(D_HEADS):
        lgf = _log_sigmoid(dl_ref[h:h + 1, :])
        lgb = _log_sigmoid(dl_ref[D_HEADS + h:D_HEADS + h + 1, :])
        q, k = _ret_qk(q_ref, k_ref, cos_ref, sin_ref, h)
        v = v_ref[:, h * D_V:(h + 1) * D_V]
        st = st_ref[h]
        dmat = jnp.where(diff >= 0, jnp.exp(jnp.maximum(diff, 0.0) * lgf[:, 0:1]),
                         jnp.exp(jnp.maximum(-diff, 0.0) * lgb[:, 0:1]))
        s = lax.dot_general(q.astype(BF16), k.astype(BF16), (((1,), (1,)), ((), ())), preferred_element_type=F32)
        o = jnp.dot((s * dmat).astype(BF16), v, preferred_element_type=F32)
        q_dec = (q * jnp.exp((row + 1.0) * lgf)).astype(BF16)
        o = o + jnp.dot(q_dec, st.astype(BF16), preferred_element_type=F32)
        o = o + xb_ref[:, h * D_V:(h + 1) * D_V]
        k_dec = (k * jnp.exp((RET_C - 1.0 - row) * lgf)).astype(BF16)
        kv = lax.dot_general(k_dec, v, (((0,), (0,)), ((), ())), preferred_element_type=F32)
        st_ref[h] = st * jnp.exp(RET_C * lgf[:, 0:1]) + kv
        mu = jnp.mean(o, axis=-1, keepdims=True)
        var = jnp.mean(jnp.square(o - mu), axis=-1, keepdims=True)
        on = (o - mu) * lax.rsqrt(var + EPS)
        g = g_ref[:, h * D_V:(h + 1) * D_V].astype(F32)
        o_ref[:, h * D_V:(h + 1) * D_V] = (g * jax.nn.sigmoid(g) * on).astype(o_ref.dtype)


def _ret_tables(lay, reverse):
    cb, first = [], []
    for (n, s, base) in ((lay.n_p, lay.s_p, 0), (lay.n_s, lay.s_s, lay.t_p)):
        for b in range(n):
            nc = s // RET_C
            order = range(nc - 1, -1, -1) if reverse else range(nc)
            for j, c in enumerate(order):
                cb.append((base + b * s) // RET_C + c)
                first.append(int(j == 0))
    return jnp.asarray(cb, dtype=I32), jnp.asarray(first, dtype=I32)


def _retention(proj, dlogit, cos, sin, lay):
    t = proj.shape[0]
    dl = jnp.broadcast_to(dlogit.astype(F32).reshape(2 * D_HEADS, 1), (2 * D_HEADS, LANES))
    qw, vw = D_HEADS * D_QK, D_HEADS * D_V
    row_spec = lambda width, off: pl.BlockSpec((pl.Element(RET_C), pl.Element(width)),
                                               lambda s, cb, f: (pl.multiple_of(cb[s] * RET_C, RET_C), off))
    tab_spec = pl.BlockSpec((RET_C, LANES), lambda s, cb, f: (cb[s], 0))
    dl_spec = pl.BlockSpec((2 * D_HEADS, LANES), lambda s, cb, f: (0, 0))
    out_spec = pl.BlockSpec((RET_C, vw), lambda s, cb, f: (cb[s], 0))
    state = pltpu.VMEM((D_HEADS, D_QK, D_V), F32)

    cb, first = _ret_tables(lay, True)
    xb = pl.pallas_call(
        _ret_bwd_kernel,
        grid_spec=pltpu.PrefetchScalarGridSpec(
            num_scalar_prefetch=2, grid=(cb.shape[0],),
            in_specs=[row_spec(qw, OFF_DQ), row_spec(qw, OFF_DK), row_spec(vw, OFF_DV), tab_spec, tab_spec, dl_spec],
            out_specs=out_spec, scratch_shapes=[state]),
        out_shape=jax.ShapeDtypeStruct((t, vw), F32),
        compiler_params=_cparams("arbitrary"),
        name="d_retention_bwd",
    )(cb, first, proj, proj, proj, cos, sin, dl)

    cb, first = _ret_tables(lay, False)
    return pl.pallas_call(
        _ret_fwd_kernel,
        grid_spec=pltpu.PrefetchScalarGridSpec(
            num_scalar_prefetch=2, grid=(cb.shape[0],),
            in_specs=[row_spec(qw, OFF_DQ), row_spec(qw, OFF_DK), row_spec(vw, OFF_DV), row_spec(vw, OFF_DG),
                      out_spec, tab_spec, tab_spec, dl_spec],
            out_specs=out_spec, scratch_shapes=[state]),
        out_shape=jax.ShapeDtypeStruct((t, vw), BF16),
        compiler_params=_cparams("arbitrary"),
        name="d_retention_fwd",
    )(cb, first, proj, proj, proj, proj, xb, cos, sin, dl)


def _merge_kernel(oa0, oa1, oa2, ls0, ls1, ls2, ob, oc, od, g0, g1, g2, g3, wa, wb, wc, wd, o_ref, oa_ref):
    @pl.when(pl.program_id(1) == 0)
    def _():
        l0, l1, l2 = ls0[...], ls1[...], ls2[...]
        m = jnp.maximum(jnp.maximum(l0, l1), l2)
        e0, e1, e2 = jnp.exp(l0 - m), jnp.exp(l1 - m), jnp.exp(l2 - m)
        num = e0 * oa0[...].astype(F32) + e1 * oa1[...].astype(F32) + e2 * oa2[...].astype(F32)
        oa_ref[...] = (num / (e0 + e1 + e2)).astype(BF16)

    def term(gate, o, w):
        return jax.nn.sigmoid(gate[...].astype(F32)) * jnp.dot(o, w[...], preferred_element_type=F32)

    acc = term(g0, oa_ref[...], wa) + term(g1, ob[...], wb) + term(g2, oc[...], wc) + term(g3, od[...], wd)
    o_ref[...] = acc.astype(o_ref.dtype)


def _branch_merge(proj, a_parts, o_b, o_c, o_d, w_branch_bf16):
    t = proj.shape[0]
    tm, tn = 512, 1024
    (oa0, ls0), (oa1, ls1), (oa2, ls2) = a_parts
    row = lambda width: pl.BlockSpec((tm, width), lambda i, j: (i, 0))
    gate = lambda b: pl.BlockSpec((pl.Element(tm), pl.Element(tn)),
                                  lambda i, j: (pl.multiple_of(i * tm, tm),
                                                pl.multiple_of(OFF_GATE + b * D_MODEL + j * tn, LANES)))
    wspec = lambda width: pl.BlockSpec((width, tn), lambda i, j: (0, j))
    offs = (0, A_OUT, A_OUT + B_OUT, A_OUT + B_OUT + C_OUT, MIX_WIDTH)
    ws = [w_branch_bf16[offs[b]:offs[b + 1]] for b in range(N_BRANCH)]
    return pl.pallas_call(
        _merge_kernel,
        grid=(t // tm, D_MODEL // tn),
        in_specs=[row(A_OUT)] * 6 + [row(B_OUT), row(C_OUT), row(D_OUT)]
                 + [gate(0), gate(1), gate(2), gate(3)]
                 + [wspec(A_OUT), wspec(B_OUT), wspec(C_OUT), wspec(D_OUT)],
        out_specs=pl.BlockSpec((tm, tn), lambda i, j: (i, j)),
        out_shape=jax.ShapeDtypeStruct((t, D_MODEL), BF16),
        scratch_shapes=[pltpu.VMEM((tm, A_OUT), BF16)],
        compiler_params=_cparams("parallel", "arbitrary"),
        name="branch_merge",
    )(oa0, oa1, oa2, ls0, ls1, ls2, o_b, o_c, o_d, proj, proj, proj, proj, *ws)


def _split3(x):
    hi = x.astype(BF16)
    lo = (x - hi.astype(F32)).astype(BF16)
    return hi, lo


def _outproj_kernel(mg_ref, x_ref, mod_ref, g_ref, w_ref, wr_hi, wr_lo, br_ref, xo_ref, h2_ref, lg_ref):
    y = jnp.dot(mg_ref[...], w_ref[...], preferred_element_type=F32)
    m = mod_ref[0]
    g = g_ref[...]
    yn = y * lax.rsqrt(jnp.mean(y * y, axis=-1, keepdims=True) + EPS) * g[1:2]
    x = x_ref[...] + m[2:3] * yn
    xo_ref[...] = x
    h2 = x * lax.rsqrt(jnp.mean(x * x, axis=-1, keepdims=True) + EPS) * g[2:3] * (1.0 + m[4:5]) + m[3:4]
    h2b = h2.astype(BF16)
    half = D_MODEL // 2
    lo = lax.shift_right_logical(pltpu.bitcast(h2b[:, :half].astype(F32), jnp.uint32), jnp.uint32(16))
    hi = pltpu.bitcast(h2b[:, half:].astype(F32), jnp.uint32)
    h2_ref[...] = hi | lo
    lg_ref[...] = (jnp.dot(h2b, wr_hi[...], preferred_element_type=F32)
                   + jnp.dot(h2b, wr_lo[...], preferred_element_type=F32) + br_ref[...])


def _outproj_residual(merged, x, modseg, gains, w_out_bf16, w_router, b_router):
    t = x.shape[0]
    tm = 256
    wr = jnp.pad(w_router.astype(F32), ((0, 0), (0, LANES - N_EXPERTS)))
    wr_hi, wr_lo = _split3(wr)
    br = jnp.pad(b_router.astype(F32), (0, LANES - N_EXPERTS)).reshape(1, LANES)
    gains8 = jnp.pad(gains.astype(F32), ((0, 4), (0, 0)))
    row = lambda width: pl.BlockSpec((tm, width), lambda i: (i, 0))
    full = lambda a, b: pl.BlockSpec((a, b), lambda i: (0, 0))
    return pl.pallas_call(
        _outproj_kernel,
        grid=(t // tm,),
        in_specs=[row(D_MODEL), row(D_MODEL),
                  pl.BlockSpec((1, 8, D_MODEL), lambda i: (i * tm // SEG, 0, 0)),
                  full(8, D_MODEL), full(D_MODEL, D_MODEL), full(D_MODEL, LANES), full(D_MODEL, LANES), full(1, LANES)],
        out_specs=[row(D_MODEL), row(D_MODEL // 2), row(LANES)],
        out_shape=[jax.ShapeDtypeStruct((t, D_MODEL), F32), jax.ShapeDtypeStruct((t, D_MODEL // 2), jnp.uint32),
                   jax.ShapeDtypeStruct((t, LANES), F32)],
        compiler_params=_cparams("parallel"),
        name="outproj_residual",
    )(merged, x, modseg, gains8, w_out_bf16, wr_hi, wr_lo, br)


R_TM = 512


def _route_kernel(lg_ref, ti_ref, tw_ref, rk_ref, cnt_ref, carry_ref):
    @pl.when(pl.program_id(0) == 0)
    def _():
        carry_ref[...] = jnp.zeros(carry_ref.shape, F32)

    lane = _lane_iota((R_TM, LANES))
    l = jnp.where(lane < N_EXPERTS, lg_ref[...], -jnp.inf)
    vals, idxs = [], []
    for _ in range(TOP_K):
        m = jnp.max(l, axis=-1, keepdims=True)
        idx = jnp.min(jnp.where(l == m, lane.astype(F32), float(LANES)), axis=-1, keepdims=True).astype(I32)
        vals.append(m)
        idxs.append(idx)
        l = jnp.where(lane == idx, -jnp.inf, l)
    es = [jnp.exp(v - vals[0]) for v in vals]
    den = es[0] + es[1] + es[2] + es[3]
    ti = jnp.zeros((R_TM, LANES), I32)
    tw = jnp.zeros((R_TM, LANES), F32)
    cnt = jnp.zeros((R_TM, LANES), F32)
    for k in range(TOP_K):
        ti = jnp.where(lane == k, idxs[k], ti)
        tw = jnp.where(lane == k, es[k] / den, tw)
        cnt = cnt + jnp.where(lane == idxs[k], 1.0, 0.0)
    ti_ref[...] = ti
    tw_ref[...] = tw
    r = lax.broadcasted_iota(I32, (R_TM, R_TM), 0)
    c = lax.broadcasted_iota(I32, (R_TM, R_TM), 1)
    tri = jnp.where(c < r, 1.0, 0.0).astype(BF16)
    before = jnp.dot(tri, cnt.astype(BF16), preferred_element_type=F32) + carry_ref[...]
    rk = jnp.zeros((R_TM, LANES), I32)
    for k in range(TOP_K):
        pos = jnp.sum(jnp.where(lane == idxs[k], before, 0.0), axis=-1, keepdims=True)
        rk = jnp.where(lane == k, pos.astype(I32), rk)
    rk_ref[...] = rk
    carry_ref[...] = carry_ref[...] + jnp.sum(cnt, axis=0, keepdims=True)
    cnt_ref[...] = jnp.broadcast_to(carry_ref[...], cnt_ref.shape)


def _route(logits):
    t = logits.shape[0]
    row = pl.BlockSpec((R_TM, LANES), lambda i: (i, 0))
    return pl.pallas_call(
        _route_kernel,
        grid=(t // R_TM,),
        in_specs=[row],
        out_specs=[row, row, row, pl.BlockSpec((8, LANES), lambda i: (0, 0))],
        out_shape=[jax.ShapeDtypeStruct((t, LANES), I32), jax.ShapeDtypeStruct((t, LANES), F32),
                   jax.ShapeDtypeStruct((t, LANES), I32), jax.ShapeDtypeStruct((8, LANES), F32)],
        scratch_shapes=[pltpu.VMEM((1, LANES), F32)],
        compiler_params=_cparams("arbitrary"),
        name="moe_route",
    )(logits)


E_TM = 512
DISP_TT = 512
COMB_TT = 512


ROW_DMA_UNROLL = 8


def _row_dma_burst(copy, n):
    per_trip = ROW_DMA_UNROLL // TOP_K

    def start(g, c):
        for u in range(ROW_DMA_UNROLL):
            copy(g * ROW_DMA_UNROLL + u, g * per_trip + u // TOP_K, u % TOP_K).start(priority=u % 2)
        return c

    def wait(g, c):
        for u in range(ROW_DMA_UNROLL):
            copy(g * ROW_DMA_UNROLL + u, g * per_trip + u // TOP_K, u % TOP_K).wait()
        return c

    lax.fori_loop(0, n // ROW_DMA_UNROLL, start, 0)
    lax.fori_loop(0, n // ROW_DMA_UNROLL, wait, 0)


def _dispatch_kernel(pos_ref, h_ref, xs_in, xs_ref, sem):
    del xs_in

    def copy(a, token, k):
        del k
        return pltpu.make_async_copy(h_ref.at[pl.ds(token, 1)], xs_ref.at[pl.ds(pos_ref[a], 1)], sem)

    _row_dma_burst(copy, DISP_TT * TOP_K)


def _dispatch(pos_flat, h2p, n_rows):
    t, width = h2p.shape
    xs0 = jnp.zeros((n_rows, width), h2p.dtype)
    return pl.pallas_call(
        _dispatch_kernel,
        grid=(t // DISP_TT,),
        in_specs=[pl.BlockSpec((DISP_TT * TOP_K,), lambda i: (i,), memory_space=pltpu.SMEM),
                  pl.BlockSpec((DISP_TT, width), lambda i: (i, 0)),
                  pl.BlockSpec(memory_space=pl.ANY)],
        out_specs=pl.BlockSpec(memory_space=pl.ANY),
        scratch_shapes=[pltpu.SemaphoreType.DMA(())],
        out_shape=jax.ShapeDtypeStruct((n_rows, width), h2p.dtype),
        input_output_aliases={2: 0},
        compiler_params=_cparams("arbitrary"),
        name="moe_dispatch",
    )(pos_flat, h2p, xs0)


UP_TN = 1024
DN_TN = 1024
SEL_K = 256


def _weights_changed(te_ref, i):
    return (i == 0) | (te_ref[i] != te_ref[jnp.maximum(i - 1, 0)])


def _up_kernel(te_ref, nu_ref, x_ref, w_ref, bg_ref, bl_ref, sel_ref, o_ref, wg_ref, wl_ref):
    i = pl.program_id(1)

    @pl.when(i < nu_ref[0])
    def _():
        @pl.when(_weights_changed(te_ref, i))
        def _():
            for c in range(UP_TN // SEL_K):
                wc = w_ref[0, :, c * SEL_K:(c + 1) * SEL_K].astype(BF16)
                cols = slice(c * (SEL_K // 2), (c + 1) * (SEL_K // 2))
                both = jnp.dot(wc, sel_ref[...], preferred_element_type=F32).astype(BF16)
                wg_ref[:, cols] = both[:, :SEL_K // 2]
                wl_ref[:, cols] = both[:, SEL_K // 2:]

        xw = x_ref[...]
        x = jnp.concatenate([pltpu.bitcast(lax.shift_left(xw, jnp.uint32(16)), F32).astype(BF16),
                             pltpu.bitcast(xw & jnp.uint32(0xFFFF0000), F32).astype(BF16)], axis=1)
        glu = jnp.dot(x, wg_ref[...], preferred_element_type=F32) + bg_ref[0]
        lin = jnp.dot(x, wl_ref[...], preferred_element_type=F32) + bl_ref[0]
        glu = jnp.minimum(glu, SWIGLU_LIMIT)
        lin = jnp.clip(lin, -SWIGLU_LIMIT, SWIGLU_LIMIT)
        o_ref[...] = (glu * jax.nn.sigmoid(SWIGLU_ALPHA * glu) * (lin + 1.0)).astype(o_ref.dtype)


def _expert_up(tile_e, n_used, xs, w_up, b_up, layer):
    n_rows = xs.shape[0]
    pick = jnp.arange(SEL_K)[:, None] - 2 * jnp.arange(SEL_K // 2)[None, :]
    sel = jnp.concatenate([pick == 0, pick == 1], axis=1).astype(BF16)
    bu = b_up[layer].astype(F32)
    bg, bl = bu[:, None, 0::2], bu[:, None, 1::2]
    half_tn = UP_TN // 2
    bspec = pl.BlockSpec((1, 1, half_tn), lambda j, i, te, nu: (te[i], 0, j))
    return pl.pallas_call(
        _up_kernel,
        grid_spec=pltpu.PrefetchScalarGridSpec(
            num_scalar_prefetch=2, grid=(2 * D_FF // UP_TN, n_rows // E_TM),
            in_specs=[pl.BlockSpec((E_TM, D_MODEL // 2), lambda j, i, te, nu: (i, 0)),
                      pl.BlockSpec((None, 1, D_MODEL, UP_TN), lambda j, i, te, nu: (layer, te[i], 0, j)),
                      bspec, bspec,
                      pl.BlockSpec((SEL_K, SEL_K), lambda j, i, te, nu: (0, 0))],
            out_specs=pl.BlockSpec((E_TM, half_tn), lambda j, i, te, nu: (i, j)),
            scratch_shapes=[pltpu.VMEM((D_MODEL, half_tn), BF16), pltpu.VMEM((D_MODEL, half_tn), BF16)]),
        out_shape=jax.ShapeDtypeStruct((n_rows, D_FF), BF16),
        compiler_params=_cparams("arbitrary", "arbitrary"),
        name="moe_up",
    )(tile_e, n_used, xs, w_up, bg, bl, sel)


def _down_kernel(te_ref, nu_ref, a_ref, w_ref, b_ref, o_ref, wbf_ref):
    i = pl.program_id(1)

    @pl.when(i < nu_ref[0])
    def _():
        @pl.when(_weights_changed(te_ref, i))
        def _():
            wbf_ref[...] = w_ref[0].astype(BF16)

        o_ref[...] = jnp.dot(a_ref[...], wbf_ref[...], preferred_element_type=F32) + b_ref[0]


def _expert_down(tile_e, n_used, act, w_down, b_down, layer):
    n_rows = act.shape[0]
    return pl.pallas_call(
        _down_kernel,
        grid_spec=pltpu.PrefetchScalarGridSpec(
            num_scalar_prefetch=2, grid=(D_MODEL // DN_TN, n_rows // E_TM),
            in_specs=[pl.BlockSpec((E_TM, D_FF), lambda j, i, te, nu: (i, 0)),
                      pl.BlockSpec((None, 1, D_FF, DN_TN), lambda j, i, te, nu: (layer, te[i], 0, j)),
                      pl.BlockSpec((None, 1, 1, DN_TN), lambda j, i, te, nu: (layer, te[i], 0, j))],
            out_specs=pl.BlockSpec((E_TM, DN_TN), lambda j, i, te, nu: (i, j)),
            scratch_shapes=[pltpu.VMEM((D_FF, DN_TN), BF16)]),
        out_shape=jax.ShapeDtypeStruct((n_rows, D_MODEL), F32),
        compiler_params=_cparams("arbitrary", "arbitrary"),
        name="moe_down",
    )(tile_e, n_used, act, w_down, b_down.reshape(DEPTH, N_EXPERTS, 1, D_MODEL))


def _combine_kernel(pos_ref, ys_ref, tw_ref, x_ref, mod_ref, g_ref, *rest, split_blk):
    o_refs, (buf, sem) = rest[:-2], rest[-2:]

    def copy(a, token, k):
        return pltpu.make_async_copy(ys_ref.at[pl.ds(pos_ref[a], 1)], buf.at[k, pl.ds(token, 1)], sem)

    _row_dma_burst(copy, COMB_TT * TOP_K)
    tw = tw_ref[...]
    f = tw[:, 0:1] * buf[0]
    for k in range(1, TOP_K):
        f = f + tw[:, k:k + 1] * buf[k]
    m = mod_ref[0]
    fn = f * lax.rsqrt(jnp.mean(f * f, axis=-1, keepdims=True) + EPS) * g_ref[3:4]
    out = x_ref[...] + m[5:6] * fn
    if split_blk is None:
        o_refs[0][...] = out
    else:
        @pl.when(pl.program_id(0) < split_blk)
        def _():
            o_refs[0][...] = out

        @pl.when(pl.program_id(0) >= split_blk)
        def _():
            o_refs[1][...] = out


def _combine(pos_flat, ys, tw, x, modseg, gains, split_rows=None):
    t = x.shape[0]
    gains8 = jnp.pad(gains.astype(F32), ((0, 4), (0, 0)))
    if split_rows is None:
        split_blk = None
        out_specs = pl.BlockSpec((COMB_TT, D_MODEL), lambda i: (i, 0))
        out_shape = jax.ShapeDtypeStruct((t, D_MODEL), F32)
    else:
        split_blk = split_rows // COMB_TT
        assert split_rows % COMB_TT == 0 and 0 < split_blk < t // COMB_TT
        out_specs = [pl.BlockSpec((COMB_TT, D_MODEL), lambda i: (jnp.minimum(i, split_blk - 1), 0)),
                     pl.BlockSpec((COMB_TT, D_MODEL), lambda i: (jnp.maximum(i - split_blk, 0), 0))]
        out_shape = [jax.ShapeDtypeStruct((split_rows, D_MODEL), F32),
                     jax.ShapeDtypeStruct((t - split_rows, D_MODEL), F32)]
    return pl.pallas_call(
        functools.partial(_combine_kernel, split_blk=split_blk),
        grid=(t // COMB_TT,),
        in_specs=[pl.BlockSpec((COMB_TT * TOP_K,), lambda i: (i,), memory_space=pltpu.SMEM),
                  pl.BlockSpec(memory_space=pl.ANY),
                  pl.BlockSpec((COMB_TT, LANES), lambda i: (i, 0)),
                  pl.BlockSpec((COMB_TT, D_MODEL), lambda i: (i, 0)),
                  pl.BlockSpec((1, 8, D_MODEL), lambda i: (i * COMB_TT // SEG, 0, 0)),
                  pl.BlockSpec((8, D_MODEL), lambda i: (0, 0))],
        out_specs=out_specs,
        scratch_shapes=[pltpu.VMEM((TOP_K, COMB_TT, D_MODEL), F32), pltpu.SemaphoreType.DMA(())],
        out_shape=out_shape,
        compiler_params=_cparams("arbitrary"),
        name="moe_combine",
    )(pos_flat, ys, tw, x, modseg, gains8)


def _moe(h2p, logits, x, modseg, gains, w_up, b_up, w_down, b_down, layer, split_rows=None):
    t = h2p.shape[0]
    ti, tw, rk, cnt = _route(logits)
    counts = cnt[0, :N_EXPERTS].astype(I32)
    padded = (counts + E_TM - 1) // E_TM * E_TM
    upto = jnp.arange(N_EXPERTS)[None, :] <= jnp.arange(N_EXPERTS)[:, None]
    ends = jnp.sum(jnp.where(upto, padded[None, :], 0), axis=1).astype(I32)
    offsets = ends - padded
    n_tiles = t * TOP_K // E_TM + N_EXPERTS
    starts = jnp.arange(n_tiles, dtype=I32) * E_TM
    tile_e = jnp.minimum(jnp.sum((ends[None, :] <= starts[:, None]).astype(I32), axis=1), N_EXPERTS - 1)
    n_used = (ends[-1:] // E_TM).astype(I32)
    ti4, rk4 = ti[:, :TOP_K], rk[:, :TOP_K]
    first_row = jnp.sum(jnp.where(ti4[:, :, None] == jnp.arange(N_EXPERTS)[None, None, :], offsets[None, None, :], 0), axis=-1)
    pos_flat = (first_row + rk4).astype(I32).reshape(-1)
    xs = _dispatch(pos_flat, h2p, n_tiles * E_TM)
    act = _expert_up(tile_e, n_used, xs, w_up, b_up, layer)
    ys = _expert_down(tile_e, n_used, act, w_down, b_down, layer)
    return _combine(pos_flat, ys, tw, x, modseg, gains, split_rows)


def _forward(x, c8, seg_rows, lay, t5_bias, w_mod, b_mod, norm_gains, w_in, qk_norm_gains, na_rpb,
             ret_decay_logit, w_branch, w_out, w_router, b_router, w_up, b_up, w_down, b_down):
    mod = _modulation(c8, w_mod, b_mod)
    modseg = mod[:, seg_rows, :].reshape(DEPTH, len(seg_rows), 6, D_MODEL)
    modseg = jnp.pad(modseg, ((0, 0), (0, 0), (0, 2), (0, 0)))
    cos_a, sin_a = _axial_tables(lay)
    cos_r, sin_r = _rope_tables(lay)
    dil_bias = [_dil_bias(t5_bias, g, d, w // (2 * d)) for g, (w, d) in enumerate(DIL_CONFIGS)]
    for l in range(DEPTH):
        gains = norm_gains[l]
        proj = _norm_inproj(x, modseg[l], gains[0], w_in[l].astype(BF16))
        a_parts = [_dilated_group(proj, dil_bias[g], lay, g) for g in range(A_GROUPS)]
        o_b = _axial_gqa(proj, _b_prepare(proj, qk_norm_gains[l], cos_a, sin_a), lay)
        o_c = _neighbourhood(proj, na_rpb[l], lay)
        o_d = _retention(proj, ret_decay_logit[l], cos_r, sin_r, lay)
        merged = _branch_merge(proj, a_parts, o_b, o_c, o_d, w_branch[l].astype(BF16))
        x, h2p, logits = _outproj_residual(merged, x, modseg[l], gains, w_out[l].astype(BF16), w_router[l], b_router[l])
        x = _moe(h2p, logits, x, modseg[l], gains, w_up, b_up, w_down, b_down, l,
                 split_rows=lay.t_p if l == DEPTH - 1 else None)
    return x


def kernel(x_prompt, x_sample, c_prompt, c_sample, t5_bias, w_mod, b_mod, norm_gains, w_in, qk_norm_gains,
           na_rpb, ret_decay_logit, w_branch, w_out, w_router, b_router, w_up, b_up, w_down, b_down):
    n_p, s_p, _ = x_prompt.shape
    n_s, s_s, _ = x_sample.shape
    lay = Layout(n_p, s_p, n_s, s_s)
    assert s_p % SEG == 0 and s_s % SEG == 0 and n_p + n_s <= 8
    x = jnp.concatenate([x_prompt.reshape(-1, D_MODEL), x_sample.reshape(-1, D_MODEL)], axis=0)
    c8 = jnp.concatenate([c_prompt, c_sample, jnp.zeros((8 - n_p - n_s, D_MODEL), F32)], axis=0)
    seg_rows = tuple([b for b in range(n_p) for _ in range(s_p // SEG)]
                     + [n_p + b for b in range(n_s) for _ in range(s_s // SEG)])
    y_p, y_s = _forward(x, c8, jnp.asarray(seg_rows, dtype=I32), lay, t5_bias, w_mod, b_mod, norm_gains, w_in,
                        qk_norm_gains, na_rpb, ret_decay_logit, w_branch, w_out, w_router, b_router,
                        w_up, b_up, w_down, b_down)
    return (y_p.reshape(n_p, s_p, D_MODEL), y_s.reshape(n_s, s_s, D_MODEL))
```

```python
import functools
import math
from typing import NamedTuple

import jax
import jax.numpy as jnp
from jax import lax
from jax.experimental import pallas as pl
from jax.experimental.pallas import tpu as pltpu

F32 = jnp.float32
BF16 = jnp.bfloat16
I32 = jnp.int32

D_MODEL = 2048
DEPTH = 2
HEAD_DIM = 128
GRID_W = 64
EPS = 1e-6
ROPE_THETA = 10000.0
NEG_INF = -1e30
LOG2E = 1.4426950408889634

DIL_CONFIGS = ((128, 1), (512, 4), (2048, 16))
A_GROUPS = 3
A_HEADS_PER_GROUP = 6
A_HEADS = A_GROUPS * A_HEADS_PER_GROUP
T5_BUCKETS = 32
T5_MAX_DIST = 1024
B_Q_HEADS = 6
B_KV_HEADS = 2
C_HEADS = 6
NA_ROWS = 8
NA_COLS = 16
D_HEADS = 4
D_QK = 128
D_V = 256
N_EXPERTS = 32
TOP_K = 4
D_FF = 2048
SWIGLU_LIMIT = 7.0
SWIGLU_ALPHA = 1.702
N_BRANCH = 4

A_OUT = A_HEADS_PER_GROUP * HEAD_DIM
B_OUT = B_Q_HEADS * HEAD_DIM
C_OUT = C_HEADS * HEAD_DIM
D_OUT = D_HEADS * D_V
MIX_WIDTH = A_OUT + B_OUT + C_OUT + D_OUT

IN_SPLITS = (A_HEADS * HEAD_DIM, A_HEADS * HEAD_DIM, A_HEADS * HEAD_DIM,
             B_Q_HEADS * HEAD_DIM, B_KV_HEADS * HEAD_DIM, B_KV_HEADS * HEAD_DIM,
             C_HEADS * HEAD_DIM, C_HEADS * HEAD_DIM, C_HEADS * HEAD_DIM,
             D_HEADS * D_QK, D_HEADS * D_QK, D_HEADS * D_V, D_HEADS * D_V,
             N_BRANCH * D_MODEL)
N_IN = sum(IN_SPLITS)
_OFF = [0]
for _w in IN_SPLITS:
    _OFF.append(_OFF[-1] + _w)
(OFF_AQ, OFF_AK, OFF_AV, OFF_BQ, OFF_BK, OFF_BV, OFF_CQ, OFF_CK, OFF_CV,
 OFF_DQ, OFF_DK, OFF_DV, OFF_DG, OFF_GATE) = _OFF[:-1]

SEG = 2048
LANES = 128
VMEM_LIMIT = 56 * 1024 * 1024


class Layout(NamedTuple):
    n_p: int
    s_p: int
    n_s: int
    s_s: int

    @property
    def t_p(self):
        return self.n_p * self.s_p

    @property
    def t(self):
        return self.n_p * self.s_p + self.n_s * self.s_s


def _cparams(*sem):
    return pltpu.CompilerParams(dimension_semantics=sem, vmem_limit_bytes=VMEM_LIMIT)


def _mod_kernel(c_ref, w_ref, b_ref, o_ref):
    c = c_ref[...]
    s = (c * jax.nn.sigmoid(c)).astype(BF16)
    o_ref[0] = jnp.dot(s, w_ref[0].astype(BF16), preferred_element_type=F32) + b_ref[0]


def _modulation(c8, w_mod, b_mod):
    tn = 1024
    n6 = 6 * D_MODEL
    return pl.pallas_call(
        _mod_kernel,
        grid=(DEPTH, n6 // tn),
        in_specs=[pl.BlockSpec((8, D_MODEL), lambda l, j: (0, 0)),
                  pl.BlockSpec((1, D_MODEL, tn), lambda l, j: (l, 0, j)),
                  pl.BlockSpec((1, 1, tn), lambda l, j: (l, 0, j))],
        out_specs=pl.BlockSpec((1, 8, tn), lambda l, j: (l, 0, j)),
        out_shape=jax.ShapeDtypeStruct((DEPTH, 8, n6), F32),
        compiler_params=_cparams("parallel", "parallel"),
        name="adaln_mod",
    )(c8, w_mod, b_mod.reshape(DEPTH, 1, n6))


def _inproj_kernel(x_ref, mod_ref, g_ref, w_ref, o_ref, h_ref):
    @pl.when(pl.program_id(1) == 0)
    def _():
        x = x_ref[...]
        y = x * lax.rsqrt(jnp.mean(x * x, axis=-1, keepdims=True) + EPS) * g_ref[...]
        m = mod_ref[0]
        h_ref[...] = (y * (1.0 + m[1:2]) + m[0:1]).astype(BF16)

    o_ref[...] = jnp.dot(h_ref[...], w_ref[...], preferred_element_type=F32).astype(o_ref.dtype)


def _norm_inproj(x, modseg, gain, w_in_bf16):
    t = x.shape[0]
    tm, tn = 1024, 1280
    assert t % tm == 0 and N_IN % tn == 0 and SEG % tm == 0
    return pl.pallas_call(
        _inproj_kernel,
        grid=(t // tm, N_IN // tn),
        in_specs=[pl.BlockSpec((tm, D_MODEL), lambda i, j: (i, 0)),
                  pl.BlockSpec((1, 8, D_MODEL), lambda i, j: (i * tm // SEG, 0, 0)),
                  pl.BlockSpec((1, D_MODEL), lambda i, j: (0, 0)),
                  pl.BlockSpec((D_MODEL, tn), lambda i, j: (0, j))],
        out_specs=pl.BlockSpec((tm, tn), lambda i, j: (i, j)),
        out_shape=jax.ShapeDtypeStruct((t, N_IN), BF16),
        scratch_shapes=[pltpu.VMEM((tm, D_MODEL), BF16)],
        compiler_params=_cparams("parallel", "arbitrary"),
        name="norm_inproj",
    )(x, modseg, gain.reshape(1, D_MODEL), w_in_bf16)


def _local_pos(lay):
    return jnp.concatenate([jnp.tile(jnp.arange(lay.s_p), lay.n_p), jnp.tile(jnp.arange(lay.s_s), lay.n_s)])


def _axial_tables(lay):
    pos = _local_pos(lay)
    lane = jnp.arange(LANES)
    quarter = HEAD_DIM // 4
    freqs = ROPE_THETA ** (-jnp.arange(quarter, dtype=F32) / quarter)
    f = freqs[lane % quarter]
    p = jnp.where(lane[None, :] < HEAD_DIM // 2, (pos // GRID_W)[:, None], (pos % GRID_W)[:, None]).astype(F32)
    ang = p * f[None, :]
    sign = jnp.where((lane % (2 * quarter)) < quarter, -1.0, 1.0).astype(F32)
    return jnp.cos(ang), jnp.sin(ang) * sign[None, :]


def _rope_tables(lay):
    pos = _local_pos(lay)
    lane = jnp.arange(LANES)
    half = D_QK // 2
    freqs = ROPE_THETA ** (-jnp.arange(half, dtype=F32) / half)
    ang = pos.astype(F32)[:, None] * freqs[lane % half][None, :]
    sign = jnp.where(lane < half, -1.0, 1.0).astype(F32)
    return jnp.cos(ang), jnp.sin(ang) * sign[None, :]


def _lane_iota(shape):
    return lax.broadcasted_iota(I32, shape, len(shape) - 1)


def _axial_rotate(x, cos, sin_signed):
    q = HEAD_DIM // 4
    lo = (_lane_iota(x.shape) % (2 * q)) < q
    partner = jnp.where(lo, pltpu.roll(x, LANES - q, 1), pltpu.roll(x, q, 1))
    return x * cos + partner * sin_signed


def _rope_rotate(x, cos, sin_signed):
    return x * cos + pltpu.roll(x, D_QK // 2, 1) * sin_signed


def _bprep_kernel(x_ref, g_ref, cos_ref, sin_ref, o_ref):
    x = x_ref[...].astype(F32)
    y = x * lax.rsqrt(jnp.mean(x * x, axis=-1, keepdims=True) + EPS) * g_ref[0]
    y = _axial_rotate(y, cos_ref[...], sin_ref[...])
    scale = jnp.where(pl.program_id(1) < B_Q_HEADS, HEAD_DIM ** -0.5 * LOG2E, 1.0)
    o_ref[...] = (y * scale).astype(o_ref.dtype)


def _b_prepare(proj, qk_g, cos, sin):
    t = proj.shape[0]
    tr = 1024
    nh = B_Q_HEADS + B_KV_HEADS
    g8 = jnp.concatenate([jnp.tile(qk_g[0:1], (B_Q_HEADS, 1)), jnp.tile(qk_g[1:2], (B_KV_HEADS, 1))]).reshape(nh, 1, HEAD_DIM)
    return pl.pallas_call(
        _bprep_kernel,
        grid=(t // tr, nh),
        in_specs=[pl.BlockSpec((tr, HEAD_DIM), lambda i, h: (i, OFF_BQ // HEAD_DIM + h)),
                  pl.BlockSpec((1, 1, HEAD_DIM), lambda i, h: (h, 0, 0)),
                  pl.BlockSpec((tr, HEAD_DIM), lambda i, h: (i, 0)),
                  pl.BlockSpec((tr, HEAD_DIM), lambda i, h: (i, 0))],
        out_specs=pl.BlockSpec((tr, HEAD_DIM), lambda i, h: (i, h)),
        out_shape=jax.ShapeDtypeStruct((t, nh * HEAD_DIM), BF16),
        compiler_params=_cparams("parallel", "parallel"),
        name="b_prep",
    )(proj, g8, cos, sin)


FLASH_ROW_SPLIT = 2


def _flash_kernel(qt_ref, kt_ref, first_ref, last_ref, q_ref, k_ref, v_ref, o_ref, m_ref, acc_ref):
    s_id = pl.program_id(1)
    rep = B_Q_HEADS // B_KV_HEADS

    @pl.when(first_ref[s_id] == 1)
    def _():
        m_ref[...] = jnp.full(m_ref.shape, -jnp.inf, F32)
        acc_ref[...] = jnp.zeros(acc_ref.shape, F32)

    k = k_ref[...]
    v = v_ref[...]
    rb = q_ref.shape[0] // FLASH_ROW_SPLIT
    units = [(r, slice(u * rb, (u + 1) * rb)) for r in range(rep) for u in range(FLASH_ROW_SPLIT)]
    scores = [lax.dot_general(q_ref[rows, r * HEAD_DIM:(r + 1) * HEAD_DIM], k, (((1,), (1,)), ((), ())),
                              preferred_element_type=F32) for (r, rows) in units]
    probs, alphas = [], []
    for (r, rows), s in zip(units, scores):
        m_prev = m_ref[r, rows, :]
        m_cur = jnp.maximum(m_prev, jnp.max(s, axis=-1, keepdims=True))
        alphas.append(jnp.exp2(m_prev - m_cur))
        probs.append(jnp.exp2(s - m_cur[:, 0:1]).astype(BF16))
        m_ref[r, rows, :] = m_cur
    for (r, rows), p, alpha in zip(units, probs, alphas):
        pv = jnp.dot(p, v, preferred_element_type=F32)
        acc_ref[r, rows, :] = jnp.concatenate([alpha, alpha], axis=1) * acc_ref[r, rows, :] + pv

    @pl.when(last_ref[s_id] == 1)
    def _():
        for r in range(rep):
            acc = acc_ref[r]
            o_ref[:, r * HEAD_DIM:(r + 1) * HEAD_DIM] = (acc[:, :HEAD_DIM] / acc[:, HEAD_DIM:]).astype(o_ref.dtype)


def _flash_tables(lay, tq, tk):
    qt, kt, first, last = [], [], [], []
    for (n, s, base) in ((lay.n_p, lay.s_p, 0), (lay.n_s, lay.s_s, lay.t_p)):
        for b in range(n):
            for qi in range(s // tq):
                nk = s // tk
                for ki in range(nk):
                    qt.append((base + b * s) // tq + qi)
                    kt.append((base + b * s) // tk + ki)
                    first.append(int(ki == 0))
                    last.append(int(ki == nk - 1))
    mk = lambda a: jnp.asarray(a, dtype=I32)
    return mk(qt), mk(kt), mk(first), mk(last)


def _axial_gqa(proj, bprep, lay):
    t = proj.shape[0]
    tq, tk = 512, 2048
    assert lay.s_p % tk == 0 and lay.s_s % tk == 0
    rep = B_Q_HEADS // B_KV_HEADS
    qt, kt, first, last = _flash_tables(lay, tq, tk)
    n_steps = qt.shape[0]
    v = proj[:, OFF_BV:OFF_BV + B_KV_HEADS * HEAD_DIM].reshape(t, B_KV_HEADS, HEAD_DIM)
    v_ones = jnp.concatenate([v, jnp.ones_like(v)], axis=-1).reshape(t, B_KV_HEADS * 2 * HEAD_DIM)
    gs = pltpu.PrefetchScalarGridSpec(
        num_scalar_prefetch=4,
        grid=(B_KV_HEADS, n_steps),
        in_specs=[pl.BlockSpec((tq, rep * HEAD_DIM), lambda g, s, qt, kt, f, l: (qt[s], g)),
                  pl.BlockSpec((tk, HEAD_DIM), lambda g, s, qt, kt, f, l: (kt[s], B_Q_HEADS + g)),
                  pl.BlockSpec((tk, 2 * HEAD_DIM), lambda g, s, qt, kt, f, l: (kt[s], g))],
        out_specs=pl.BlockSpec((tq, rep * HEAD_DIM), lambda g, s, qt, kt, f, l: (qt[s], g)),
        scratch_shapes=[pltpu.VMEM((rep, tq, LANES), F32), pltpu.VMEM((rep, tq, 2 * HEAD_DIM), F32)],
    )
    return pl.pallas_call(
        _flash_kernel,
        grid_spec=gs,
        out_shape=jax.ShapeDtypeStruct((t, B_OUT), BF16),
        compiler_params=_cparams("parallel", "arbitrary"),
        name="b_flash",
    )(qt, kt, first, last, bprep, bprep, v_ones)


A_BQ = 128
A_NSUB = 2


def _t5_bucket(rel):
    nb = T5_BUCKETS // 2
    max_exact = nb // 2
    n = jnp.abs(rel)
    large = max_exact + (jnp.log(jnp.maximum(n, 1).astype(F32) / max_exact)
                         / math.log(T5_MAX_DIST / max_exact) * (nb - max_exact)).astype(I32)
    large = jnp.minimum(large, nb - 1)
    return jnp.where(rel > 0, nb, 0) + jnp.where(n < max_exact, n, large)


def _dil_bias(t5_bias, g, d, half):
    rel = (jnp.arange(3 * A_BQ)[None, :] - A_BQ) - jnp.arange(A_BQ)[:, None]
    tab = t5_bias[:, g * A_HEADS_PER_GROUP:(g + 1) * A_HEADS_PER_GROUP].astype(F32)
    onehot = (_t5_bucket(rel * d)[:, :, None] == jnp.arange(T5_BUCKETS)[None, None, :]).astype(F32)
    bias = jnp.einsum("qkb,bh->hqk", onehot, tab, precision=lax.Precision.HIGHEST)
    return jnp.where((jnp.abs(rel) <= half)[None], bias, NEG_INF)


def _dil_kernel(q_ref, kp_ref, kc_ref, kn_ref, vp_ref, vc_ref, vn_ref, b_ref, o_ref, lse_ref, *, nblk_p, tblk_p, nblk_s):
    scale = HEAD_DIM ** -0.5
    dn = (((1,), (1,)), ((), ()))
    units = []
    for s in range(A_NSUB):
        r = pl.program_id(1) * A_NSUB + s
        in_p = r < tblk_p
        nblk = jnp.where(in_p, nblk_p, nblk_s)
        il = jnp.where(in_p, r, r - tblk_p) % nblk
        prev_ok = il > 0
        next_ok = il < nblk - 1
        rows = slice(s * A_BQ, (s + 1) * A_BQ)
        before = (kp_ref, vp_ref, slice((A_NSUB - 1) * A_BQ, A_NSUB * A_BQ)) if s == 0 else \
            (kc_ref, vc_ref, slice((s - 1) * A_BQ, s * A_BQ))
        after = (kn_ref, vn_ref, slice(0, A_BQ)) if s == A_NSUB - 1 else \
            (kc_ref, vc_ref, slice((s + 1) * A_BQ, (s + 2) * A_BQ))
        for h in range(A_HEADS_PER_GROUP):
            sl = slice(h * HEAD_DIM, (h + 1) * HEAD_DIM)
            q = q_ref[rows, sl]
            b = b_ref[h]
            sp = lax.dot_general(q, before[0][before[2], sl], dn, preferred_element_type=F32) * scale + b[:, 0:A_BQ]
            sc = lax.dot_general(q, kc_ref[rows, sl], dn, preferred_element_type=F32) * scale + b[:, A_BQ:2 * A_BQ]
            sn = lax.dot_general(q, after[0][after[2], sl], dn, preferred_element_type=F32) * scale + b[:, 2 * A_BQ:]
            sp = jnp.where(prev_ok, sp, NEG_INF)
            sn = jnp.where(next_ok, sn, NEG_INF)
            units.append((rows, sl, before, after, sp, sc, sn))
    soft = []
    for (_, _, _, _, sp, sc, sn) in units:
        m = jnp.maximum(jnp.maximum(jnp.max(sp, axis=-1, keepdims=True), jnp.max(sc, axis=-1, keepdims=True)),
                        jnp.max(sn, axis=-1, keepdims=True))
        pp, pc, pn = jnp.exp(sp - m), jnp.exp(sc - m), jnp.exp(sn - m)
        l = (jnp.sum(pp, axis=-1, keepdims=True) + jnp.sum(pc, axis=-1, keepdims=True)
             + jnp.sum(pn, axis=-1, keepdims=True))
        soft.append((m, l, pp, pc, pn))
    for (rows, sl, before, after, _, _, _), (m, l, pp, pc, pn) in zip(units, soft):
        o = (jnp.dot(pp.astype(BF16), before[1][before[2], sl], preferred_element_type=F32)
             + jnp.dot(pc.astype(BF16), vc_ref[rows, sl], preferred_element_type=F32)
             + jnp.dot(pn.astype(BF16), after[1][after[2], sl], preferred_element_type=F32))
        o_ref[rows, sl] = (o / l).astype(o_ref.dtype)
        lse_ref[rows, sl] = jnp.broadcast_to(m + jnp.log(l), (A_BQ, HEAD_DIM))


def _dilated_group(proj, bias, lay, g):
    w, d = DIL_CONFIGS[g]
    t = proj.shape[0]
    rows = t // d
    nblk_p = lay.s_p // d // A_BQ
    nblk_s = lay.s_s // d // A_BQ
    assert nblk_p >= 1 and nblk_s >= 1 and w // (2 * d) <= A_BQ
    tblk_p = lay.t_p // d // A_BQ
    br = A_BQ * A_NSUB
    assert rows % br == 0
    tblk = rows // br
    gw = A_HEADS_PER_GROUP * HEAD_DIM

    def spec(shift):
        return pl.BlockSpec((br, gw), lambda c, i: (jnp.clip(i + shift, 0, tblk - 1), c))

    if d == 1:
        def win(off, shift):
            return pl.BlockSpec(
                (pl.Element(br), pl.Element(gw)),
                lambda c, i: (pl.multiple_of(jnp.clip(i + shift, 0, tblk - 1) * br, br), off + g * gw))
        q_c = k_c = v_c = proj
        qkv_specs = [win(OFF_AQ, 0), win(OFF_AK, -1), win(OFF_AK, 0), win(OFF_AK, 1),
                     win(OFF_AV, -1), win(OFF_AV, 0), win(OFF_AV, 1)]
    else:
        q_c, k_c, v_c = [proj[:, off + g * gw:off + (g + 1) * gw].reshape(rows, d * gw)
                         for off in (OFF_AQ, OFF_AK, OFF_AV)]
        qkv_specs = [spec(0), spec(-1), spec(0), spec(1), spec(-1), spec(0), spec(1)]

    kern = functools.partial(_dil_kernel, nblk_p=nblk_p, tblk_p=tblk_p, nblk_s=nblk_s)
    o, lse = pl.pallas_call(
        kern,
        grid=(d, tblk),
        in_specs=qkv_specs + [pl.BlockSpec((A_HEADS_PER_GROUP, A_BQ, 3 * A_BQ), lambda c, i: (0, 0, 0))],
        out_specs=[spec(0), spec(0)],
        out_shape=[jax.ShapeDtypeStruct((rows, d * gw), BF16), jax.ShapeDtypeStruct((rows, d * gw), F32)],
        compiler_params=_cparams("parallel", "parallel"),
        name=f"a_dilated_g{g}",
    )(q_c, k_c, k_c, k_c, v_c, v_c, v_c, bias)
    return o.reshape(t, gw), lse.reshape(t, gw)


C_QROWS = 8
C_KROWS = 2 * NA_ROWS
C_TQ = C_QROWS * GRID_W
C_TK = C_KROWS * GRID_W


def _na_bias(rpb):
    hi = lax.Precision.HIGHEST
    qc = jnp.arange(GRID_W)[:, None]
    kc = jnp.arange(GRID_W)[None, :]
    cstart = jnp.clip(qc - NA_COLS // 2, 0, GRID_W - NA_COLS)
    col_ok = (kc >= cstart) & (kc < cstart + NA_COLS)
    ci = jnp.clip(kc - qc, -(NA_COLS - 1), NA_COLS - 1) + NA_COLS - 1
    oh_c = (ci[:, :, None] == jnp.arange(2 * NA_COLS - 1)[None, None, :]).astype(F32)
    by_col = jnp.einsum("hrc,abc->hrab", rpb.astype(F32), oh_c, precision=hi)
    out = []
    for off in (0, NA_ROWS // 2, NA_ROWS):
        qr = (off + jnp.arange(C_QROWS))[:, None]
        kr = jnp.arange(C_KROWS)[None, :]
        rstart = jnp.clip(qr - NA_ROWS // 2, 0, C_KROWS - NA_ROWS)
        row_ok = (kr >= rstart) & (kr < rstart + NA_ROWS)
        ri = jnp.clip(kr - qr + NA_ROWS - 1, 0, 2 * NA_ROWS - 2)
        oh_r = (ri[:, :, None] == jnp.arange(2 * NA_ROWS - 1)[None, None, :]).astype(F32)
        b = jnp.einsum("qkr,hrab->hqakb", oh_r, by_col, precision=hi)
        ok = row_ok[:, None, :, None] & col_ok[None, :, None, :]
        out.append(jnp.where(ok[None], b, NEG_INF).reshape(C_HEADS, C_TQ, C_TK))
    return jnp.stack(out)


C_ROW_SPLIT = 2


def _na_kernel(q_ref, k_ref, v_ref, b_ref, o_ref):
    k = k_ref[...]
    v = v_ref[...]
    rb = C_TQ // C_ROW_SPLIT
    units = [slice(u * rb, (u + 1) * rb) for u in range(C_ROW_SPLIT)]
    scores = [lax.dot_general(q_ref[rows, :], k, (((1,), (1,)), ((), ())), preferred_element_type=F32)
              * (HEAD_DIM ** -0.5) + b_ref[rows, :] for rows in units]
    soft = []
    for s in scores:
        m = jnp.max(s, axis=-1, keepdims=True)
        p = jnp.exp(s - m)
        soft.append((p.astype(BF16), jnp.sum(p, axis=-1, keepdims=True)))
    for rows, (p, l) in zip(units, soft):
        o_ref[rows, :] = (jnp.dot(p, v, preferred_element_type=F32) / l).astype(o_ref.dtype)


def _neighbourhood(proj, rpb, lay):
    t = proj.shape[0]
    r_p, r_s = lay.s_p // GRID_W, lay.s_s // GRID_W
    assert r_p >= C_KROWS and r_s >= C_KROWS and r_p % C_QROWS == 0 and r_s % C_QROWS == 0
    blk_p = lay.t_p // C_TQ
    bias = _na_bias(rpb)

    def window(i):
        in_p = i < blk_p
        per_seq = jnp.where(in_p, lay.s_p // C_TQ, lay.s_s // C_TQ)
        rows = jnp.where(in_p, r_p, r_s)
        il = jnp.where(in_p, i, i - blk_p)
        seq0 = (i - il % per_seq) * C_TQ
        r0 = (il % per_seq) * C_QROWS
        w0 = jnp.clip(r0 - NA_ROWS // 2, 0, rows - C_KROWS)
        return seq0 + w0 * GRID_W, (r0 - w0) // (NA_ROWS // 2)

    def kv_spec(off):
        return pl.BlockSpec((pl.Element(C_TK), pl.Element(HEAD_DIM)),
                            lambda h, i: (pl.multiple_of(window(i)[0], GRID_W),
                                          pl.multiple_of(off + h * HEAD_DIM, LANES)))

    return pl.pallas_call(
        _na_kernel,
        grid=(C_HEADS, t // C_TQ),
        in_specs=[pl.BlockSpec((C_TQ, HEAD_DIM), lambda h, i: (i, OFF_CQ // HEAD_DIM + h)),
                  kv_spec(OFF_CK), kv_spec(OFF_CV),
                  pl.BlockSpec((None, None, C_TQ, C_TK), lambda h, i: (window(i)[1], h, 0, 0))],
        out_specs=pl.BlockSpec((C_TQ, HEAD_DIM), lambda h, i: (i, h)),
        out_shape=jax.ShapeDtypeStruct((t, C_OUT), BF16),
        compiler_params=_cparams("parallel", "parallel"),
        name="c_neighbourhood",
    )(proj, proj, proj, bias)


RET_C = 256


def _log_sigmoid(x):
    return jnp.minimum(x, 0.0) - jnp.log(1.0 + jnp.exp(-jnp.abs(x)))


def _ret_qk(q_ref, k_ref, cos_ref, sin_ref, h):
    sl = slice(h * D_QK, (h + 1) * D_QK)
    cos, sin = cos_ref[...], sin_ref[...]
    q = _rope_rotate(q_ref[:, sl].astype(F32), cos, sin)
    k = _rope_rotate(k_ref[:, sl].astype(F32), cos, sin) * (D_QK ** -0.5)
    return q, k


def _ret_bwd_kernel(cb_ref, first_ref, q_ref, k_ref, v_ref, cos_ref, sin_ref, dl_ref, o_ref, st_ref):
    s_id = pl.program_id(0)

    @pl.when(first_ref[s_id] == 1)
    def _():
        st_ref[...] = jnp.zeros(st_ref.shape, F32)

    row = lax.broadcasted_iota(I32, (RET_C, 1), 0).astype(F32)
    for h in range(D_HEADS):
        lg = _log_sigmoid(dl_ref[D_HEADS + h:D_HEADS + h + 1, :])
        q, k = _ret_qk(q_ref, k_ref, cos_ref, sin_ref, h)
        v = v_ref[:, h * D_V:(h + 1) * D_V]
        st = st_ref[h]
        q_dec = (q * jnp.exp((RET_C - row) * lg)).astype(BF16)
        o_ref[:, h * D_V:(h + 1) * D_V] = jnp.dot(q_dec, st.astype(BF16), preferred_element_type=F32)
        k_dec = (k * jnp.exp(row * lg)).astype(BF16)
        kv = lax.dot_general(k_dec, v, (((0,), (0,)), ((), ())), preferred_element_type=F32)
        st_ref[h] = st * jnp.exp(RET_C * lg[:, 0:1]) + kv


def _ret_fwd_kernel(cb_ref, first_ref, q_ref, k_ref, v_ref, g_ref, xb_ref, cos_ref, sin_ref, dl_ref, o_ref, st_ref):
    s_id = pl.program_id(0)

    @pl.when(first_ref[s_id] == 1)
    def _():
        st_ref[...] = jnp.zeros(st_ref.shape, F32)

    row = lax.broadcasted_iota(I32, (RET_C, 1), 0).astype(F32)
    diff = (lax.broadcasted_iota(I32, (RET_C, RET_C), 0) - lax.broadcasted_iota(I32, (RET_C, RET_C), 1)).astype(F32)
    for h in range(D_HEADS):
        lgf = _log_sigmoid(dl_ref[h:h + 1, :])
        lgb = _log_sigmoid(dl_ref[D_HEADS + h:D_HEADS + h + 1, :])
        q, k = _ret_qk(q_ref, k_ref, cos_ref, sin_ref, h)
        v = v_ref[:, h * D_V:(h + 1) * D_V]
        st = st_ref[h]
        dmat = jnp.where(diff >= 0, jnp.exp(jnp.maximum(diff, 0.0) * lgf[:, 0:1]),
                         jnp.exp(jnp.maximum(-diff, 0.0) * lgb[:, 0:1]))
        s = lax.dot_general(q.astype(BF16), k.astype(BF16), (((1,), (1,)), ((), ())), preferred_element_type=F32)
        o = jnp.dot((s * dmat).astype(BF16), v, preferred_element_type=F32)
        q_dec = (q * jnp.exp((row + 1.0) * lgf)).astype(BF16)
        o = o + jnp.dot(q_dec, st.astype(BF16), preferred_element_type=F32)
        o = o + xb_ref[:, h * D_V:(h + 1) * D_V]
        k_dec = (k * jnp.exp((RET_C - 1.0 - row) * lgf)).astype(BF16)
        kv = lax.dot_general(k_dec, v, (((0,), (0,)), ((), ())), preferred_element_type=F32)
        st_ref[h] = st * jnp.exp(RET_C * lgf[:, 0:1]) + kv
        mu = jnp.mean(o, axis=-1, keepdims=True)
        var = jnp.mean(jnp.square(o - mu), axis=-1, keepdims=True)
        on = (o - mu) * lax.rsqrt(var + EPS)
        g = g_ref[:, h * D_V:(h + 1) * D_V].astype(F32)
        o_ref[:, h * D_V:(h + 1) * D_V] = (g * jax.nn.sigmoid(g) * on).astype(o_ref.dtype)


def _ret_tables(lay, reverse):
    cb, first = [], []
    for (n, s, base) in ((lay.n_p, lay.s_p, 0), (lay.n_s, lay.s_s, lay.t_p)):
        for b in range(n):
            nc = s // RET_C
            order = range(nc - 1, -1, -1) if reverse else range(nc)
            for j, c in enumerate(order):
                cb.append((base + b * s) // RET_C + c)
                first.append(int(j == 0))
    return jnp.asarray(cb, dtype=I32), jnp.asarray(first, dtype=I32)


def _retention(proj, dlogit, cos, sin, lay):
    t = proj.shape[0]
    dl = jnp.broadcast_to(dlogit.astype(F32).reshape(2 * D_HEADS, 1), (2 * D_HEADS, LANES))
    qw, vw = D_HEADS * D_QK, D_HEADS * D_V
    row_spec = lambda width, off: pl.BlockSpec((pl.Element(RET_C), pl.Element(width)),
                                               lambda s, cb, f: (pl.multiple_of(cb[s] * RET_C, RET_C), off))
    tab_spec = pl.BlockSpec((RET_C, LANES), lambda s, cb, f: (cb[s], 0))
    dl_spec = pl.BlockSpec((2 * D_HEADS, LANES), lambda s, cb, f: (0, 0))
    out_spec = pl.BlockSpec((RET_C, vw), lambda s, cb, f: (cb[s], 0))
    state = pltpu.VMEM((D_HEADS, D_QK, D_V), F32)

    cb, first = _ret_tables(lay, True)
    xb = pl.pallas_call(
        _ret_bwd_kernel,
        grid_spec=pltpu.PrefetchScalarGridSpec(
            num_scalar_prefetch=2, grid=(cb.shape[0],),
            in_specs=[row_spec(qw, OFF_DQ), row_spec(qw, OFF_DK), row_spec(vw, OFF_DV), tab_spec, tab_spec, dl_spec],
            out_specs=out_spec, scratch_shapes=[state]),
        out_shape=jax.ShapeDtypeStruct((t, vw), F32),
        compiler_params=_cparams("arbitrary"),
        name="d_retention_bwd",
    )(cb, first, proj, proj, proj, cos, sin, dl)

    cb, first = _ret_tables(lay, False)
    return pl.pallas_call(
        _ret_fwd_kernel,
        grid_spec=pltpu.PrefetchScalarGridSpec(
            num_scalar_prefetch=2, grid=(cb.shape[0],),
            in_specs=[row_spec(qw, OFF_DQ), row_spec(qw, OFF_DK), row_spec(vw, OFF_DV), row_spec(vw, OFF_DG),
                      out_spec, tab_spec, tab_spec, dl_spec],
            out_specs=out_spec, scratch_shapes=[state]),
        out_shape=jax.ShapeDtypeStruct((t, vw), BF16),
        compiler_params=_cparams("arbitrary"),
        name="d_retention_fwd",
    )(cb, first, proj, proj, proj, proj, xb, cos, sin, dl)


def _merge_kernel(oa0, oa1, oa2, ls0, ls1, ls2, ob, oc, od, g0, g1, g2, g3, wa, wb, wc, wd, o_ref, oa_ref):
    @pl.when(pl.program_id(1) == 0)
    def _():
        l0, l1, l2 = ls0[...], ls1[...], ls2[...]
        m = jnp.maximum(jnp.maximum(l0, l1), l2)
        e0, e1, e2 = jnp.exp(l0 - m), jnp.exp(l1 - m), jnp.exp(l2 - m)
        num = e0 * oa0[...].astype(F32) + e1 * oa1[...].astype(F32) + e2 * oa2[...].astype(F32)
        oa_ref[...] = (num / (e0 + e1 + e2)).astype(BF16)

    def term(gate, o, w):
        return jax.nn.sigmoid(gate[...].astype(F32)) * jnp.dot(o, w[...], preferred_element_type=F32)

    acc = term(g0, oa_ref[...], wa) + term(g1, ob[...], wb) + term(g2, oc[...], wc) + term(g3, od[...], wd)
    o_ref[...] = acc.astype(o_ref.dtype)


def _branch_merge(proj, a_parts, o_b, o_c, o_d, w_branch_bf16):
    t = proj.shape[0]
    tm, tn = 512, 1024
    (oa0, ls0), (oa1, ls1), (oa2, ls2) = a_parts
    row = lambda width: pl.BlockSpec((tm, width), lambda i, j: (i, 0))
    gate = lambda b: pl.BlockSpec((pl.Element(tm), pl.Element(tn)),
                                  lambda i, j: (pl.multiple_of(i * tm, tm),
                                                pl.multiple_of(OFF_GATE + b * D_MODEL + j * tn, LANES)))
    wspec = lambda width: pl.BlockSpec((width, tn), lambda i, j: (0, j))
    offs = (0, A_OUT, A_OUT + B_OUT, A_OUT + B_OUT + C_OUT, MIX_WIDTH)
    ws = [w_branch_bf16[offs[b]:offs[b + 1]] for b in range(N_BRANCH)]
    return pl.pallas_call(
        _merge_kernel,
        grid=(t // tm, D_MODEL // tn),
        in_specs=[row(A_OUT)] * 6 + [row(B_OUT), row(C_OUT), row(D_OUT)]
                 + [gate(0), gate(1), gate(2), gate(3)]
                 + [wspec(A_OUT), wspec(B_OUT), wspec(C_OUT), wspec(D_OUT)],
        out_specs=pl.BlockSpec((tm, tn), lambda i, j: (i, j)),
        out_shape=jax.ShapeDtypeStruct((t, D_MODEL), BF16),
        scratch_shapes=[pltpu.VMEM((tm, A_OUT), BF16)],
        compiler_params=_cparams("parallel", "arbitrary"),
        name="branch_merge",
    )(oa0, oa1, oa2, ls0, ls1, ls2, o_b, o_c, o_d, proj, proj, proj, proj, *ws)


def _split3(x):
    hi = x.astype(BF16)
    lo = (x - hi.astype(F32)).astype(BF16)
    return hi, lo


def _outproj_kernel(mg_ref, x_ref, mod_ref, g_ref, w_ref, wr_hi, wr_lo, br_ref, xo_ref, h2_ref, lg_ref):
    y = jnp.dot(mg_ref[...], w_ref[...], preferred_element_type=F32)
    m = mod_ref[0]
    g = g_ref[...]
    yn = y * lax.rsqrt(jnp.mean(y * y, axis=-1, keepdims=True) + EPS) * g[1:2]
    x = x_ref[...] + m[2:3] * yn
    xo_ref[...] = x
    h2 = x * lax.rsqrt(jnp.mean(x * x, axis=-1, keepdims=True) + EPS) * g[2:3] * (1.0 + m[4:5]) + m[3:4]
    h2b = h2.astype(BF16)
    half = D_MODEL // 2
    lo = lax.shift_right_logical(pltpu.bitcast(h2b[:, :half].astype(F32), jnp.uint32), jnp.uint32(16))
    hi = pltpu.bitcast(h2b[:, half:].astype(F32), jnp.uint32)
    h2_ref[...] = hi | lo
    lg_ref[...] = (jnp.dot(h2b, wr_hi[...], preferred_element_type=F32)
                   + jnp.dot(h2b, wr_lo[...], preferred_element_type=F32) + br_ref[...])


def _outproj_residual(merged, x, modseg, gains, w_out_bf16, w_router, b_router):
    t = x.shape[0]
    tm = 256
    wr = jnp.pad(w_router.astype(F32), ((0, 0), (0, LANES - N_EXPERTS)))
    wr_hi, wr_lo = _split3(wr)
    br = jnp.pad(b_router.astype(F32), (0, LANES - N_EXPERTS)).reshape(1, LANES)
    gains8 = jnp.pad(gains.astype(F32), ((0, 4), (0, 0)))
    row = lambda width: pl.BlockSpec((tm, width), lambda i: (i, 0))
    full = lambda a, b: pl.BlockSpec((a, b), lambda i: (0, 0))
    return pl.pallas_call(
        _outproj_kernel,
        grid=(t // tm,),
        in_specs=[row(D_MODEL), row(D_MODEL),
                  pl.BlockSpec((1, 8, D_MODEL), lambda i: (i * tm // SEG, 0, 0)),
                  full(8, D_MODEL), full(D_MODEL, D_MODEL), full(D_MODEL, LANES), full(D_MODEL, LANES), full(1, LANES)],
        out_specs=[row(D_MODEL), row(D_MODEL // 2), row(LANES)],
        out_shape=[jax.ShapeDtypeStruct((t, D_MODEL), F32), jax.ShapeDtypeStruct((t, D_MODEL // 2), jnp.uint32),
                   jax.ShapeDtypeStruct((t, LANES), F32)],
        compiler_params=_cparams("parallel"),
        name="outproj_residual",
    )(merged, x, modseg, gains8, w_out_bf16, wr_hi, wr_lo, br)


R_TM = 512


def _route_kernel(lg_ref, ti_ref, tw_ref, rk_ref, cnt_ref, carry_ref):
    @pl.when(pl.program_id(0) == 0)
    def _():
        carry_ref[...] = jnp.zeros(carry_ref.shape, F32)

    lane = _lane_iota((R_TM, LANES))
    l = jnp.where(lane < N_EXPERTS, lg_ref[...], -jnp.inf)
    vals, idxs = [], []
    for _ in range(TOP_K):
        m = jnp.max(l, axis=-1, keepdims=True)
        idx = jnp.min(jnp.where(l == m, lane.astype(F32), float(LANES)), axis=-1, keepdims=True).astype(I32)
        vals.append(m)
        idxs.append(idx)
        l = jnp.where(lane == idx, -jnp.inf, l)
    es = [jnp.exp(v - vals[0]) for v in vals]
    den = es[0] + es[1] + es[2] + es[3]
    ti = jnp.zeros((R_TM, LANES), I32)
    tw = jnp.zeros((R_TM, LANES), F32)
    cnt = jnp.zeros((R_TM, LANES), F32)
    for k in range(TOP_K):
        ti = jnp.where(lane == k, idxs[k], ti)
        tw = jnp.where(lane == k, es[k] / den, tw)
        cnt = cnt + jnp.where(lane == idxs[k], 1.0, 0.0)
    ti_ref[...] = ti
    tw_ref[...] = tw
    r = lax.broadcasted_iota(I32, (R_TM, R_TM), 0)
    c = lax.broadcasted_iota(I32, (R_TM, R_TM), 1)
    tri = jnp.where(c < r, 1.0, 0.0).astype(BF16)
    before = jnp.dot(tri, cnt.astype(BF16), preferred_element_type=F32) + carry_ref[...]
    rk = jnp.zeros((R_TM, LANES), I32)
    for k in range(TOP_K):
        pos = jnp.sum(jnp.where(lane == idxs[k], before, 0.0), axis=-1, keepdims=True)
        rk = jnp.where(lane == k, pos.astype(I32), rk)
    rk_ref[...] = rk
    carry_ref[...] = carry_ref[...] + jnp.sum(cnt, axis=0, keepdims=True)
    cnt_ref[...] = jnp.broadcast_to(carry_ref[...], cnt_ref.shape)


def _route(logits):
    t = logits.shape[0]
    row = pl.BlockSpec((R_TM, LANES), lambda i: (i, 0))
    return pl.pallas_call(
        _route_kernel,
        grid=(t // R_TM,),
        in_specs=[row],
        out_specs=[row, row, row, pl.BlockSpec((8, LANES), lambda i: (0, 0))],
        out_shape=[jax.ShapeDtypeStruct((t, LANES), I32), jax.ShapeDtypeStruct((t, LANES), F32),
                   jax.ShapeDtypeStruct((t, LANES), I32), jax.ShapeDtypeStruct((8, LANES), F32)],
        scratch_shapes=[pltpu.VMEM((1, LANES), F32)],
        compiler_params=_cparams("arbitrary"),
        name="moe_route",
    )(logits)


E_TM = 512
DISP_TT = 512
COMB_TT = 512


ROW_DMA_UNROLL = 8


def _row_dma_burst(copy, n, bulk_waits=None):
    per_trip = ROW_DMA_UNROLL // TOP_K

    def start(g, c):
        for u in range(ROW_DMA_UNROLL):
            copy(g * ROW_DMA_UNROLL + u, g * per_trip + u // TOP_K, u % TOP_K).start(priority=u % 2)
        return c

    def wait(g, c):
        for u in range(ROW_DMA_UNROLL):
            copy(g * ROW_DMA_UNROLL + u, g * per_trip + u // TOP_K, u % TOP_K).wait()
        return c

    lax.fori_loop(0, n // ROW_DMA_UNROLL, start, 0)
    if bulk_waits is None:
        lax.fori_loop(0, n // ROW_DMA_UNROLL, wait, 0)
    else:
        for w in bulk_waits:
            w.wait()


def _dispatch_kernel(pos_ref, h_ref, xs_in, xs_ref, sem):
    del xs_in

    def copy(a, token, k):
        del k
        return pltpu.make_async_copy(h_ref.at[pl.ds(token, 1)], xs_ref.at[pl.ds(pos_ref[a], 1)], sem)

    _row_dma_burst(copy, DISP_TT * TOP_K)


def _dispatch(pos_flat, h2p, n_rows):
    t, width = h2p.shape
    xs0 = jnp.zeros((n_rows, width), h2p.dtype)
    return pl.pallas_call(
        _dispatch_kernel,
        grid=(t // DISP_TT,),
        in_specs=[pl.BlockSpec((DISP_TT * TOP_K,), lambda i: (i,), memory_space=pltpu.SMEM),
                  pl.BlockSpec((DISP_TT, width), lambda i: (i, 0)),
                  pl.BlockSpec(memory_space=pl.ANY)],
        out_specs=pl.BlockSpec(memory_space=pl.ANY),
        scratch_shapes=[pltpu.SemaphoreType.DMA(())],
        out_shape=jax.ShapeDtypeStruct((n_rows, width), h2p.dtype),
        input_output_aliases={2: 0},
        compiler_params=_cparams("arbitrary"),
        name="moe_dispatch",
    )(pos_flat, h2p, xs0)


UP_TN = 1024
DN_TN = 1024
SEL_K = 256


def _weights_changed(te_ref, i):
    return (i == 0) | (te_ref[i] != te_ref[jnp.maximum(i - 1, 0)])


def _up_kernel(te_ref, nu_ref, x_ref, w_ref, bg_ref, bl_ref, sel_ref, o_ref, wg_ref, wl_ref):
    i = pl.program_id(1)

    @pl.when(i < nu_ref[0])
    def _():
        @pl.when(_weights_changed(te_ref, i))
        def _():
            for c in range(UP_TN // SEL_K):
                wc = w_ref[0, :, c * SEL_K:(c + 1) * SEL_K].astype(BF16)
                cols = slice(c * (SEL_K // 2), (c + 1) * (SEL_K // 2))
                both = jnp.dot(wc, sel_ref[...], preferred_element_type=F32).astype(BF16)
                wg_ref[:, cols] = both[:, :SEL_K // 2]
                wl_ref[:, cols] = both[:, SEL_K // 2:]

        xw = x_ref[...]
        x = jnp.concatenate([pltpu.bitcast(lax.shift_left(xw, jnp.uint32(16)), F32).astype(BF16),
                             pltpu.bitcast(xw & jnp.uint32(0xFFFF0000), F32).astype(BF16)], axis=1)
        glu = jnp.dot(x, wg_ref[...], preferred_element_type=F32) + bg_ref[0]
        lin = jnp.dot(x, wl_ref[...], preferred_element_type=F32) + bl_ref[0]
        glu = jnp.minimum(glu, SWIGLU_LIMIT)
        lin = jnp.clip(lin, -SWIGLU_LIMIT, SWIGLU_LIMIT)
        o_ref[...] = (glu * jax.nn.sigmoid(SWIGLU_ALPHA * glu) * (lin + 1.0)).astype(o_ref.dtype)


def _expert_up(tile_e, n_used, xs, w_up, b_up, layer):
    n_rows = xs.shape[0]
    pick = jnp.arange(SEL_K)[:, None] - 2 * jnp.arange(SEL_K // 2)[None, :]
    sel = jnp.concatenate([pick == 0, pick == 1], axis=1).astype(BF16)
    bu = b_up[layer].astype(F32)
    bg, bl = bu[:, None, 0::2], bu[:, None, 1::2]
    half_tn = UP_TN // 2
    bspec = pl.BlockSpec((1, 1, half_tn), lambda j, i, te, nu: (te[i], 0, j))
    return pl.pallas_call(
        _up_kernel,
        grid_spec=pltpu.PrefetchScalarGridSpec(
            num_scalar_prefetch=2, grid=(2 * D_FF // UP_TN, n_rows // E_TM),
            in_specs=[pl.BlockSpec((E_TM, D_MODEL // 2), lambda j, i, te, nu: (i, 0)),
                      pl.BlockSpec((None, 1, D_MODEL, UP_TN), lambda j, i, te, nu: (layer, te[i], 0, j)),
                      bspec, bspec,
                      pl.BlockSpec((SEL_K, SEL_K), lambda j, i, te, nu: (0, 0))],
            out_specs=pl.BlockSpec((E_TM, half_tn), lambda j, i, te, nu: (i, j)),
            scratch_shapes=[pltpu.VMEM((D_MODEL, half_tn), BF16), pltpu.VMEM((D_MODEL, half_tn), BF16)]),
        out_shape=jax.ShapeDtypeStruct((n_rows, D_FF), BF16),
        compiler_params=_cparams("arbitrary", "arbitrary"),
        name="moe_up",
    )(tile_e, n_used, xs, w_up, bg, bl, sel)


def _down_kernel(te_ref, nu_ref, a_ref, w_ref, b_ref, o_ref, wbf_ref):
    i = pl.program_id(1)

    @pl.when(i < nu_ref[0])
    def _():
        @pl.when(_weights_changed(te_ref, i))
        def _():
            wbf_ref[...] = w_ref[0].astype(BF16)

        o_ref[...] = jnp.dot(a_ref[...], wbf_ref[...], preferred_element_type=F32) + b_ref[0]


def _expert_down(tile_e, n_used, act, w_down, b_down, layer):
    n_rows = act.shape[0]
    return pl.pallas_call(
        _down_kernel,
        grid_spec=pltpu.PrefetchScalarGridSpec(
            num_scalar_prefetch=2, grid=(D_MODEL // DN_TN, n_rows // E_TM),
            in_specs=[pl.BlockSpec((E_TM, D_FF), lambda j, i, te, nu: (i, 0)),
                      pl.BlockSpec((None, 1, D_FF, DN_TN), lambda j, i, te, nu: (layer, te[i], 0, j)),
                      pl.BlockSpec((None, 1, 1, DN_TN), lambda j, i, te, nu: (layer, te[i], 0, j))],
            out_specs=pl.BlockSpec((E_TM, DN_TN), lambda j, i, te, nu: (i, j)),
            scratch_shapes=[pltpu.VMEM((D_FF, DN_TN), BF16)]),
        out_shape=jax.ShapeDtypeStruct((n_rows, D_MODEL), F32),
        compiler_params=_cparams("arbitrary", "arbitrary"),
        name="moe_down",
    )(tile_e, n_used, act, w_down, b_down.reshape(DEPTH, N_EXPERTS, 1, D_MODEL))


def _combine_kernel(pos_ref, ys_ref, tw_ref, x_ref, mod_ref, g_ref, *rest, split_blk):
    o_refs, (buf, sem) = rest[:-2], rest[-2:]

    def copy(a, token, k):
        return pltpu.make_async_copy(ys_ref.at[pl.ds(pos_ref[a], 1)], buf.at[k, pl.ds(token, 1)], sem)

    _row_dma_burst(copy, COMB_TT * TOP_K,
                   bulk_waits=[pltpu.make_async_copy(buf.at[k], buf.at[k], sem) for k in range(TOP_K)])
    tw = tw_ref[...]
    f = tw[:, 0:1] * buf[0]
    for k in range(1, TOP_K):
        f = f + tw[:, k:k + 1] * buf[k]
    m = mod_ref[0]
    fn = f * lax.rsqrt(jnp.mean(f * f, axis=-1, keepdims=True) + EPS) * g_ref[3:4]
    out = x_ref[...] + m[5:6] * fn
    if split_blk is None:
        o_refs[0][...] = out
    else:
        @pl.when(pl.program_id(0) < split_blk)
        def _():
            o_refs[0][...] = out

        @pl.when(pl.program_id(0) >= split_blk)
        def _():
            o_refs[1][...] = out


def _combine(pos_flat, ys, tw, x, modseg, gains, split_rows=None):
    t = x.shape[0]
    gains8 = jnp.pad(gains.astype(F32), ((0, 4), (0, 0)))
    if split_rows is None:
        split_blk = None
        out_specs = pl.BlockSpec((COMB_TT, D_MODEL), lambda i: (i, 0))
        out_shape = jax.ShapeDtypeStruct((t, D_MODEL), F32)
    else:
        split_blk = split_rows // COMB_TT
        assert split_rows % COMB_TT == 0 and 0 < split_blk < t // COMB_TT
        out_specs = [pl.BlockSpec((COMB_TT, D_MODEL), lambda i: (jnp.minimum(i, split_blk - 1), 0)),
                     pl.BlockSpec((COMB_TT, D_MODEL), lambda i: (jnp.maximum(i - split_blk, 0), 0))]
        out_shape = [jax.ShapeDtypeStruct((split_rows, D_MODEL), F32),
                     jax.ShapeDtypeStruct((t - split_rows, D_MODEL), F32)]
    return pl.pallas_call(
        functools.partial(_combine_kernel, split_blk=split_blk),
        grid=(t // COMB_TT,),
        in_specs=[pl.BlockSpec((COMB_TT * TOP_K,), lambda i: (i,), memory_space=pltpu.SMEM),
                  pl.BlockSpec(memory_space=pl.ANY),
                  pl.BlockSpec((COMB_TT, LANES), lambda i: (i, 0)),
                  pl.BlockSpec((COMB_TT, D_MODEL), lambda i: (i, 0)),
                  pl.BlockSpec((1, 8, D_MODEL), lambda i: (i * COMB_TT // SEG, 0, 0)),
                  pl.BlockSpec((8, D_MODEL), lambda i: (0, 0))],
        out_specs=out_specs,
        scratch_shapes=[pltpu.VMEM((TOP_K, COMB_TT, D_MODEL), F32), pltpu.SemaphoreType.DMA(())],
        out_shape=out_shape,
        compiler_params=_cparams("arbitrary"),
        name="moe_combine",
    )(pos_flat, ys, tw, x, modseg, gains8)


def _moe(h2p, logits, x, modseg, gains, w_up, b_up, w_down, b_down, layer, split_rows=None):
    t = h2p.shape[0]
    ti, tw, rk, cnt = _route(logits)
    counts = cnt[0, :N_EXPERTS].astype(I32)
    padded = (counts + E_TM - 1) // E_TM * E_TM
    upto = jnp.arange(N_EXPERTS)[None, :] <= jnp.arange(N_EXPERTS)[:, None]
    ends = jnp.sum(jnp.where(upto, padded[None, :], 0), axis=1).astype(I32)
    offsets = ends - padded
    n_tiles = t * TOP_K // E_TM + N_EXPERTS
    starts = jnp.arange(n_tiles, dtype=I32) * E_TM
    tile_e = jnp.minimum(jnp.sum((ends[None, :] <= starts[:, None]).astype(I32), axis=1), N_EXPERTS - 1)
    n_used = (ends[-1:] // E_TM).astype(I32)
    ti4, rk4 = ti[:, :TOP_K], rk[:, :TOP_K]
    first_row = jnp.sum(jnp.where(ti4[:, :, None] == jnp.arange(N_EXPERTS)[None, None, :], offsets[None, None, :], 0), axis=-1)
    pos_flat = (first_row + rk4).astype(I32).reshape(-1)
    xs = _dispatch(pos_flat, h2p, n_tiles * E_TM)
    act = _expert_up(tile_e, n_used, xs, w_up, b_up, layer)
    ys = _expert_down(tile_e, n_used, act, w_down, b_down, layer)
    return _combine(pos_flat, ys, tw, x, modseg, gains, split_rows)


def _forward(x, c8, seg_rows, lay, t5_bias, w_mod, b_mod, norm_gains, w_in, qk_norm_gains, na_rpb,
             ret_decay_logit, w_branch, w_out, w_router, b_router, w_up, b_up, w_down, b_down):
    mod = _modulation(c8, w_mod, b_mod)
    modseg = mod[:, seg_rows, :].reshape(DEPTH, len(seg_rows), 6, D_MODEL)
    modseg = jnp.pad(modseg, ((0, 0), (0, 0), (0, 2), (0, 0)))
    cos_a, sin_a = _axial_tables(lay)
    cos_r, sin_r = _rope_tables(lay)
    dil_bias = [_dil_bias(t5_bias, g, d, w // (2 * d)) for g, (w, d) in enumerate(DIL_CONFIGS)]
    for l in range(DEPTH):
        gains = norm_gains[l]
        proj = _norm_inproj(x, modseg[l], gains[0], w_in[l].astype(BF16))
        a_parts = [_dilated_group(proj, dil_bias[g], lay, g) for g in range(A_GROUPS)]
        o_b = _axial_gqa(proj, _b_prepare(proj, qk_norm_gains[l], cos_a, sin_a), lay)
        o_c = _neighbourhood(proj, na_rpb[l], lay)
        o_d = _retention(proj, ret_decay_logit[l], cos_r, sin_r, lay)
        merged = _branch_merge(proj, a_parts, o_b, o_c, o_d, w_branch[l].astype(BF16))
        x, h2p, logits = _outproj_residual(merged, x, modseg[l], gains, w_out[l].astype(BF16), w_router[l], b_router[l])
        x = _moe(h2p, logits, x, modseg[l], gains, w_up, b_up, w_down, b_down, l,
                 split_rows=lay.t_p if l == DEPTH - 1 else None)
    return x


def kernel(x_prompt, x_sample, c_prompt, c_sample, t5_bias, w_mod, b_mod, norm_gains, w_in, qk_norm_gains,
           na_rpb, ret_decay_logit, w_branch, w_out, w_router, b_router, w_up, b_up, w_down, b_down):
    n_p, s_p, _ = x_prompt.shape
    n_s, s_s, _ = x_sample.shape
    lay = Layout(n_p, s_p, n_s, s_s)
    assert s_p % SEG == 0 and s_s % SEG == 0 and n_p + n_s <= 8
    x = jnp.concatenate([x_prompt.reshape(-1, D_MODEL), x_sample.reshape(-1, D_MODEL)], axis=0)
    c8 = jnp.concatenate([c_prompt, c_sample, jnp.zeros((8 - n_p - n_s, D_MODEL), F32)], axis=0)
    seg_rows = tuple([b for b in range(n_p) for _ in range(s_p // SEG)]
                     + [n_p + b for b in range(n_s) for _ in range(s_s // SEG)])
    y_p, y_s = _forward(x, c8, jnp.asarray(seg_rows, dtype=I32), lay, t5_bias, w_mod, b_mod, norm_gains, w_in,
                        qk_norm_gains, na_rpb, ret_decay_logit, w_branch, w_out, w_router, b_router,
                        w_up, b_up, w_down, b_down)
    return (y_p.reshape(n_p, s_p, D_MODEL), y_s.reshape(n_s, s_s, D_MODEL))
```
